```python
import jax, jax.numpy as jnp
from jax import lax
import numpy as np

D_MODEL = 1024
BATCH = 8
SEQ = 2048
DEPTH = 1
DEC_BATCH = 128
DEC_SEQ = 1
PAST_LEN = 16384
PAGE_SIZE = 128

D_RNN = D_MODEL
RNN_HEADS = 16
RNN_BLOCK = D_RNN // RNN_HEADS
CONV_W = 4
RG_C = 8.0
SSD_EXPAND = 2
D_INNER = SSD_EXPAND * D_MODEL
SSD_HEAD_DIM = 64
SSD_HEADS = D_INNER // SSD_HEAD_DIM
SSD_GROUPS = 4
HEADS_PER_GROUP = SSD_HEADS // SSD_GROUPS
D_STATE = 128
SSD_CONV_DIM = D_INNER + 2 * SSD_GROUPS * D_STATE
CHUNK = 128
N_EXPERTS = 32
TOP_K = 4
D_EXPERT = D_MODEL
SWIGLU_LIMIT = 7.0
SWIGLU_ALPHA = 1.702
EXPERT_BLOCK = 128
EPS = 1e-6

IN_SPLITS = (D_RNN,
             2 * D_RNN,
             2 * D_RNN + D_INNER,
             2 * D_RNN + D_INNER + SSD_CONV_DIM,
             2 * D_RNN + D_INNER + SSD_CONV_DIM + SSD_HEADS,
             2 * D_RNN + D_INNER + SSD_CONV_DIM + SSD_HEADS + D_MODEL)
N_IN = IN_SPLITS[-1] + D_MODEL

kernel_name = 'hybrid_rglru_ssd_moe_step'


def rmsnorm(x, g):
    xf = x.astype(jnp.float32)
    y = xf * lax.rsqrt(jnp.mean(xf * xf, axis=-1, keepdims=True) + EPS)
    return (y * g.astype(jnp.float32)).astype(x.dtype)


def causal_conv(x, buf, w, b):
    xx = jnp.concatenate([buf.astype(x.dtype), x], axis=1)
    c = x.shape[-1]
    y = lax.conv_general_dilated(xx, w[:, None, :].astype(x.dtype), window_strides=(1,), padding='VALID',
                                 dimension_numbers=('NWC', 'WIO', 'NWC'), feature_group_count=c)
    return y + b.astype(x.dtype), xx[:, -(CONV_W - 1):].astype(buf.dtype)


def rg_lru(x, h0, w_rg, b_rg, w_ig, b_ig, lam):
    bsz, L, _ = x.shape
    xb = x.reshape(bsz, L, RNN_HEADS, RNN_BLOCK)
    r = jax.nn.sigmoid(jnp.einsum('blhi,hij->blhj', xb, w_rg) + b_rg).reshape(bsz, L, D_RNN)
    i = jax.nn.sigmoid(jnp.einsum('blhi,hij->blhj', xb, w_ig) + b_ig).reshape(bsz, L, D_RNN)
    log_a = (-RG_C * r.astype(jnp.float32)) * jax.nn.softplus(-lam.astype(jnp.float32))
    a = jnp.exp(log_a)
    mult = jnp.sqrt(-jnp.expm1(2.0 * log_a))
    bt = mult * (i * x).astype(jnp.float32)

    def combine(lhs, rhs):
        a1, b1 = lhs
        a2, b2 = rhs
        return a1 * a2, a2 * b1 + b2

    a_cum, b_cum = lax.associative_scan(combine, (a, bt), axis=1)
    h = a_cum * h0.astype(jnp.float32)[:, None] + b_cum
    return h.astype(x.dtype), h[:, -1].astype(h0.dtype)


def ssd_chunked(x, dt, A, Bm, Cm, s0):
    b, L = x.shape[:2]
    q = min(CHUNK, L)
    nc = -(-L // q)
    pad = nc * q - L
    padf = lambda t: jnp.pad(t, [(0, 0), (0, pad)] + [(0, 0)] * (t.ndim - 2))
    x, dt, Bm, Cm = padf(x), padf(dt), padf(Bm), padf(Cm)
    x = x.reshape(b, nc, q, SSD_GROUPS, HEADS_PER_GROUP, SSD_HEAD_DIM)
    dt = dt.reshape(b, nc, q, SSD_GROUPS, HEADS_PER_GROUP)
    Bm = Bm.reshape(b, nc, q, SSD_GROUPS, D_STATE)
    Cm = Cm.reshape(b, nc, q, SSD_GROUPS, D_STATE)
    a_cum = jnp.cumsum(dt * A.reshape(SSD_GROUPS, HEADS_PER_GROUP), axis=2)
    xdt = x * dt[..., None]
    ac = jnp.moveaxis(a_cum, 2, -1)
    diff = ac[..., :, None] - ac[..., None, :]
    causal = jnp.tril(jnp.ones((q, q), dtype=bool))
    decay = jnp.exp(jnp.where(causal, diff, -jnp.inf))
    cb = jnp.einsum('bcqgn,bckgn->bcgqk', Cm, Bm)
    y_diag = jnp.einsum('bcgqk,bcghqk,bckghp->bcqghp', cb, decay, xdt)
    a_last = a_cum[:, :, -1]
    decay_st = jnp.exp(a_last[:, :, None] - a_cum)
    states = jnp.einsum('bcqgn,bcqgh,bcqghp->bcghpn', Bm, decay_st, xdt)

    def step(carry, inp):
        st, dec = inp
        return carry * dec[..., None, None] + st, carry

    s_init = s0.reshape(b, SSD_GROUPS, HEADS_PER_GROUP, SSD_HEAD_DIM, D_STATE)
    s_fin, prev = lax.scan(step, s_init, (jnp.moveaxis(states, 1, 0), jnp.moveaxis(jnp.exp(a_last), 1, 0)))
    prev = jnp.moveaxis(prev, 0, 1)
    y_off = jnp.einsum('bcqgn,bcghpn,bcqgh->bcqghp', Cm, prev, jnp.exp(a_cum))
    y = (y_diag + y_off).reshape(b, nc * q, SSD_HEADS, SSD_HEAD_DIM)[:, :L]
    return y, s_fin.reshape(b, SSD_HEADS, SSD_HEAD_DIM, D_STATE)


def mixer(u, lru_conv0, lru_h0, ssd_conv0, ssd_s0, w_in, conv_lru_w, conv_lru_b, w_rg, b_rg, w_ig, b_ig,
          lam, w_proj_a, conv_ssd_w, conv_ssd_b, dt_bias, a_log, d_skip, ssd_norm, w_proj_b, w_out):
    b, L, _ = u.shape
    proj = u @ w_in
    lru_x, lru_gate, z, xbc, dt_raw, g_a, g_b = jnp.split(proj, IN_SPLITS, axis=-1)
    xc, lru_conv1 = causal_conv(lru_x, lru_conv0, conv_lru_w, conv_lru_b)
    h, lru_h1 = rg_lru(xc, lru_h0, w_rg, b_rg, w_ig, b_ig, lam)
    branch_a = (jax.nn.gelu(lru_gate) * h) @ w_proj_a
    xbc_c, ssd_conv1 = causal_conv(xbc, ssd_conv0, conv_ssd_w, conv_ssd_b)
    xbc_c = jax.nn.silu(xbc_c).astype(jnp.float32)
    xs, Bm, Cm = jnp.split(xbc_c, [D_INNER, D_INNER + SSD_GROUPS * D_STATE], axis=-1)
    dt = jax.nn.softplus(dt_raw.astype(jnp.float32) + dt_bias.astype(jnp.float32))
    A = -jnp.exp(a_log.astype(jnp.float32))
    xh = xs.reshape(b, L, SSD_HEADS, SSD_HEAD_DIM)
    y, ssd_s1 = ssd_chunked(xh, dt, A, Bm.reshape(b, L, SSD_GROUPS, D_STATE),
                            Cm.reshape(b, L, SSD_GROUPS, D_STATE), ssd_s0.astype(jnp.float32))
    y = y + d_skip.astype(jnp.float32)[:, None] * xh
    y = y.reshape(b, L, D_INNER) * jax.nn.silu(z.astype(jnp.float32))
    yg = y.reshape(b, L, SSD_GROUPS, D_INNER // SSD_GROUPS)
    yg = yg * lax.rsqrt(jnp.mean(yg * yg, axis=-1, keepdims=True) + EPS)
    y = (yg.reshape(b, L, D_INNER) * ssd_norm.astype(jnp.float32)).astype(u.dtype)
    branch_b = y @ w_proj_b
    merged = jax.nn.sigmoid(g_a) * branch_a + jax.nn.sigmoid(g_b) * branch_b
    return merged @ w_out, (lru_conv1, lru_h1, ssd_conv1, ssd_s1.astype(ssd_s0.dtype))


def moe(u, w_router, b_router, w_gate_up, b_gate_up, w_down, b_down):
    b, L, D = u.shape
    xt = u.reshape(-1, D)
    T = xt.shape[0]
    logits = (xt @ w_router + b_router).astype(jnp.float32)
    top_v, top_i = lax.top_k(logits, TOP_K)
    gates = jax.nn.softmax(top_v, axis=-1)
    n_assign = T * TOP_K
    flat_e = top_i.reshape(-1)
    flat_g = gates.reshape(-1)
    flat_tok = jnp.arange(n_assign, dtype=jnp.int32) // TOP_K
    order = jnp.argsort(flat_e)
    sorted_e = flat_e[order]
    counts = jnp.bincount(flat_e, length=N_EXPERTS)
    padded = (counts + EXPERT_BLOCK - 1) // EXPERT_BLOCK * EXPERT_BLOCK
    pend = jnp.cumsum(padded)
    pstart = pend - padded
    ustart = jnp.cumsum(counts) - counts
    dest = pstart[sorted_e] + jnp.arange(n_assign) - ustart[sorted_e]
    n_blocks = -(-n_assign // EXPERT_BLOCK) + N_EXPERTS
    n_rows = n_blocks * EXPERT_BLOCK
    row_tok = jnp.zeros((n_rows,), jnp.int32).at[dest].set(flat_tok[order])
    row_g = jnp.zeros((n_rows,), jnp.float32).at[dest].set(flat_g[order])
    block_e = jnp.minimum(jnp.searchsorted(pend, jnp.arange(n_blocks) * EXPERT_BLOCK, side='right'),
                          N_EXPERTS - 1)
    xr = xt[row_tok].reshape(n_blocks, EXPERT_BLOCK, D)

    def expert_block(args):
        xb, e = args
        gu = xb @ w_gate_up[e] + b_gate_up[e]
        gate, up = gu[:, :D_EXPERT], gu[:, D_EXPERT:]
        gate = jnp.minimum(gate, SWIGLU_LIMIT)
        up = jnp.clip(up, -SWIGLU_LIMIT, SWIGLU_LIMIT)
        act = (up + 1.0) * (gate * jax.nn.sigmoid(SWIGLU_ALPHA * gate))
        return act @ w_down[e] + b_down[e]

    yr = lax.map(expert_block, (xr, block_e)).reshape(n_rows, D)
    y = jnp.zeros((T, D), jnp.float32).at[row_tok].add(yr.astype(jnp.float32) * row_g[:, None])
    return y.astype(u.dtype).reshape(b, L, D)


def setup_inputs(seed: int = 0) -> dict:
    key = jax.random.key(seed)
    ks = iter(jax.random.split(key, 40))
    f32 = jnp.float32
    nrm = lambda shape, scale: jax.random.normal(next(ks), shape, f32) * scale
    s = jax.random.uniform(next(ks), (DEPTH, D_RNN), f32, 0.9, 0.999) ** (1.0 / RG_C)
    lam = jnp.log(s) - jnp.log1p(-s)
    dt0 = jnp.exp(jax.random.uniform(next(ks), (DEPTH, SSD_HEADS), f32, float(np.log(1e-3)), float(np.log(1e-1))))
    dt_bias = dt0 + jnp.log(-jnp.expm1(-dt0))
    a_log = jnp.log(jax.random.uniform(next(ks), (DEPTH, SSD_HEADS), f32, 1.0, 16.0))
    return {
        'x_prompt': nrm((BATCH, SEQ, D_MODEL), 1.0),
        'x_sample': nrm((DEC_BATCH, DEC_SEQ, D_MODEL), 1.0),
        'state_lru_conv': nrm((DEPTH, DEC_BATCH, CONV_W - 1, D_RNN), 1.0),
        'state_lru_h': nrm((DEPTH, DEC_BATCH, D_RNN), 0.5),
        'state_ssd_conv': nrm((DEPTH, DEC_BATCH, CONV_W - 1, SSD_CONV_DIM), 1.0),
        'state_ssd': nrm((DEPTH, DEC_BATCH, SSD_HEADS, SSD_HEAD_DIM, D_STATE), 0.1),
        'norm_mix': 1.0 + nrm((DEPTH, D_MODEL), 0.02),
        'w_in': nrm((DEPTH, D_MODEL, N_IN), D_MODEL ** -0.5),
        'conv_lru_w': nrm((DEPTH, CONV_W, D_RNN), CONV_W ** -0.5),
        'conv_lru_b': nrm((DEPTH, D_RNN), 0.01),
        'w_rg': nrm((DEPTH, RNN_HEADS, RNN_BLOCK, RNN_BLOCK), RNN_BLOCK ** -0.5),
        'b_rg': nrm((DEPTH, RNN_HEADS, RNN_BLOCK), 0.01),
        'w_ig': nrm((DEPTH, RNN_HEADS, RNN_BLOCK, RNN_BLOCK), RNN_BLOCK ** -0.5),
        'b_ig': nrm((DEPTH, RNN_HEADS, RNN_BLOCK), 0.01),
        'lam': lam,
        'w_proj_a': nrm((DEPTH, D_RNN, D_MODEL), D_RNN ** -0.5),
        'conv_ssd_w': nrm((DEPTH, CONV_W, SSD_CONV_DIM), CONV_W ** -0.5),
        'conv_ssd_b': nrm((DEPTH, SSD_CONV_DIM), 0.01),
        'dt_bias': dt_bias,
        'a_log': a_log,
        'd_skip': 1.0 + nrm((DEPTH, SSD_HEADS), 0.02),
        'ssd_norm': 1.0 + nrm((DEPTH, D_INNER), 0.02),
        'w_proj_b': nrm((DEPTH, D_INNER, D_MODEL), D_INNER ** -0.5),
        'w_out': nrm((DEPTH, D_MODEL, D_MODEL), D_MODEL ** -0.5),
        'norm_ffn': 1.0 + nrm((DEPTH, D_MODEL), 0.02),
        'w_router': nrm((DEPTH, D_MODEL, N_EXPERTS), D_MODEL ** -0.5),
        'b_router': nrm((DEPTH, N_EXPERTS), 0.01),
        'w_gate_up': nrm((DEPTH, N_EXPERTS, D_MODEL, 2 * D_EXPERT), D_MODEL ** -0.5),
        'b_gate_up': nrm((DEPTH, N_EXPERTS, 2 * D_EXPERT), 0.01),
        'w_down': nrm((DEPTH, N_EXPERTS, D_EXPERT, D_MODEL), D_EXPERT ** -0.5),
        'b_down': nrm((DEPTH, N_EXPERTS, D_MODEL), 0.01),
        'norm_final': 1.0 + nrm((D_MODEL,), 0.02),
    }


def reference(x_prompt, x_sample, state_lru_conv, state_lru_h, state_ssd_conv, state_ssd,
              norm_mix, w_in, conv_lru_w, conv_lru_b, w_rg, b_rg, w_ig, b_ig, lam, w_proj_a,
              conv_ssd_w, conv_ssd_b, dt_bias, a_log, d_skip, ssd_norm, w_proj_b, w_out,
              norm_ffn, w_router, b_router, w_gate_up, b_gate_up, w_down, b_down, norm_final):
    bp = x_prompt.shape[0]
    xp, xs = x_prompt, x_sample
    p_lc, p_lh, p_sc, p_ss = [], [], [], []
    s_lc, s_lh, s_sc, s_ss = [], [], [], []
    for l in range(DEPTH):
        mix_p = (w_in[l], conv_lru_w[l], conv_lru_b[l], w_rg[l], b_rg[l], w_ig[l], b_ig[l], lam[l],
                 w_proj_a[l], conv_ssd_w[l], conv_ssd_b[l], dt_bias[l], a_log[l], d_skip[l], ssd_norm[l],
                 w_proj_b[l], w_out[l])
        ffn_p = (w_router[l], b_router[l], w_gate_up[l], b_gate_up[l], w_down[l], b_down[l])
        zero_st = (jnp.zeros((bp, CONV_W - 1, D_RNN), state_lru_conv.dtype),
                   jnp.zeros((bp, D_RNN), state_lru_h.dtype),
                   jnp.zeros((bp, CONV_W - 1, SSD_CONV_DIM), state_ssd_conv.dtype),
                   jnp.zeros((bp, SSD_HEADS, SSD_HEAD_DIM, D_STATE), state_ssd.dtype))
        m, (lc, lh, sc, ss) = mixer(rmsnorm(xp, norm_mix[l]), *zero_st, *mix_p)
        xp = xp + m
        xp = xp + moe(rmsnorm(xp, norm_ffn[l]), *ffn_p)
        p_lc.append(lc); p_lh.append(lh); p_sc.append(sc); p_ss.append(ss)
        m, (lc, lh, sc, ss) = mixer(rmsnorm(xs, norm_mix[l]), state_lru_conv[l], state_lru_h[l],
                                    state_ssd_conv[l], state_ssd[l], *mix_p)
        xs = xs + m
        xs = xs + moe(rmsnorm(xs, norm_ffn[l]), *ffn_p)
        s_lc.append(lc); s_lh.append(lh); s_sc.append(sc); s_ss.append(ss)
    y_prompt = rmsnorm(xp, norm_final)
    y_sample = rmsnorm(xs, norm_final)
    return (y_prompt, y_sample,
            jnp.stack(p_lc), jnp.stack(p_lh), jnp.stack(p_sc), jnp.stack(p_ss),
            jnp.stack(s_lc), jnp.stack(s_lh), jnp.stack(s_sc), jnp.stack(s_ss))
```

```python
import functools

import jax
import jax.numpy as jnp
from jax import lax
from jax.experimental import pallas as pl
from jax.experimental.pallas import tpu as pltpu

F32 = jnp.float32
BF16 = jnp.bfloat16

D_MODEL = 1024
D_RNN = 1024
RNN_HEADS = 16
RNN_BLOCK = 64
CONV_W = 4
RG_C = 8.0
D_INNER = 2048
SSD_HEAD_DIM = 64
SSD_HEADS = 32
SSD_GROUPS = 4
D_STATE = 128
SSD_CONV_DIM = D_INNER + 2 * SSD_GROUPS * D_STATE
CHUNK = 128
N_EXPERTS = 32
TOP_K = 4
D_EXPERT = 1024
SWIGLU_LIMIT = 7.0
SWIGLU_ALPHA = 1.702
EPS = 1e-6

LANES = 128
SUBLANES = 8
N_MAIN = 9 * 1024
COL_Z, COL_LRU, COL_XBC, COL_GATE, COL_GA, COL_GB = 0, 2, 1, 6, 7, 8
MOE_BLOCK = 256
VMEM_LIMIT = 48 * 1024 * 1024


def _cparams(sem):
    return pltpu.CompilerParams(dimension_semantics=sem, vmem_limit_bytes=VMEM_LIMIT)


def _sigmoid(x):
    return jax.nn.sigmoid(x)


def _softplus(x):
    return jnp.maximum(x, 0.0) + jnp.log1p(jnp.exp(-jnp.abs(x)))


def _gelu_tanh(x):
    return 0.5 * x * (1.0 + jnp.tanh(0.7978845608028654 * (x + 0.044715 * (x * x * x))))


def _dot(a, b):
    return jnp.dot(a, b, preferred_element_type=F32)


def _dot_exact(a, b):
    return jnp.dot(a, b, preferred_element_type=F32, precision=lax.Precision.HIGHEST)


def _inproj_kernel(x_ref, g_ref, w_ref, wdt_ref, o_ref, dt_ref, u_ref):
    @pl.when(pl.program_id(1) == 0)
    def _():
        x = x_ref[...]
        ms = jnp.mean(x * x, axis=-1, keepdims=True)
        u = (x * lax.rsqrt(ms + EPS) * g_ref[...]).astype(BF16)
        u_ref[...] = u
        dt_ref[...] = _dot(u, wdt_ref[...])

    o_ref[...] = _dot(u_ref[...], w_ref[...])


def _inproj(x, g, w_main, w_dt, tm):
    t = x.shape[0]
    return pl.pallas_call(
        _inproj_kernel,
        grid=(t // tm, N_MAIN // 1024),
        in_specs=[
            pl.BlockSpec((tm, D_MODEL), lambda i, j: (i, 0)),
            pl.BlockSpec((1, D_MODEL), lambda i, j: (0, 0)),
            pl.BlockSpec((D_MODEL, 1024), lambda i, j: (0, j)),
            pl.BlockSpec((D_MODEL, LANES), lambda i, j: (0, 0)),
        ],
        out_specs=[
            pl.BlockSpec((tm, 1024), lambda i, j: (i, j)),
            pl.BlockSpec((tm, LANES), lambda i, j: (i, 0)),
        ],
        out_shape=[jax.ShapeDtypeStruct((t, N_MAIN), F32), jax.ShapeDtypeStruct((t, LANES), F32)],
        scratch_shapes=[pltpu.VMEM((tm, D_MODEL), BF16)],
        compiler_params=_cparams(("parallel", "arbitrary")),
        name="inproj",
    )(x, g, w_main, w_dt)


def _lru_gates(xc, wg_ref, bg_ref, lam):
    xcb = xc.astype(BF16)
    sp = _softplus(-lam)
    a_parts, b_parts = [], []
    for g in range(4):
        sl = slice(256 * g, 256 * (g + 1))
        pre = _dot(xcb[:, sl], wg_ref[g]) + bg_ref[g]
        r = _sigmoid(pre[:, :256])
        i = _sigmoid(pre[:, 256:])
        log_a = (-RG_C * r) * sp[:, sl]
        a_parts.append(jnp.exp(log_a))
        th = jnp.tanh(log_a)
        mult = jnp.sqrt(-2.0 * th / (1.0 - th))
        b_parts.append(mult * (i * xc[:, sl]))
    return jnp.concatenate(a_parts, axis=1), jnp.concatenate(b_parts, axis=1)


def _lru_kernel(x_ref, gate_ref, cw_ref, cb_ref, wg_ref, bg_ref, lam_ref,
                out_ref, conv_ref, h_ref, xx_ref, a_ref, b_ref, hc_ref, *, tl):
    @pl.when(pl.program_id(1) == 0)
    def _():
        xx_ref[0:SUBLANES, :] = jnp.zeros((SUBLANES, D_RNN), F32)
        hc_ref[...] = jnp.zeros((SUBLANES, D_RNN), F32)

    x = x_ref[...]
    xx_ref[SUBLANES:SUBLANES + tl, :] = x
    cw = cw_ref[...]
    xc = (cw[3:4] * x + cw[2:3] * xx_ref[7:7 + tl, :] + cw[1:2] * xx_ref[6:6 + tl, :]
          + cw[0:1] * xx_ref[5:5 + tl, :] + cb_ref[...])
    xx_ref[0:SUBLANES, :] = xx_ref[tl:tl + SUBLANES, :]
    conv_ref[0] = x_ref[tl - 3:tl, :]

    a, b = _lru_gates(xc, wg_ref, bg_ref, lam_ref[...])
    nt = tl // SUBLANES
    a = a.reshape(nt, SUBLANES, D_RNN)
    b = b.reshape(nt, SUBLANES, D_RNN)
    rows = lax.broadcasted_iota(jnp.int32, (nt, SUBLANES, D_RNN), 1)
    for d in (1, 2, 4):
        m = rows >= d
        b = jnp.where(m, a * pltpu.roll(b, d, 1) + b, b)
        a = jnp.where(m, a * pltpu.roll(a, d, 1), a)
    a_ref[...] = a
    b_ref[...] = b

    def carry(k, hprev):
        h = a_ref[k] * hprev + b_ref[k]
        b_ref[k] = h
        return jnp.broadcast_to(h[SUBLANES - 1:SUBLANES, :], (SUBLANES, D_RNN))

    hlast = lax.fori_loop(0, nt, carry, hc_ref[...])
    hc_ref[...] = hlast
    h_ref[0] = hlast[0:1, :]
    h_all = b_ref[...].reshape(tl, D_RNN)
    out_ref[...] = (_gelu_tanh(gate_ref[...]) * h_all).astype(BF16)


def _lru_prompt(proj, cw, cb, wg, bg, lam, nb, seq, tl):
    nl = seq // tl
    return pl.pallas_call(
        functools.partial(_lru_kernel, tl=tl),
        grid=(nb, nl),
        in_specs=[
            pl.BlockSpec((tl, 1024), lambda b, l: (b * nl + l, COL_LRU)),
            pl.BlockSpec((tl, 1024), lambda b, l: (b * nl + l, COL_GATE)),
            pl.BlockSpec((CONV_W, D_RNN), lambda b, l: (0, 0)),
            pl.BlockSpec((1, D_RNN), lambda b, l: (0, 0)),
            pl.BlockSpec((4, 256, 512), lambda b, l: (0, 0, 0)),
            pl.BlockSpec((4, 1, 512), lambda b, l: (0, 0, 0)),
            pl.BlockSpec((1, D_RNN), lambda b, l: (0, 0)),
        ],
        out_specs=[
            pl.BlockSpec((tl, D_RNN), lambda b, l: (b * nl + l, 0)),
            pl.BlockSpec((1, CONV_W - 1, D_RNN), lambda b, l: (b, 0, 0)),
            pl.BlockSpec((1, 1, D_RNN), lambda b, l: (b, 0, 0)),
        ],
        out_shape=[
            jax.ShapeDtypeStruct((nb * seq, D_RNN), BF16),
            jax.ShapeDtypeStruct((nb, CONV_W - 1, D_RNN), F32),
            jax.ShapeDtypeStruct((nb, 1, D_RNN), F32),
        ],
        scratch_shapes=[
            pltpu.VMEM((tl + SUBLANES, D_RNN), F32),
            pltpu.VMEM((tl // SUBLANES, SUBLANES, D_RNN), F32),
            pltpu.VMEM((tl // SUBLANES, SUBLANES, D_RNN), F32),
            pltpu.VMEM((SUBLANES, D_RNN), F32),
        ],
        compiler_params=_cparams(("parallel", "arbitrary")),
        name="lru_prompt",
    )(proj, proj, cw, cb, wg, bg, lam)


def _group_norm_gate(y, z, nrm):
    y = y * (z * _sigmoid(z))
    gw = D_INNER // SSD_GROUPS
    parts = []
    for g in range(SSD_GROUPS):
        yg = y[:, gw * g:gw * (g + 1)]
        ms = jnp.mean(yg * yg, axis=-1, keepdims=True)
        parts.append(yg * lax.rsqrt(ms + EPS))
    return jnp.concatenate(parts, axis=1) * nrm


def _ssd_kernel(xbc_ref, z_ref, dt_ref, cw_ref, cb_ref, dtb_ref, a_ref, dsk_ref, nrm_ref, e_ref,
                y_ref, conv_ref, st_ref, xx_ref, stt_ref, yacc_ref):
    q = CHUNK

    @pl.when(pl.program_id(1) == 0)
    def _():
        xx_ref[0:SUBLANES, :] = jnp.zeros((SUBLANES, SSD_CONV_DIM), F32)
        stt_ref[...] = jnp.zeros((D_STATE, D_INNER), F32)

    xbc = xbc_ref[...]
    xx_ref[SUBLANES:SUBLANES + q, :] = xbc
    cw = cw_ref[...]
    conv = (cw[3:4] * xbc + cw[2:3] * xx_ref[7:7 + q, :] + cw[1:2] * xx_ref[6:6 + q, :]
            + cw[0:1] * xx_ref[5:5 + q, :] + cb_ref[...])
    xx_ref[0:SUBLANES, :] = xx_ref[q:q + SUBLANES, :]
    conv_ref[0] = xbc_ref[q - 3:q, :]
    act = conv * _sigmoid(conv)
    xs = act[:, :D_INNER]
    bm = act[:, D_INNER:D_INNER + SSD_GROUPS * D_STATE]
    cm = act[:, D_INNER + SSD_GROUPS * D_STATE:]

    dt = _softplus(dt_ref[...] + dtb_ref[...])
    dta = dt * a_ref[...]
    iq = lax.broadcasted_iota(jnp.int32, (q, q), 0)
    ik = lax.broadcasted_iota(jnp.int32, (q, q), 1)
    causal = iq >= ik
    a_cum = _dot_exact(causal.astype(F32), dta)
    a_cum_t = a_cum.T
    dt_t = dt.T
    e_cum = jnp.exp(a_cum)
    w_t = jnp.exp(a_cum_t[:, q - 1:q] - a_cum_t) * dt_t
    a_last = jnp.broadcast_to(a_cum[q - 1:q, :], (SUBLANES, LANES))
    da_e = jnp.exp(_dot_exact(a_last, e_ref[...])[0:1, :])
    lo = ik < SSD_HEAD_DIM

    for g in range(SSD_GROUPS):
        bg = bm[:, D_STATE * g:D_STATE * (g + 1)]
        cg = cm[:, D_STATE * g:D_STATE * (g + 1)]
        cb = lax.dot_general(cg.astype(BF16), bg.astype(BF16), (((1,), (1,)), ((), ())),
                             preferred_element_type=F32)
        bg_t = bg.T
        for pp in range(4):
            j = 4 * g + pp
            ms, cs, bw = [], [], []
            for s in range(2):
                h = 2 * j + s
                col = a_cum[:, h:h + 1]
                row = a_cum_t[h:h + 1, :]
                dec = jnp.exp(jnp.where(causal, col - row, -jnp.inf))
                ms.append(cb * dec * dt_t[h:h + 1, :])
                cs.append(cg * e_cum[:, h:h + 1])
                bw.append(bg_t * w_t[h:h + 1, :])
            sl = slice(LANES * j, LANES * (j + 1))
            xp = xs[:, sl]
            rhs_x = jnp.concatenate([jnp.where(lo, xp, 0.0), jnp.where(lo, 0.0, xp)], axis=0).astype(BF16)
            stp = stt_ref[:, sl]
            rhs_s = jnp.concatenate([jnp.where(lo, stp, 0.0), jnp.where(lo, 0.0, stp)], axis=0).astype(BF16)
            l_m = jnp.concatenate(ms, axis=1).astype(BF16)
            l_c = jnp.concatenate(cs, axis=1).astype(BF16)
            l_b = jnp.concatenate(bw, axis=1).astype(BF16)
            yacc_ref[:, sl] = _dot(l_m, rhs_x) + _dot(l_c, rhs_s)
            stt_ref[:, sl] = stp * da_e[:, sl] + _dot(l_b, rhs_x)

    y = yacc_ref[...] + dsk_ref[...] * xs
    y_ref[...] = _group_norm_gate(y, z_ref[...], nrm_ref[...]).astype(BF16)

    @pl.when(pl.program_id(1) == pl.num_programs(1) - 1)
    def _():
        st_ref[0] = stt_ref[...].T


def _ssd_prompt(proj, dt, cw, cb, dtb, a_row, dsk, nrm, e_mat, nb, seq):
    nc = seq // CHUNK
    return pl.pallas_call(
        _ssd_kernel,
        grid=(nb, nc),
        in_specs=[
            pl.BlockSpec((CHUNK, SSD_CONV_DIM), lambda b, c: (b * nc + c, COL_XBC)),
            pl.BlockSpec((CHUNK, D_INNER), lambda b, c: (b * nc + c, COL_Z)),
            pl.BlockSpec((CHUNK, LANES), lambda b, c: (b * nc + c, 0)),
            pl.BlockSpec((CONV_W, SSD_CONV_DIM), lambda b, c: (0, 0)),
            pl.BlockSpec((1, SSD_CONV_DIM), lambda b, c: (0, 0)),
            pl.BlockSpec((1, LANES), lambda b, c: (0, 0)),
            pl.BlockSpec((1, LANES), lambda b, c: (0, 0)),
            pl.BlockSpec((1, D_INNER), lambda b, c: (0, 0)),
            pl.BlockSpec((1, D_INNER), lambda b, c: (0, 0)),
            pl.BlockSpec((LANES, D_INNER), lambda b, c: (0, 0)),
        ],
        out_specs=[
            pl.BlockSpec((CHUNK, D_INNER), lambda b, c: (b * nc + c, 0)),
            pl.BlockSpec((1, CONV_W - 1, SSD_CONV_DIM), lambda b, c: (b, 0, 0)),
            pl.BlockSpec((1, D_INNER, D_STATE), lambda b, c: (b, 0, 0)),
        ],
        out_shape=[
            jax.ShapeDtypeStruct((nb * seq, D_INNER), BF16),
            jax.ShapeDtypeStruct((nb, CONV_W - 1, SSD_CONV_DIM), F32),
            jax.ShapeDtypeStruct((nb, D_INNER, D_STATE), F32),
        ],
        scratch_shapes=[
            pltpu.VMEM((CHUNK + SUBLANES, SSD_CONV_DIM), F32),
            pltpu.VMEM((D_STATE, D_INNER), F32),
            pltpu.VMEM((CHUNK, D_INNER), F32),
        ],
        compiler_params=_cparams(("parallel", "arbitrary")),
        name="ssd_prompt",
    )(proj, proj, dt, cw, cb, dtb, a_row, dsk, nrm, e_mat)


def _sample_pre_kernel(proj_ref, dt_ref, lconv_ref, h0_ref, sconv_ref,
                       lcw_ref, lcb_ref, wg_ref, bg_ref, lam_ref,
                       scw_ref, scb_ref, dtb_ref, a_ref, e_ref,
                       ga_ref, lconv_o, h_o, sconv_o, xdt_t_o, da_o, b_o, c_o, xs_o):
    x = proj_ref[:, 2048:3072]
    gate = proj_ref[:, 6144:7168]
    cw = lcw_ref[...]
    c0 = lconv_ref[:, 0:1024]
    c1 = lconv_ref[:, 1024:2048]
    c2 = lconv_ref[:, 2048:3072]
    xc = cw[3:4] * x + cw[2:3] * c2 + cw[1:2] * c1 + cw[0:1] * c0 + lcb_ref[...]
    lconv_o[:, 0:1024] = c1
    lconv_o[:, 1024:2048] = c2
    lconv_o[:, 2048:3072] = x
    a, bt = _lru_gates(xc, wg_ref, bg_ref, lam_ref[...])
    h = a * h0_ref[...] + bt
    h_o[...] = h
    ga_ref[...] = (_gelu_tanh(gate) * h).astype(BF16)

    xbc = proj_ref[:, 3072:6144]
    sw = scw_ref[...]
    w = SSD_CONV_DIM
    s0 = sconv_ref[:, 0:w]
    s1 = sconv_ref[:, w:2 * w]
    s2 = sconv_ref[:, 2 * w:3 * w]
    conv = sw[3:4] * xbc + sw[2:3] * s2 + sw[1:2] * s1 + sw[0:1] * s0 + scb_ref[...]
    sconv_o[:, 0:w] = s1
    sconv_o[:, w:2 * w] = s2
    sconv_o[:, 2 * w:3 * w] = xbc
    act = conv * _sigmoid(conv)
    xs = act[:, :D_INNER]
    xs_o[...] = xs
    b_o[...] = act[:, D_INNER:D_INNER + SSD_GROUPS * D_STATE]
    c_o[...] = act[:, D_INNER + SSD_GROUPS * D_STATE:]
    dt = _softplus(dt_ref[...] + dtb_ref[...])
    da_o[...] = jnp.exp(dt * a_ref[...])
    dt_e = _dot_exact(dt, e_ref[...])
    xdt_t_o[...] = (xs * dt_e).T


def _sample_pre(proj, dt, lconv, h0, sconv, lcw, lcb, wg, bg, lam, scw, scb, dtb, a_row, e_mat):
    nb = proj.shape[0]
    out_shape = [
        jax.ShapeDtypeStruct((nb, D_RNN), BF16),
        jax.ShapeDtypeStruct((nb, 3 * D_RNN), F32),
        jax.ShapeDtypeStruct((nb, D_RNN), F32),
        jax.ShapeDtypeStruct((nb, 3 * SSD_CONV_DIM), F32),
        jax.ShapeDtypeStruct((D_INNER, nb), F32),
        jax.ShapeDtypeStruct((nb, LANES), F32),
        jax.ShapeDtypeStruct((nb, SSD_GROUPS * D_STATE), F32),
        jax.ShapeDtypeStruct((nb, SSD_GROUPS * D_STATE), F32),
        jax.ShapeDtypeStruct((nb, D_INNER), F32),
    ]
    return pl.pallas_call(
        _sample_pre_kernel,
        out_shape=out_shape,
        compiler_params=pltpu.CompilerParams(vmem_limit_bytes=VMEM_LIMIT),
        name="sample_pre",
    )(proj, dt, lconv, h0, sconv, lcw, lcb, wg, bg, lam, scw, scb, dtb, a_row, e_mat)


def _sample_state_kernel(da_ref, s0_ref, xdt_t_ref, b_ref, c_ref, s1_ref, y_t_ref):
    b = pl.program_id(0)
    nb = xdt_t_ref.shape[1]

    @pl.when(b == 0)
    def _():
        y_t_ref[...] = jnp.zeros(y_t_ref.shape, F32)

    lane = lax.broadcasted_iota(jnp.int32, (D_INNER, nb), 1)
    sel = lane == b
    xcol = jnp.sum(jnp.where(sel, xdt_t_ref[...], 0.0), axis=1, keepdims=True)
    ycols = []
    for g in range(SSD_GROUPS):
        brow = b_ref[0, :, D_STATE * g:D_STATE * (g + 1)]
        crow = c_ref[0, :, D_STATE * g:D_STATE * (g + 1)]
        for hh in range(SSD_HEADS // SSD_GROUPS):
            h = (SSD_HEADS // SSD_GROUPS) * g + hh
            sl = slice(SSD_HEAD_DIM * h, SSD_HEAD_DIM * (h + 1))
            s1 = s0_ref[0, sl, :] * da_ref[b * SSD_HEADS + h] + xcol[sl, :] * brow
            s1_ref[0, sl, :] = s1
            ycols.append(jnp.sum(s1 * crow, axis=1, keepdims=True))
    ycol = jnp.concatenate(ycols, axis=0)
    y_t_ref[...] = jnp.where(sel, ycol, y_t_ref[...])


def _sample_state(da_flat, s0, xdt_t, bmat, cmat):
    nb = s0.shape[0]
    return pl.pallas_call(
        _sample_state_kernel,
        grid_spec=pltpu.PrefetchScalarGridSpec(
            num_scalar_prefetch=1,
            grid=(nb,),
            in_specs=[
                pl.BlockSpec((1, D_INNER, D_STATE), lambda b, da: (b, 0, 0)),
                pl.BlockSpec((D_INNER, nb), lambda b, da: (0, 0)),
                pl.BlockSpec((1, 1, SSD_GROUPS * D_STATE), lambda b, da: (b, 0, 0)),
                pl.BlockSpec((1, 1, SSD_GROUPS * D_STATE), lambda b, da: (b, 0, 0)),
            ],
            out_specs=[
                pl.BlockSpec((1, D_INNER, D_STATE), lambda b, da: (b, 0, 0)),
                pl.BlockSpec((D_INNER, nb), lambda b, da: (0, 0)),
            ],
        ),
        out_shape=[
            jax.ShapeDtypeStruct((nb, D_INNER, D_STATE), F32),
            jax.ShapeDtypeStruct((D_INNER, nb), F32),
        ],
        compiler_params=_cparams(("arbitrary",)),
        name="sample_state",
    )(da_flat, s0, xdt_t, bmat, cmat)


def _sample_post_kernel(y_t_ref, xs_ref, z_ref, dsk_ref, nrm_ref, y_ref):
    y = y_t_ref[...].T + dsk_ref[...] * xs_ref[...]
    y_ref[...] = _group_norm_gate(y, z_ref[:, 0:D_INNER], nrm_ref[...]).astype(BF16)


def _sample_post(y_t, xs, proj, dsk, nrm):
    nb = xs.shape[0]
    return pl.pallas_call(
        _sample_post_kernel,
        grid=(1,),
        in_specs=[
            pl.BlockSpec((D_INNER, nb), lambda i: (0, 0)),
            pl.BlockSpec((nb, D_INNER), lambda i: (0, 0)),
            pl.BlockSpec((nb, D_INNER), lambda i: (0, COL_Z)),
            pl.BlockSpec((1, D_INNER), lambda i: (0, 0)),
            pl.BlockSpec((1, D_INNER), lambda i: (0, 0)),
        ],
        out_specs=pl.BlockSpec((nb, D_INNER), lambda i: (0, 0)),
        out_shape=jax.ShapeDtypeStruct((nb, D_INNER), BF16),
        compiler_params=_cparams(("arbitrary",)),
        name="sample_post",
    )(y_t, xs, proj, dsk, nrm)


def _merge_kernel(ga_ref, yb_ref, gta_ref, gtb_ref, x_ref, wa_ref, wb_ref, wo_ref, nf_ref, wr_ref, br_ref,
                  *refs):
    x1_ref, u2_ref, lg_ref = refs[-3:]
    br_a = _dot(ga_ref[...], wa_ref[...])
    br_b = _dot(yb_ref[...], wb_ref[...])
    merged = _sigmoid(gta_ref[...]) * br_a + _sigmoid(gtb_ref[...]) * br_b
    x1 = x_ref[...] + _dot(merged.astype(BF16), wo_ref[...])
    x1_ref[...] = x1
    ms = jnp.mean(x1 * x1, axis=-1, keepdims=True)
    u2 = (x1 * lax.rsqrt(ms + EPS) * nf_ref[...]).astype(BF16)
    u2_ref[...] = u2
    lg_ref[...] = _dot(u2, wr_ref[...]) + br_ref[...]


def _merge(ga, yb, proj, x, wa, wb, wo, nf, wr, br, tm, t_all, row0, prev):
    t = x.shape[0]
    blk0 = row0 // tm
    const = lambda i: (0, 0)
    in_specs = [
        pl.BlockSpec((tm, D_RNN), lambda i: (i, 0)),
        pl.BlockSpec((tm, D_INNER), lambda i: (i, 0)),
        pl.BlockSpec((tm, 1024), lambda i: (i, COL_GA)),
        pl.BlockSpec((tm, 1024), lambda i: (i, COL_GB)),
        pl.BlockSpec((tm, D_MODEL), lambda i: (i, 0)),
        pl.BlockSpec((D_RNN, D_MODEL), const),
        pl.BlockSpec((D_INNER, D_MODEL), const),
        pl.BlockSpec((D_MODEL, D_MODEL), const),
        pl.BlockSpec((1, D_MODEL), const),
        pl.BlockSpec((D_MODEL, LANES), const),
        pl.BlockSpec((1, LANES), const),
    ]
    args = [ga, yb, proj, proj, x, wa, wb, wo, nf, wr, br]
    aliases = {}
    if prev is not None:
        in_specs += [pl.BlockSpec(memory_space=pl.ANY)] * 3
        aliases = {len(args) + k: k for k in range(3)}
        args += list(prev)
    return pl.pallas_call(
        _merge_kernel,
        grid=(t // tm,),
        in_specs=in_specs,
        out_specs=[
            pl.BlockSpec((tm, D_MODEL), lambda i: (blk0 + i, 0)),
            pl.BlockSpec((tm, D_MODEL), lambda i: (blk0 + i, 0)),
            pl.BlockSpec((tm, LANES), lambda i: (blk0 + i, 0)),
        ],
        out_shape=[
            jax.ShapeDtypeStruct((t_all, D_MODEL), F32),
            jax.ShapeDtypeStruct((t_all, D_MODEL), BF16),
            jax.ShapeDtypeStruct((t_all, LANES), F32),
        ],
        input_output_aliases=aliases,
        compiler_params=_cparams(("parallel",)),
        name="merge_sample" if prev is not None else "merge_prompt",
    )(*args)


def _expert_kernel(be_ref, nu_ref, x_ref, wgu_ref, bgu_ref, wd_ref, bd_ref, o_ref):
    i = pl.program_id(0)

    @pl.when(i < nu_ref[0])
    def _():
        gu = _dot(x_ref[...], wgu_ref[0]) + bgu_ref[0]
        gate = jnp.minimum(gu[:, :D_EXPERT], SWIGLU_LIMIT)
        up = jnp.clip(gu[:, D_EXPERT:], -SWIGLU_LIMIT, SWIGLU_LIMIT)
        act = (up + 1.0) * (gate * _sigmoid(SWIGLU_ALPHA * gate))
        o_ref[...] = _dot(act.astype(BF16), wd_ref[0]) + bd_ref[0]

    @pl.when(i >= nu_ref[0])
    def _():
        o_ref[...] = jnp.zeros(o_ref.shape, F32)


def _experts(block_e, n_used, xr, wgu, bgu, wd, bd):
    n_rows = xr.shape[0]
    n_blocks = n_rows // MOE_BLOCK
    return pl.pallas_call(
        _expert_kernel,
        grid_spec=pltpu.PrefetchScalarGridSpec(
            num_scalar_prefetch=2,
            grid=(n_blocks,),
            in_specs=[
                pl.BlockSpec((MOE_BLOCK, D_MODEL), lambda i, be, nu: (jnp.minimum(i, nu[0] - 1), 0)),
                pl.BlockSpec((1, D_MODEL, 2 * D_EXPERT), lambda i, be, nu: (be[i], 0, 0)),
                pl.BlockSpec((1, 1, 2 * D_EXPERT), lambda i, be, nu: (be[i], 0, 0)),
                pl.BlockSpec((1, D_EXPERT, D_MODEL), lambda i, be, nu: (be[i], 0, 0)),
                pl.BlockSpec((1, 1, D_MODEL), lambda i, be, nu: (be[i], 0, 0)),
            ],
            out_specs=pl.BlockSpec((MOE_BLOCK, D_MODEL), lambda i, be, nu: (i, 0)),
        ),
        out_shape=jax.ShapeDtypeStruct((n_rows, D_MODEL), F32),
        compiler_params=_cparams(("arbitrary",)),
        name="experts",
    )(block_e, n_used, xr, wgu, bgu, wd, bd)


def _final_kernel(x1_ref, yk_ref, g_ref, nf_ref, op_ref, os_ref, *, n_prompt_blocks):
    i = pl.program_id(0)
    g = g_ref[...]
    moe = yk_ref[0] * g[:, 0:1]
    for k in range(1, TOP_K):
        moe = moe + yk_ref[k] * g[:, k:k + 1]
    x2 = x1_ref[...] + moe
    ms = jnp.mean(x2 * x2, axis=-1, keepdims=True)
    y = x2 * lax.rsqrt(ms + EPS) * nf_ref[...]

    @pl.when(i < n_prompt_blocks)
    def _():
        op_ref[...] = y

    @pl.when(i >= n_prompt_blocks)
    def _():
        os_ref[...] = y


def _final(x1, yk, gates, nf, t_prompt, tm):
    t_all = x1.shape[0]
    npb = t_prompt // tm
    return pl.pallas_call(
        functools.partial(_final_kernel, n_prompt_blocks=npb),
        grid=(t_all // tm,),
        in_specs=[
            pl.BlockSpec((tm, D_MODEL), lambda i: (i, 0)),
            pl.BlockSpec((TOP_K, tm, D_MODEL), lambda i: (0, i, 0)),
            pl.BlockSpec((tm, LANES), lambda i: (i, 0)),
            pl.BlockSpec((1, D_MODEL), lambda i: (0, 0)),
        ],
        out_specs=[
            pl.BlockSpec((tm, D_MODEL), lambda i: (jnp.minimum(i, npb - 1), 0)),
            pl.BlockSpec((tm, D_MODEL), lambda i: (jnp.maximum(i - npb, 0), 0)),
        ],
        out_shape=[
            jax.ShapeDtypeStruct((t_prompt, D_MODEL), F32),
            jax.ShapeDtypeStruct((t_all - t_prompt, D_MODEL), F32),
        ],
        compiler_params=_cparams(("arbitrary",)),
        name="final",
    )(x1, yk, gates, nf)


def _block_diag4(w):
    w4 = w.reshape(4, 4, RNN_BLOCK, RNN_BLOCK)
    eye = jnp.eye(4, dtype=w.dtype)
    return jnp.einsum('ghij,hk->ghikj', w4, eye).reshape(4, 256, 256)


def _route(logits, t_all):
    top_v, top_i = lax.top_k(logits[:, :N_EXPERTS], TOP_K)
    gates = jax.nn.softmax(top_v, axis=-1)
    n_assign = t_all * TOP_K
    flat_e = top_i.reshape(-1)
    onehot = (flat_e[:, None] == jnp.arange(N_EXPERTS, dtype=jnp.int32)[None, :]).astype(jnp.int32)
    ranks = jnp.cumsum(onehot, axis=0)
    counts = ranks[-1]
    pos_in_e = jnp.sum((ranks - 1) * onehot, axis=1)
    padded = (counts + MOE_BLOCK - 1) // MOE_BLOCK * MOE_BLOCK
    pend = jnp.cumsum(padded)
    pstart = pend - padded
    dest = pstart[flat_e] + pos_in_e
    n_blocks = -(-n_assign // MOE_BLOCK) + N_EXPERTS
    n_rows = n_blocks * MOE_BLOCK
    row_tok = jnp.zeros((n_rows,), jnp.int32).at[dest].set(
        jnp.arange(n_assign, dtype=jnp.int32) // TOP_K, unique_indices=True)
    n_used = (pend[-1] // MOE_BLOCK).astype(jnp.int32)
    blk = jnp.minimum(jnp.arange(n_blocks, dtype=jnp.int32), n_used - 1) * MOE_BLOCK
    block_e = jnp.minimum(jnp.searchsorted(pend, blk, side='right'), N_EXPERTS - 1).astype(jnp.int32)
    return gates, dest.reshape(t_all, TOP_K), row_tok, block_e, n_used.reshape(1)


def kernel(x_prompt, x_sample, state_lru_conv, state_lru_h, state_ssd_conv, state_ssd, norm_mix, w_in, conv_lru_w, conv_lru_b, w_rg, b_rg, w_ig, b_ig, lam, w_proj_a, conv_ssd_w, conv_ssd_b, dt_bias, a_log, d_skip, ssd_norm, w_proj_b, w_out, norm_ffn, w_router, b_router, w_gate_up, b_gate_up, w_down, b_down, norm_final):
    nbp, seq, _ = x_prompt.shape
    nbs = x_sample.shape[0]
    t_p = nbp * seq
    t_all = t_p + nbs
    l = 0

    wi = w_in[l]
    s = (0, 1024, 2048, 4096, 7168, 7200, 8224, 9248)
    w_main = jnp.concatenate([wi[:, s[2]:s[3]], wi[:, s[0]:s[1]], wi[:, s[3]:s[4]], wi[:, s[1]:s[2]],
                              wi[:, s[5]:s[6]], wi[:, s[6]:s[7]]], axis=1).astype(BF16)
    w_dt = jnp.pad(wi[:, s[4]:s[5]], ((0, 0), (0, LANES - SSD_HEADS))).astype(BF16)
    g_mix = norm_mix[l].reshape(1, D_MODEL)
    wg = jnp.concatenate([_block_diag4(w_rg[l]), _block_diag4(w_ig[l])], axis=2).astype(BF16)
    bg = jnp.concatenate([b_rg[l].reshape(4, 1, 256), b_ig[l].reshape(4, 1, 256)], axis=2)
    lam_r = lam[l].reshape(1, D_RNN)
    lcw, lcb = conv_lru_w[l], conv_lru_b[l].reshape(1, D_RNN)
    scw, scb = conv_ssd_w[l], conv_ssd_b[l].reshape(1, SSD_CONV_DIM)
    dtb = jnp.pad(dt_bias[l], (0, LANES - SSD_HEADS)).reshape(1, LANES)
    a_row = jnp.pad(-jnp.exp(a_log[l]), (0, LANES - SSD_HEADS)).reshape(1, LANES)
    dsk = jnp.repeat(d_skip[l], SSD_HEAD_DIM).reshape(1, D_INNER)
    nrm = ssd_norm[l].reshape(1, D_INNER)
    e_mat = (jnp.arange(LANES)[:, None] == (jnp.arange(D_INNER) // SSD_HEAD_DIM)[None, :]).astype(F32)
    wa, wb, wo = w_proj_a[l].astype(BF16), w_proj_b[l].astype(BF16), w_out[l].astype(BF16)
    nf = norm_ffn[l].reshape(1, D_MODEL)
    wr = jnp.pad(w_router[l], ((0, 0), (0, LANES - N_EXPERTS))).astype(BF16)
    br = jnp.pad(b_router[l], (0, LANES - N_EXPERTS)).reshape(1, LANES)
    wgu, wd = w_gate_up[l].astype(BF16), w_down[l].astype(BF16)
    bgu = b_gate_up[l].reshape(N_EXPERTS, 1, 2 * D_EXPERT)
    bd = b_down[l].reshape(N_EXPERTS, 1, D_MODEL)

    xp = x_prompt.reshape(t_p, D_MODEL)
    proj_p, dt_p = _inproj(xp, g_mix, w_main, w_dt, tm=1024)
    ga_p, p_lc, p_lh = _lru_prompt(proj_p, lcw, lcb, wg, bg, lam_r, nbp, seq, tl=256)
    yb_p, p_sc, p_ss = _ssd_prompt(proj_p, dt_p, scw, scb, dtb, a_row, dsk, nrm, e_mat, nbp, seq)

    xs_in = x_sample.reshape(nbs, D_MODEL)
    proj_s, dt_s = _inproj(xs_in, g_mix, w_main, w_dt, tm=nbs)
    (ga_s, s_lc, s_lh, s_sc, xdt_t, da, b_s, c_s, xs_s) = _sample_pre(
        proj_s, dt_s, state_lru_conv[l].reshape(nbs, 3 * D_RNN), state_lru_h[l],
        state_ssd_conv[l].reshape(nbs, 3 * SSD_CONV_DIM), lcw, lcb, wg, bg, lam_r, scw, scb, dtb, a_row, e_mat)
    s_ss, y_t = _sample_state(da[:, :SSD_HEADS].reshape(-1), state_ssd[l].reshape(nbs, D_INNER, D_STATE),
                              xdt_t, b_s.reshape(nbs, 1, -1), c_s.reshape(nbs, 1, -1))
    yb_s = _sample_post(y_t, xs_s, proj_s, dsk, nrm)

    prev = _merge(ga_p, yb_p, proj_p, xp, wa, wb, wo, nf, wr, br, tm=512, t_all=t_all, row0=0, prev=None)
    x1, u2, logits = _merge(ga_s, yb_s, proj_s, xs_in, wa, wb, wo, nf, wr, br, tm=nbs, t_all=t_all,
                            row0=t_p, prev=prev)

    gates, dest, row_tok, block_e, n_used = _route(logits, t_all)
    xr = jnp.take(u2, row_tok, axis=0)
    yr = _experts(block_e, n_used, xr, wgu, bgu, wd, bd)
    yk = jnp.take(yr, dest.T.reshape(-1), axis=0).reshape(TOP_K, t_all, D_MODEL)
    gates_p = jnp.pad(gates, ((0, 0), (0, LANES - TOP_K)))
    y_p, y_s = _final(x1, yk, gates_p, norm_final.reshape(1, D_MODEL), t_p, tm=nbs)

    return (y_p.reshape(nbp, seq, D_MODEL), y_s.reshape(nbs, 1, D_MODEL),
            p_lc[None], p_lh.reshape(1, nbp, D_RNN), p_sc[None],
            p_ss.reshape(1, nbp, SSD_HEADS, SSD_HEAD_DIM, D_STATE),
            s_lc.reshape(1, nbs, CONV_W - 1, D_RNN), s_lh[None],
            s_sc.reshape(1, nbs, CONV_W - 1, SSD_CONV_DIM),
            s_ss.reshape(1, nbs, SSD_HEADS, SSD_HEAD_DIM, D_STATE))
```

```python
import functools

import jax
import jax.numpy as jnp
from jax import lax
from jax.experimental import pallas as pl
from jax.experimental.pallas import tpu as pltpu

F32 = jnp.float32
BF16 = jnp.bfloat16

D_MODEL = 1024
D_RNN = 1024
RNN_HEADS = 16
RNN_BLOCK = 64
CONV_W = 4
RG_C = 8.0
D_INNER = 2048
SSD_HEAD_DIM = 64
SSD_HEADS = 32
SSD_GROUPS = 4
D_STATE = 128
SSD_CONV_DIM = D_INNER + 2 * SSD_GROUPS * D_STATE
CHUNK = 128
N_EXPERTS = 32
TOP_K = 4
D_EXPERT = 1024
SWIGLU_LIMIT = 7.0
SWIGLU_ALPHA = 1.702
EPS = 1e-6

LANES = 128
SUBLANES = 8
N_MAIN = 9 * 1024
COL_Z, COL_LRU, COL_XBC, COL_GATE, COL_GA, COL_GB = 0, 2, 1, 6, 7, 8
MOE_BLOCK = 256
VMEM_LIMIT = 48 * 1024 * 1024


def _cparams(sem):
    return pltpu.CompilerParams(dimension_semantics=sem, vmem_limit_bytes=VMEM_LIMIT)


def _sigmoid(x):
    return jax.nn.sigmoid(x)


def _softplus(x):
    return jnp.maximum(x, 0.0) + jnp.log1p(jnp.exp(-jnp.abs(x)))


def _gelu_tanh(x):
    return 0.5 * x * (1.0 + jnp.tanh(0.7978845608028654 * (x + 0.044715 * (x * x * x))))


def _dot(a, b):
    return jnp.dot(a, b, preferred_element_type=F32)


def _dot_exact(a, b):
    return jnp.dot(a, b, preferred_element_type=F32, precision=lax.Precision.HIGHEST)


def _inproj_kernel(x_ref, g_ref, w_ref, wdt_ref, o_ref, dt_ref, u_ref):
    @pl.when(pl.program_id(1) == 0)
    def _():
        x = x_ref[...]
        ms = jnp.mean(x * x, axis=-1, keepdims=True)
        u = (x * lax.rsqrt(ms + EPS) * g_ref[...]).astype(BF16)
        u_ref[...] = u
        dt_ref[...] = _dot(u, wdt_ref[...])

    o_ref[...] = _dot(u_ref[...], w_ref[...])


def _inproj(x, g, w_main, w_dt, tm):
    t = x.shape[0]
    return pl.pallas_call(
        _inproj_kernel,
        grid=(t // tm, N_MAIN // 1024),
        in_specs=[
            pl.BlockSpec((tm, D_MODEL), lambda i, j: (i, 0)),
            pl.BlockSpec((1, D_MODEL), lambda i, j: (0, 0)),
            pl.BlockSpec((D_MODEL, 1024), lambda i, j: (0, j)),
            pl.BlockSpec((D_MODEL, LANES), lambda i, j: (0, 0)),
        ],
        out_specs=[
            pl.BlockSpec((tm, 1024), lambda i, j: (i, j)),
            pl.BlockSpec((tm, LANES), lambda i, j: (i, 0)),
        ],
        out_shape=[jax.ShapeDtypeStruct((t, N_MAIN), F32), jax.ShapeDtypeStruct((t, LANES), F32)],
        scratch_shapes=[pltpu.VMEM((tm, D_MODEL), BF16)],
        compiler_params=_cparams(("parallel", "arbitrary")),
        name="inproj",
    )(x, g, w_main, w_dt)


def _lru_gates(xc, wg_ref, bg_ref, lam):
    xcb = xc.astype(BF16)
    sp = _softplus(-lam)
    a_parts, b_parts = [], []
    for g in range(4):
        sl = slice(256 * g, 256 * (g + 1))
        pre = _dot(xcb[:, sl], wg_ref[g]) + bg_ref[g]
        r = _sigmoid(pre[:, :256])
        i = _sigmoid(pre[:, 256:])
        log_a = (-RG_C * r) * sp[:, sl]
        a_parts.append(jnp.exp(log_a))
        th = jnp.tanh(log_a)
        mult = jnp.sqrt(-2.0 * th / (1.0 - th))
        b_parts.append(mult * (i * xc[:, sl]))
    return jnp.concatenate(a_parts, axis=1), jnp.concatenate(b_parts, axis=1)


def _lru_kernel(x_ref, gate_ref, cw_ref, cb_ref, wg_ref, bg_ref, lam_ref,
                out_ref, conv_ref, h_ref, xx_ref, a_ref, b_ref, hc_ref, *, tl):
    @pl.when(pl.program_id(1) == 0)
    def _():
        xx_ref[0:SUBLANES, :] = jnp.zeros((SUBLANES, D_RNN), F32)
        hc_ref[...] = jnp.zeros((SUBLANES, D_RNN), F32)

    x = x_ref[...]
    xx_ref[SUBLANES:SUBLANES + tl, :] = x
    cw = cw_ref[...]
    xc = (cw[3:4] * x + cw[2:3] * xx_ref[7:7 + tl, :] + cw[1:2] * xx_ref[6:6 + tl, :]
          + cw[0:1] * xx_ref[5:5 + tl, :] + cb_ref[...])
    xx_ref[0:SUBLANES, :] = xx_ref[tl:tl + SUBLANES, :]
    conv_ref[0] = x_ref[tl - 3:tl, :]

    a, b = _lru_gates(xc, wg_ref, bg_ref, lam_ref[...])
    nt = tl // SUBLANES
    a = a.reshape(nt, SUBLANES, D_RNN)
    b = b.reshape(nt, SUBLANES, D_RNN)
    rows = lax.broadcasted_iota(jnp.int32, (nt, SUBLANES, D_RNN), 1)
    for d in (1, 2, 4):
        m = rows >= d
        b = jnp.where(m, a * pltpu.roll(b, d, 1) + b, b)
        a = jnp.where(m, a * pltpu.roll(a, d, 1), a)
    a_ref[...] = a
    b_ref[...] = b

    def carry(k, hprev):
        h = a_ref[k] * hprev + b_ref[k]
        b_ref[k] = h
        return jnp.broadcast_to(h[SUBLANES - 1:SUBLANES, :], (SUBLANES, D_RNN))

    hlast = lax.fori_loop(0, nt, carry, hc_ref[...])
    hc_ref[...] = hlast
    h_ref[0] = hlast[0:1, :]
    h_all = b_ref[...].reshape(tl, D_RNN)
    out_ref[...] = (_gelu_tanh(gate_ref[...]) * h_all).astype(BF16)


def _lru_prompt(proj, cw, cb, wg, bg, lam, nb, seq, tl):
    nl = seq // tl
    return pl.pallas_call(
        functools.partial(_lru_kernel, tl=tl),
        grid=(nb, nl),
        in_specs=[
            pl.BlockSpec((tl, 1024), lambda b, l: (b * nl + l, COL_LRU)),
            pl.BlockSpec((tl, 1024), lambda b, l: (b * nl + l, COL_GATE)),
            pl.BlockSpec((CONV_W, D_RNN), lambda b, l: (0, 0)),
            pl.BlockSpec((1, D_RNN), lambda b, l: (0, 0)),
            pl.BlockSpec((4, 256, 512), lambda b, l: (0, 0, 0)),
            pl.BlockSpec((4, 1, 512), lambda b, l: (0, 0, 0)),
            pl.BlockSpec((1, D_RNN), lambda b, l: (0, 0)),
        ],
        out_specs=[
            pl.BlockSpec((tl, D_RNN), lambda b, l: (b * nl + l, 0)),
            pl.BlockSpec((1, CONV_W - 1, D_RNN), lambda b, l: (b, 0, 0)),
            pl.BlockSpec((1, 1, D_RNN), lambda b, l: (b, 0, 0)),
        ],
        out_shape=[
            jax.ShapeDtypeStruct((nb * seq, D_RNN), BF16),
            jax.ShapeDtypeStruct((nb, CONV_W - 1, D_RNN), F32),
            jax.ShapeDtypeStruct((nb, 1, D_RNN), F32),
        ],
        scratch_shapes=[
            pltpu.VMEM((tl + SUBLANES, D_RNN), F32),
            pltpu.VMEM((tl // SUBLANES, SUBLANES, D_RNN), F32),
            pltpu.VMEM((tl // SUBLANES, SUBLANES, D_RNN), F32),
            pltpu.VMEM((SUBLANES, D_RNN), F32),
        ],
        compiler_params=_cparams(("parallel", "arbitrary")),
        name="lru_prompt",
    )(proj, proj, cw, cb, wg, bg, lam)


def _group_norm_gate(y, z, nrm):
    y = y * (z * _sigmoid(z))
    gw = D_INNER // SSD_GROUPS
    parts = []
    for g in range(SSD_GROUPS):
        yg = y[:, gw * g:gw * (g + 1)]
        ms = jnp.mean(yg * yg, axis=-1, keepdims=True)
        parts.append(yg * lax.rsqrt(ms + EPS))
    return jnp.concatenate(parts, axis=1) * nrm


def _ssd_kernel(xbc_ref, z_ref, dt_ref, cw_ref, cb_ref, dtb_ref, a_ref, dsk_ref, nrm_ref, e_ref,
                y_ref, conv_ref, st_ref, xx_ref, stt_ref, yacc_ref):
    q = CHUNK

    @pl.when(pl.program_id(1) == 0)
    def _():
        xx_ref[0:SUBLANES, :] = jnp.zeros((SUBLANES, SSD_CONV_DIM), F32)
        stt_ref[...] = jnp.zeros((D_STATE, D_INNER), F32)

    xbc = xbc_ref[...]
    xx_ref[SUBLANES:SUBLANES + q, :] = xbc
    cw = cw_ref[...]
    conv = (cw[3:4] * xbc + cw[2:3] * xx_ref[7:7 + q, :] + cw[1:2] * xx_ref[6:6 + q, :]
            + cw[0:1] * xx_ref[5:5 + q, :] + cb_ref[...])
    xx_ref[0:SUBLANES, :] = xx_ref[q:q + SUBLANES, :]
    conv_ref[0] = xbc_ref[q - 3:q, :]
    act = conv * _sigmoid(conv)
    xs = act[:, :D_INNER]
    bm = act[:, D_INNER:D_INNER + SSD_GROUPS * D_STATE]
    cm = act[:, D_INNER + SSD_GROUPS * D_STATE:]

    dt = _softplus(dt_ref[...] + dtb_ref[...])
    dta = dt * a_ref[...]
    iq = lax.broadcasted_iota(jnp.int32, (q, q), 0)
    ik = lax.broadcasted_iota(jnp.int32, (q, q), 1)
    causal = iq >= ik
    a_cum = _dot_exact(causal.astype(F32), dta)
    a_cum_t = a_cum.T
    dt_t = dt.T
    e_cum = jnp.exp(a_cum)
    w_t = jnp.exp(a_cum_t[:, q - 1:q] - a_cum_t) * dt_t
    a_last = jnp.broadcast_to(a_cum[q - 1:q, :], (SUBLANES, LANES))
    da_e = jnp.exp(_dot_exact(a_last, e_ref[...])[0:1, :])
    lo = ik < SSD_HEAD_DIM

    for g in range(SSD_GROUPS):
        bg = bm[:, D_STATE * g:D_STATE * (g + 1)]
        cg = cm[:, D_STATE * g:D_STATE * (g + 1)]
        cb = lax.dot_general(cg.astype(BF16), bg.astype(BF16), (((1,), (1,)), ((), ())),
                             preferred_element_type=F32)
        bg_t = bg.T
        for pp in range(4):
            j = 4 * g + pp
            ms, cs, bw = [], [], []
            for s in range(2):
                h = 2 * j + s
                col = a_cum[:, h:h + 1]
                row = a_cum_t[h:h + 1, :]
                dec = jnp.exp(jnp.where(causal, col - row, -jnp.inf))
                ms.append(cb * dec * dt_t[h:h + 1, :])
                cs.append(cg * e_cum[:, h:h + 1])
                bw.append(bg_t * w_t[h:h + 1, :])
            sl = slice(LANES * j, LANES * (j + 1))
            xp = xs[:, sl]
            rhs_x = jnp.concatenate([jnp.where(lo, xp, 0.0), jnp.where(lo, 0.0, xp)], axis=0).astype(BF16)
            stp = stt_ref[:, sl]
            rhs_s = jnp.concatenate([jnp.where(lo, stp, 0.0), jnp.where(lo, 0.0, stp)], axis=0).astype(BF16)
            l_m = jnp.concatenate(ms, axis=1).astype(BF16)
            l_c = jnp.concatenate(cs, axis=1).astype(BF16)
            l_b = jnp.concatenate(bw, axis=1).astype(BF16)
            yacc_ref[:, sl] = _dot(l_m, rhs_x) + _dot(l_c, rhs_s)
            stt_ref[:, sl] = stp * da_e[:, sl] + _dot(l_b, rhs_x)

    y = yacc_ref[...] + dsk_ref[...] * xs
    y_ref[...] = _group_norm_gate(y, z_ref[...], nrm_ref[...]).astype(BF16)

    @pl.when(pl.program_id(1) == pl.num_programs(1) - 1)
    def _():
        st_ref[0] = stt_ref[...].T


def _ssd_prompt(proj, dt, cw, cb, dtb, a_row, dsk, nrm, e_mat, nb, seq):
    nc = seq // CHUNK
    return pl.pallas_call(
        _ssd_kernel,
        grid=(nb, nc),
        in_specs=[
            pl.BlockSpec((CHUNK, SSD_CONV_DIM), lambda b, c: (b * nc + c, COL_XBC)),
            pl.BlockSpec((CHUNK, D_INNER), lambda b, c: (b * nc + c, COL_Z)),
            pl.BlockSpec((CHUNK, LANES), lambda b, c: (b * nc + c, 0)),
            pl.BlockSpec((CONV_W, SSD_CONV_DIM), lambda b, c: (0, 0)),
            pl.BlockSpec((1, SSD_CONV_DIM), lambda b, c: (0, 0)),
            pl.BlockSpec((1, LANES), lambda b, c: (0, 0)),
            pl.BlockSpec((1, LANES), lambda b, c: (0, 0)),
            pl.BlockSpec((1, D_INNER), lambda b, c: (0, 0)),
            pl.BlockSpec((1, D_INNER), lambda b, c: (0, 0)),
            pl.BlockSpec((LANES, D_INNER), lambda b, c: (0, 0)),
        ],
        out_specs=[
            pl.BlockSpec((CHUNK, D_INNER), lambda b, c: (b * nc + c, 0)),
            pl.BlockSpec((1, CONV_W - 1, SSD_CONV_DIM), lambda b, c: (b, 0, 0)),
            pl.BlockSpec((1, D_INNER, D_STATE), lambda b, c: (b, 0, 0)),
        ],
        out_shape=[
            jax.ShapeDtypeStruct((nb * seq, D_INNER), BF16),
            jax.ShapeDtypeStruct((nb, CONV_W - 1, SSD_CONV_DIM), F32),
            jax.ShapeDtypeStruct((nb, D_INNER, D_STATE), F32),
        ],
        scratch_shapes=[
            pltpu.VMEM((CHUNK + SUBLANES, SSD_CONV_DIM), F32),
            pltpu.VMEM((D_STATE, D_INNER), F32),
            pltpu.VMEM((CHUNK, D_INNER), F32),
        ],
        compiler_params=_cparams(("parallel", "arbitrary")),
        name="ssd_prompt",
    )(proj, proj, dt, cw, cb, dtb, a_row, dsk, nrm, e_mat)


def _sample_pre_kernel(proj_ref, dt_ref, lconv_ref, h0_ref, sconv_ref,
                       lcw_ref, lcb_ref, wg_ref, bg_ref, lam_ref,
                       scw_ref, scb_ref, dtb_ref, a_ref, e_ref,
                       ga_ref, lconv_o, h_o, sconv_o, xdt_t_o, da_o, b_o, c_o, xs_o):
    x = proj_ref[:, 2048:3072]
    gate = proj_ref[:, 6144:7168]
    cw = lcw_ref[...]
    c0 = lconv_ref[:, 0:1024]
    c1 = lconv_ref[:, 1024:2048]
    c2 = lconv_ref[:, 2048:3072]
    xc = cw[3:4] * x + cw[2:3] * c2 + cw[1:2] * c1 + cw[0:1] * c0 + lcb_ref[...]
    lconv_o[:, 0:1024] = c1
    lconv_o[:, 1024:2048] = c2
    lconv_o[:, 2048:3072] = x
    a, bt = _lru_gates(xc, wg_ref, bg_ref, lam_ref[...])
    h = a * h0_ref[...] + bt
    h_o[...] = h
    ga_ref[...] = (_gelu_tanh(gate) * h).astype(BF16)

    xbc = proj_ref[:, 3072:6144]
    sw = scw_ref[...]
    w = SSD_CONV_DIM
    s0 = sconv_ref[:, 0:w]
    s1 = sconv_ref[:, w:2 * w]
    s2 = sconv_ref[:, 2 * w:3 * w]
    conv = sw[3:4] * xbc + sw[2:3] * s2 + sw[1:2] * s1 + sw[0:1] * s0 + scb_ref[...]
    sconv_o[:, 0:w] = s1
    sconv_o[:, w:2 * w] = s2
    sconv_o[:, 2 * w:3 * w] = xbc
    act = conv * _sigmoid(conv)
    xs = act[:, :D_INNER]
    xs_o[...] = xs
    b_o[...] = act[:, D_INNER:D_INNER + SSD_GROUPS * D_STATE]
    c_o[...] = act[:, D_INNER + SSD_GROUPS * D_STATE:]
    dt = _softplus(dt_ref[...] + dtb_ref[...])
    da_o[...] = jnp.exp(dt * a_ref[...])
    dt_e = _dot_exact(dt, e_ref[...])
    xdt_t_o[...] = (xs * dt_e).T


def _sample_pre(proj, dt, lconv, h0, sconv, lcw, lcb, wg, bg, lam, scw, scb, dtb, a_row, e_mat):
    nb = proj.shape[0]
    out_shape = [
        jax.ShapeDtypeStruct((nb, D_RNN), BF16),
        jax.ShapeDtypeStruct((nb, 3 * D_RNN), F32),
        jax.ShapeDtypeStruct((nb, D_RNN), F32),
        jax.ShapeDtypeStruct((nb, 3 * SSD_CONV_DIM), F32),
        jax.ShapeDtypeStruct((D_INNER, nb), F32),
        jax.ShapeDtypeStruct((nb, LANES), F32),
        jax.ShapeDtypeStruct((nb, SSD_GROUPS * D_STATE), F32),
        jax.ShapeDtypeStruct((nb, SSD_GROUPS * D_STATE), F32),
        jax.ShapeDtypeStruct((nb, D_INNER), F32),
    ]
    return pl.pallas_call(
        _sample_pre_kernel,
        out_shape=out_shape,
        compiler_params=pltpu.CompilerParams(vmem_limit_bytes=VMEM_LIMIT),
        name="sample_pre",
    )(proj, dt, lconv, h0, sconv, lcw, lcb, wg, bg, lam, scw, scb, dtb, a_row, e_mat)


def _sample_state_kernel(da_ref, s0_ref, xdt_t_ref, b_ref, c_ref, s1_ref, y_t_ref):
    b = pl.program_id(0)
    nb = xdt_t_ref.shape[1]

    @pl.when(b == 0)
    def _():
        y_t_ref[...] = jnp.zeros(y_t_ref.shape, F32)

    lane = lax.broadcasted_iota(jnp.int32, (D_INNER, nb), 1)
    sel = lane == b
    xcol = jnp.sum(jnp.where(sel, xdt_t_ref[...], 0.0), axis=1, keepdims=True)
    ycols = []
    for g in range(SSD_GROUPS):
        brow = b_ref[0, :, D_STATE * g:D_STATE * (g + 1)]
        crow = c_ref[0, :, D_STATE * g:D_STATE * (g + 1)]
        for hh in range(SSD_HEADS // SSD_GROUPS):
            h = (SSD_HEADS // SSD_GROUPS) * g + hh
            sl = slice(SSD_HEAD_DIM * h, SSD_HEAD_DIM * (h + 1))
            s1 = s0_ref[0, sl, :] * da_ref[b * SSD_HEADS + h] + xcol[sl, :] * brow
            s1_ref[0, sl, :] = s1
            ycols.append(jnp.sum(s1 * crow, axis=1, keepdims=True))
    ycol = jnp.concatenate(ycols, axis=0)
    y_t_ref[...] = jnp.where(sel, ycol, y_t_ref[...])


def _sample_state(da_flat, s0, xdt_t, bmat, cmat):
    nb = s0.shape[0]
    return pl.pallas_call(
        _sample_state_kernel,
        grid_spec=pltpu.PrefetchScalarGridSpec(
            num_scalar_prefetch=1,
            grid=(nb,),
            in_specs=[
                pl.BlockSpec((1, D_INNER, D_STATE), lambda b, da: (b, 0, 0)),
                pl.BlockSpec((D_INNER, nb), lambda b, da: (0, 0)),
                pl.BlockSpec((1, 1, SSD_GROUPS * D_STATE), lambda b, da: (b, 0, 0)),
                pl.BlockSpec((1, 1, SSD_GROUPS * D_STATE), lambda b, da: (b, 0, 0)),
            ],
            out_specs=[
                pl.BlockSpec((1, D_INNER, D_STATE), lambda b, da: (b, 0, 0)),
                pl.BlockSpec((D_INNER, nb), lambda b, da: (0, 0)),
            ],
        ),
        out_shape=[
            jax.ShapeDtypeStruct((nb, D_INNER, D_STATE), F32),
            jax.ShapeDtypeStruct((D_INNER, nb), F32),
        ],
        compiler_params=_cparams(("arbitrary",)),
        name="sample_state",
    )(da_flat, s0, xdt_t, bmat, cmat)


def _sample_post_kernel(y_t_ref, xs_ref, z_ref, dsk_ref, nrm_ref, y_ref):
    y = y_t_ref[...].T + dsk_ref[...] * xs_ref[...]
    y_ref[...] = _group_norm_gate(y, z_ref[:, 0:D_INNER], nrm_ref[...]).astype(BF16)


def _sample_post(y_t, xs, proj, dsk, nrm):
    nb = xs.shape[0]
    return pl.pallas_call(
        _sample_post_kernel,
        grid=(1,),
        in_specs=[
            pl.BlockSpec((D_INNER, nb), lambda i: (0, 0)),
            pl.BlockSpec((nb, D_INNER), lambda i: (0, 0)),
            pl.BlockSpec((nb, D_INNER), lambda i: (0, COL_Z)),
            pl.BlockSpec((1, D_INNER), lambda i: (0, 0)),
            pl.BlockSpec((1, D_INNER), lambda i: (0, 0)),
        ],
        out_specs=pl.BlockSpec((nb, D_INNER), lambda i: (0, 0)),
        out_shape=jax.ShapeDtypeStruct((nb, D_INNER), BF16),
        compiler_params=_cparams(("arbitrary",)),
        name="sample_post",
    )(y_t, xs, proj, dsk, nrm)


def _merge_kernel(ga_ref, yb_ref, gta_ref, gtb_ref, x_ref, wa_ref, wb_ref, wo_ref, nf_ref, wr_ref, br_ref,
                  cnt0_ref, *refs, n_real):
    x1_ref, u2_ref, rf_ref, ri_ref, cnt_ref, carry_ref = refs[-6:]

    @pl.when(pl.program_id(0) == 0)
    def _():
        carry_ref[...] = cnt0_ref[...]

    @pl.when(pl.program_id(0) >= n_real)
    def _():
        x1_ref[...] = jnp.zeros(x1_ref.shape, x1_ref.dtype)
        u2_ref[...] = jnp.zeros(u2_ref.shape, u2_ref.dtype)
        rf_ref[...] = jnp.zeros(rf_ref.shape, rf_ref.dtype)
        ri_ref[...] = jnp.zeros(ri_ref.shape, ri_ref.dtype)

    @pl.when(pl.program_id(0) < n_real)
    def _():
        _merge_body(ga_ref, yb_ref, gta_ref, gtb_ref, x_ref, wa_ref, wb_ref, wo_ref, nf_ref, wr_ref, br_ref,
                    x1_ref, u2_ref, rf_ref, ri_ref, carry_ref)

    cnt_ref[...] = carry_ref[...]


def _merge_body(ga_ref, yb_ref, gta_ref, gtb_ref, x_ref, wa_ref, wb_ref, wo_ref, nf_ref, wr_ref, br_ref,
                x1_ref, u2_ref, rf_ref, ri_ref, carry_ref):
    br_a = _dot(ga_ref[...], wa_ref[...])
    br_b = _dot(yb_ref[...], wb_ref[...])
    merged = _sigmoid(gta_ref[...]) * br_a + _sigmoid(gtb_ref[...]) * br_b
    x1 = x_ref[...] + _dot(merged.astype(BF16), wo_ref[...])
    x1_ref[...] = x1
    ms = jnp.mean(x1 * x1, axis=-1, keepdims=True)
    u2 = (x1 * lax.rsqrt(ms + EPS) * nf_ref[...]).astype(BF16)
    u2_ref[...] = u2
    logits = _dot(u2, wr_ref[...]) + br_ref[...]

    tm = logits.shape[0]
    lane = lax.broadcasted_iota(jnp.int32, (tm, LANES), 1)
    lane_f = lane.astype(F32)
    v = jnp.where(lane < N_EXPERTS, logits, -jnp.inf)
    sel = jnp.zeros((tm, LANES), F32)
    hots, vals, idxs = [], [], []
    for _ in range(TOP_K):
        m = jnp.max(v, axis=1, keepdims=True)
        idx = jnp.min(jnp.where(v == m, lane_f, float(LANES)), axis=1, keepdims=True)
        hot = lane_f == idx
        hots.append(hot)
        vals.append(m)
        idxs.append(idx)
        v = jnp.where(hot, -jnp.inf, v)
        sel = sel + hot.astype(F32)
    exps = [jnp.exp(m - vals[0]) for m in vals]
    den = exps[0] + exps[1] + exps[2] + exps[3]
    ir = lax.broadcasted_iota(jnp.int32, (tm, tm), 0)
    ic = lax.broadcasted_iota(jnp.int32, (tm, tm), 1)
    before = _dot((ir > ic).astype(BF16), sel.astype(BF16)) + carry_ref[...]
    rf = jnp.zeros((tm, LANES), F32)
    ri = jnp.zeros((tm, LANES), F32)
    for k in range(TOP_K):
        rank = jnp.sum(jnp.where(hots[k], before, 0.0), axis=1, keepdims=True)
        rf = jnp.where(lane == k, exps[k] / den, rf)
        ri = jnp.where(lane == k, rank, jnp.where(lane == TOP_K + k, idxs[k], ri))
    rf_ref[...] = rf
    ri_ref[...] = ri.astype(jnp.int32)
    carry_ref[...] = carry_ref[...] + jnp.sum(sel, axis=0, keepdims=True)


def _merge(ga, yb, proj, x, wa, wb, wo, nf, wr, br, cnt0, tm, t_all, row0, prev):
    t = x.shape[0]
    blk0 = row0 // tm
    n_real = t // tm
    n_fill = 0 if prev is not None else pl.cdiv(t_all - row0 - t, tm)
    const = lambda i: (0, 0)
    rows = lambda i: jnp.minimum(i, n_real - 1)
    in_specs = [
        pl.BlockSpec((tm, D_RNN), lambda i: (rows(i), 0)),
        pl.BlockSpec((tm, D_INNER), lambda i: (rows(i), 0)),
        pl.BlockSpec((tm, 1024), lambda i: (rows(i), COL_GA)),
        pl.BlockSpec((tm, 1024), lambda i: (rows(i), COL_GB)),
        pl.BlockSpec((tm, D_MODEL), lambda i: (rows(i), 0)),
        pl.BlockSpec((D_RNN, D_MODEL), const),
        pl.BlockSpec((D_INNER, D_MODEL), const),
        pl.BlockSpec((D_MODEL, D_MODEL), const),
        pl.BlockSpec((1, D_MODEL), const),
        pl.BlockSpec((D_MODEL, LANES), const),
        pl.BlockSpec((1, LANES), const),
        pl.BlockSpec((1, LANES), const),
    ]
    args = [ga, yb, proj, proj, x, wa, wb, wo, nf, wr, br, cnt0]
    aliases = {}
    if prev is not None:
        in_specs += [pl.BlockSpec(memory_space=pl.ANY)] * 4
        aliases = {len(args) + k: k for k in range(4)}
        args += list(prev)
    return pl.pallas_call(
        functools.partial(_merge_kernel, n_real=n_real),
        grid=(n_real + n_fill,),
        in_specs=in_specs,
        out_specs=[
            pl.BlockSpec((tm, D_MODEL), lambda i: (blk0 + i, 0)),
            pl.BlockSpec((tm, D_MODEL), lambda i: (blk0 + i, 0)),
            pl.BlockSpec((tm, LANES), lambda i: (blk0 + i, 0)),
            pl.BlockSpec((tm, LANES), lambda i: (blk0 + i, 0)),
            pl.BlockSpec((1, LANES), const),
        ],
        out_shape=[
            jax.ShapeDtypeStruct((t_all, D_MODEL), F32),
            jax.ShapeDtypeStruct((t_all, D_MODEL), BF16),
            jax.ShapeDtypeStruct((t_all, LANES), F32),
            jax.ShapeDtypeStruct((t_all, LANES), jnp.int32),
            jax.ShapeDtypeStruct((1, LANES), F32),
        ],
        scratch_shapes=[pltpu.VMEM((1, LANES), F32)],
        input_output_aliases=aliases,
        compiler_params=_cparams(("arbitrary",)),
        name="merge_sample" if prev is not None else "merge_prompt",
    )(*args)


def _expert_kernel(be_ref, first_ref, nu_ref, x_ref, wgu_ref, bgu_ref, wd_ref, bd_ref, o_ref, wgu_b, wd_b):
    i = pl.program_id(0)

    @pl.when(first_ref[i] == 1)
    def _():
        wgu_b[...] = wgu_ref[0].astype(BF16)
        wd_b[...] = wd_ref[0].astype(BF16)

    @pl.when(i < nu_ref[0])
    def _():
        gu = _dot(x_ref[...], wgu_b[...]) + bgu_ref[0]
        gate = jnp.minimum(gu[:, :D_EXPERT], SWIGLU_LIMIT)
        up = jnp.clip(gu[:, D_EXPERT:], -SWIGLU_LIMIT, SWIGLU_LIMIT)
        act = (up + 1.0) * (gate * _sigmoid(SWIGLU_ALPHA * gate))
        o_ref[...] = _dot(act.astype(BF16), wd_b[...]) + bd_ref[0]

    @pl.when(i >= nu_ref[0])
    def _():
        o_ref[...] = jnp.zeros(o_ref.shape, F32)


def _experts(block_e, first, n_used, xr, wgu, bgu, wd, bd):
    n_rows = xr.shape[0]
    n_blocks = n_rows // MOE_BLOCK
    return pl.pallas_call(
        _expert_kernel,
        grid_spec=pltpu.PrefetchScalarGridSpec(
            num_scalar_prefetch=3,
            grid=(n_blocks,),
            in_specs=[
                pl.BlockSpec((MOE_BLOCK, D_MODEL), lambda i, be, fi, nu: (jnp.minimum(i, nu[0] - 1), 0)),
                pl.BlockSpec((1, D_MODEL, 2 * D_EXPERT), lambda i, be, fi, nu: (be[i], 0, 0)),
                pl.BlockSpec((1, 1, 2 * D_EXPERT), lambda i, be, fi, nu: (be[i], 0, 0)),
                pl.BlockSpec((1, D_EXPERT, D_MODEL), lambda i, be, fi, nu: (be[i], 0, 0)),
                pl.BlockSpec((1, 1, D_MODEL), lambda i, be, fi, nu: (be[i], 0, 0)),
            ],
            out_specs=pl.BlockSpec((MOE_BLOCK, D_MODEL), lambda i, be, fi, nu: (i, 0)),
            scratch_shapes=[pltpu.VMEM((D_MODEL, 2 * D_EXPERT), BF16), pltpu.VMEM((D_EXPERT, D_MODEL), BF16)],
        ),
        out_shape=jax.ShapeDtypeStruct((n_rows, D_MODEL), F32),
        compiler_params=pltpu.CompilerParams(dimension_semantics=("arbitrary",),
                                             vmem_limit_bytes=56 * 1024 * 1024),
        name="experts",
    )(block_e, first, n_used, xr, wgu, bgu, wd, bd)


def _final_kernel(dcur_ref, dnxt_ref, x1_ref, g_ref, nf_ref, yr_hbm, op_ref, os_ref, buf, sem,
                  *, n_prompt_blocks, tm):
    i = pl.program_id(0)
    n = pl.num_programs(0)
    slot = i % 2

    def row_copy(row, s, k, t):
        return pltpu.make_async_copy(yr_hbm.at[pl.ds(row, 1)], buf.at[s, k, pl.ds(t, 1)], sem.at[s])

    def issue(dest_ref, s):
        def body(t, c):
            for k in range(TOP_K):
                row_copy(dest_ref[t * TOP_K + k], s, k, t).start()
            return c
        lax.fori_loop(0, tm, body, 0, unroll=8)

    @pl.when(i == 0)
    def _():
        issue(dcur_ref, 0)

    @pl.when(i + 1 < n)
    def _():
        issue(dnxt_ref, 1 - slot)

    def drain(t, c):
        for k in range(TOP_K):
            row_copy(0, slot, k, t).wait()
        return c
    lax.fori_loop(0, tm, drain, 0, unroll=8)

    g = g_ref[...]
    moe = buf[slot, 0] * g[:, 0:1]
    for k in range(1, TOP_K):
        moe = moe + buf[slot, k] * g[:, k:k + 1]
    x2 = x1_ref[...] + moe
    ms = jnp.mean(x2 * x2, axis=-1, keepdims=True)
    y = x2 * lax.rsqrt(ms + EPS) * nf_ref[...]

    @pl.when(i < n_prompt_blocks)
    def _():
        op_ref[...] = y

    @pl.when(i >= n_prompt_blocks)
    def _():
        os_ref[...] = y


def _final(dest_flat, x1, yr, gates, nf, t_prompt, tm):
    t_all = x1.shape[0]
    npb = t_prompt // tm
    nsteps = t_all // tm
    return pl.pallas_call(
        functools.partial(_final_kernel, n_prompt_blocks=npb, tm=tm),
        grid=(nsteps,),
        in_specs=[
            pl.BlockSpec((tm * TOP_K,), lambda i: (i,), memory_space=pltpu.SMEM),
            pl.BlockSpec((tm * TOP_K,), lambda i: (jnp.minimum(i + 1, nsteps - 1),), memory_space=pltpu.SMEM),
            pl.BlockSpec((tm, D_MODEL), lambda i: (i, 0)),
            pl.BlockSpec((tm, LANES), lambda i: (i, 0)),
            pl.BlockSpec((1, D_MODEL), lambda i: (0, 0)),
            pl.BlockSpec(memory_space=pl.ANY),
        ],
        out_specs=[
            pl.BlockSpec((tm, D_MODEL), lambda i: (jnp.minimum(i, npb - 1), 0)),
            pl.BlockSpec((tm, D_MODEL), lambda i: (jnp.maximum(i - npb, 0), 0)),
        ],
        out_shape=[
            jax.ShapeDtypeStruct((t_prompt, D_MODEL), F32),
            jax.ShapeDtypeStruct((t_all - t_prompt, D_MODEL), F32),
        ],
        scratch_shapes=[pltpu.VMEM((2, TOP_K, tm, D_MODEL), F32), pltpu.SemaphoreType.DMA((2,))],
        compiler_params=_cparams(("arbitrary",)),
        name="final",
    )(dest_flat, dest_flat, x1, gates, nf, yr)


def _block_diag4(w):
    w4 = w.reshape(4, 4, RNN_BLOCK, RNN_BLOCK)
    eye = jnp.eye(4, dtype=w.dtype)
    return jnp.einsum('ghij,hk->ghikj', w4, eye).reshape(4, 256, 256)


def _route_tables(route_i, cnt, t_all):
    rank = route_i[:, 0:TOP_K]
    expert = route_i[:, TOP_K:2 * TOP_K]
    counts = cnt[0, :N_EXPERTS].astype(jnp.int32)
    padded = (counts + MOE_BLOCK - 1) // MOE_BLOCK * MOE_BLOCK
    pend = jnp.cumsum(padded)
    pstart = pend - padded
    ustart = jnp.cumsum(counts) - counts
    dest = jnp.take(pstart, expert) + rank
    n_assign = t_all * TOP_K
    n_blocks = -(-n_assign // MOE_BLOCK) + N_EXPERTS
    n_used = (pend[-1] // MOE_BLOCK).astype(jnp.int32)
    blk = jnp.minimum(jnp.arange(n_blocks, dtype=jnp.int32), n_used - 1) * MOE_BLOCK
    block_e = jnp.minimum(jnp.searchsorted(pend, blk, side='right'), N_EXPERTS - 1).astype(jnp.int32)
    first = jnp.concatenate([jnp.ones((1,), jnp.int32), (block_e[1:] != block_e[:-1]).astype(jnp.int32)])
    order = jnp.argsort(expert.reshape(-1), stable=True).astype(jnp.int32)
    row = jnp.arange(n_blocks * MOE_BLOCK, dtype=jnp.int32)
    row_e = jnp.repeat(block_e, MOE_BLOCK)
    off = row - jnp.take(pstart, row_e)
    valid = (off < jnp.take(counts, row_e)) & (row < pend[-1])
    src = jnp.clip(jnp.take(ustart, row_e) + off, 0, n_assign - 1)
    row_tok = jnp.where(valid, jnp.take(order, src) // TOP_K, 0)
    return dest, row_tok, block_e, first, n_used.reshape(1)


def kernel(x_prompt, x_sample, state_lru_conv, state_lru_h, state_ssd_conv, state_ssd, norm_mix, w_in, conv_lru_w, conv_lru_b, w_rg, b_rg, w_ig, b_ig, lam, w_proj_a, conv_ssd_w, conv_ssd_b, dt_bias, a_log, d_skip, ssd_norm, w_proj_b, w_out, norm_ffn, w_router, b_router, w_gate_up, b_gate_up, w_down, b_down, norm_final):
    nbp, seq, _ = x_prompt.shape
    nbs = x_sample.shape[0]
    t_p = nbp * seq
    t_all = t_p + nbs
    l = 0

    wi = w_in[l]
    s = (0, 1024, 2048, 4096, 7168, 7200, 8224, 9248)
    w_main = jnp.concatenate([wi[:, s[2]:s[3]], wi[:, s[0]:s[1]], wi[:, s[3]:s[4]], wi[:, s[1]:s[2]],
                              wi[:, s[5]:s[6]], wi[:, s[6]:s[7]]], axis=1).astype(BF16)
    w_dt = jnp.pad(wi[:, s[4]:s[5]], ((0, 0), (0, LANES - SSD_HEADS))).astype(BF16)
    g_mix = norm_mix[l].reshape(1, D_MODEL)
    wg = jnp.concatenate([_block_diag4(w_rg[l]), _block_diag4(w_ig[l])], axis=2).astype(BF16)
    bg = jnp.concatenate([b_rg[l].reshape(4, 1, 256), b_ig[l].reshape(4, 1, 256)], axis=2)
    lam_r = lam[l].reshape(1, D_RNN)
    lcw, lcb = conv_lru_w[l], conv_lru_b[l].reshape(1, D_RNN)
    scw, scb = conv_ssd_w[l], conv_ssd_b[l].reshape(1, SSD_CONV_DIM)
    dtb = jnp.pad(dt_bias[l], (0, LANES - SSD_HEADS)).reshape(1, LANES)
    a_row = jnp.pad(-jnp.exp(a_log[l]), (0, LANES - SSD_HEADS)).reshape(1, LANES)
    dsk = jnp.repeat(d_skip[l], SSD_HEAD_DIM).reshape(1, D_INNER)
    nrm = ssd_norm[l].reshape(1, D_INNER)
    e_mat = (jnp.arange(LANES)[:, None] == (jnp.arange(D_INNER) // SSD_HEAD_DIM)[None, :]).astype(F32)
    wa, wb, wo = w_proj_a[l].astype(BF16), w_proj_b[l].astype(BF16), w_out[l].astype(BF16)
    nf = norm_ffn[l].reshape(1, D_MODEL)
    wr = jnp.pad(w_router[l], ((0, 0), (0, LANES - N_EXPERTS))).astype(BF16)
    br = jnp.pad(b_router[l], (0, LANES - N_EXPERTS)).reshape(1, LANES)
    bgu =b_gate_up[l].reshape(N_EXPERTS, 1, 2 * D_EXPERT)
    bd = b_down[l].reshape(N_EXPERTS, 1, D_MODEL)

    xp = x_prompt.reshape(t_p, D_MODEL)
    proj_p, dt_p = _inproj(xp, g_mix, w_main, w_dt, tm=1024)
    ga_p, p_lc, p_lh = _lru_prompt(proj_p, lcw, lcb, wg, bg, lam_r, nbp, seq, tl=256)
    yb_p, p_sc, p_ss = _ssd_prompt(proj_p, dt_p, scw, scb, dtb, a_row, dsk, nrm, e_mat, nbp, seq)

    xs_in = x_sample.reshape(nbs, D_MODEL)
    proj_s, dt_s = _inproj(xs_in, g_mix, w_main, w_dt, tm=nbs)
    (ga_s, s_lc, s_lh, s_sc, xdt_t, da, b_s, c_s, xs_s) = _sample_pre(
        proj_s, dt_s, state_lru_conv[l].reshape(nbs, 3 * D_RNN), state_lru_h[l],
        state_ssd_conv[l].reshape(nbs, 3 * SSD_CONV_DIM), lcw, lcb, wg, bg, lam_r, scw, scb, dtb, a_row, e_mat)
    s_ss, y_t = _sample_state(da[:, :SSD_HEADS].reshape(-1), state_ssd[l].reshape(nbs, D_INNER, D_STATE),
                              xdt_t, b_s.reshape(nbs, 1, -1), c_s.reshape(nbs, 1, -1))
    yb_s = _sample_post(y_t, xs_s, proj_s, dsk, nrm)

    cnt0 = jnp.zeros((1, LANES), F32)
    *prev, cnt_p = _merge(ga_p, yb_p, proj_p, xp, wa, wb, wo, nf, wr, br, cnt0, tm=512, t_all=t_all, row0=0,
                          prev=None)
    x1, u2, route_f, route_i, cnt = _merge(ga_s, yb_s, proj_s, xs_in, wa, wb, wo, nf, wr, br, cnt_p, tm=nbs,
                                           t_all=t_all, row0=t_p, prev=prev)

    dest, row_tok, block_e, first, n_used = _route_tables(route_i, cnt, t_all)
    xr = jnp.take(u2, row_tok, axis=0)
    yr = _experts(block_e, first, n_used, xr, w_gate_up[l], bgu, w_down[l], bd)
    y_p, y_s = _final(dest.reshape(-1), x1, yr, route_f, norm_final.reshape(1, D_MODEL), t_p, tm=nbs)

    return (y_p.reshape(nbp, seq, D_MODEL), y_s.reshape(nbs, 1, D_MODEL),
            p_lc[None], p_lh.reshape(1, nbp, D_RNN), p_sc[None],
            p_ss.reshape(1, nbp, SSD_HEADS, SSD_HEAD_DIM, D_STATE),
            s_lc.reshape(1, nbs, CONV_W - 1, D_RNN), s_lh[None],
            s_sc.reshape(1, nbs, CONV_W - 1, SSD_CONV_DIM),
            s_ss.reshape(1, nbs, SSD_HEADS, SSD_HEAD_DIM, D_STATE))
```

```python
import functools

import jax
import jax.numpy as jnp
from jax import lax
from jax.experimental import pallas as pl
from jax.experimental.pallas import tpu as pltpu

F32 = jnp.float32
BF16 = jnp.bfloat16

D_MODEL = 1024
D_RNN = 1024
RNN_HEADS = 16
RNN_BLOCK = 64
CONV_W = 4
RG_C = 8.0
D_INNER = 2048
SSD_HEAD_DIM = 64
SSD_HEADS = 32
SSD_GROUPS = 4
D_STATE = 128
SSD_CONV_DIM = D_INNER + 2 * SSD_GROUPS * D_STATE
CHUNK = 128
N_EXPERTS = 32
TOP_K = 4
D_EXPERT = 1024
SWIGLU_LIMIT = 7.0
SWIGLU_ALPHA = 1.702
EPS = 1e-6

LANES = 128
SUBLANES = 8
N_MAIN = 9 * 1024
COL_Z, COL_LRU, COL_XBC, COL_GATE, COL_GA, COL_GB = 0, 2, 1, 6, 7, 8
MOE_BLOCK = 256
ISSUE_GROUP = 32
VMEM_LIMIT = 48 * 1024 * 1024


def _cparams(sem):
    return pltpu.CompilerParams(dimension_semantics=sem, vmem_limit_bytes=VMEM_LIMIT)


def _sigmoid(x):
    return jax.nn.sigmoid(x)


def _softplus(x):
    return jnp.maximum(x, 0.0) + jnp.log1p(jnp.exp(-jnp.abs(x)))


def _gelu_tanh(x):
    return 0.5 * x * (1.0 + jnp.tanh(0.7978845608028654 * (x + 0.044715 * (x * x * x))))


def _dot(a, b):
    return jnp.dot(a, b, preferred_element_type=F32)


def _dot_exact(a, b):
    return jnp.dot(a, b, preferred_element_type=F32, precision=lax.Precision.HIGHEST)


def _inproj_kernel(x_ref, g_ref, w_ref, wdt_ref, o_ref, dt_ref, u_ref):
    @pl.when(pl.program_id(1) == 0)
    def _():
        x = x_ref[...]
        ms = jnp.mean(x * x, axis=-1, keepdims=True)
        u = (x * lax.rsqrt(ms + EPS) * g_ref[...]).astype(BF16)
        u_ref[...] = u
        dt_ref[...] = _dot(u, wdt_ref[...])

    o_ref[...] = _dot(u_ref[...], w_ref[...])


def _inproj(x, g, w_main, w_dt, tm):
    t = x.shape[0]
    return pl.pallas_call(
        _inproj_kernel,
        grid=(t // tm, N_MAIN // 1024),
        in_specs=[
            pl.BlockSpec((tm, D_MODEL), lambda i, j: (i, 0)),
            pl.BlockSpec((1, D_MODEL), lambda i, j: (0, 0)),
            pl.BlockSpec((D_MODEL, 1024), lambda i, j: (0, j)),
            pl.BlockSpec((D_MODEL, LANES), lambda i, j: (0, 0)),
        ],
        out_specs=[
            pl.BlockSpec((tm, 1024), lambda i, j: (i, j)),
            pl.BlockSpec((tm, LANES), lambda i, j: (i, 0)),
        ],
        out_shape=[jax.ShapeDtypeStruct((t, N_MAIN), F32), jax.ShapeDtypeStruct((t, LANES), F32)],
        scratch_shapes=[pltpu.VMEM((tm, D_MODEL), BF16)],
        compiler_params=pltpu.CompilerParams(dimension_semantics=("parallel", "arbitrary"),
                                             vmem_limit_bytes=56 * 1024 * 1024),
        name="inproj",
    )(x, g, w_main, w_dt)


def _lru_gates(xc, wg_ref, bg_ref, lam):
    xcb = xc.astype(BF16)
    sp = _softplus(-lam)
    a_parts, b_parts = [], []
    for g in range(4):
        sl = slice(256 * g, 256 * (g + 1))
        pre = _dot(xcb[:, sl], wg_ref[g]) + bg_ref[g]
        r = _sigmoid(pre[:, :256])
        i = _sigmoid(pre[:, 256:])
        log_a = (-RG_C * r) * sp[:, sl]
        a_parts.append(jnp.exp(log_a))
        th = jnp.tanh(log_a)
        mult = jnp.sqrt(-2.0 * th / (1.0 - th))
        b_parts.append(mult * (i * xc[:, sl]))
    return jnp.concatenate(a_parts, axis=1), jnp.concatenate(b_parts, axis=1)


def _lru_kernel(x_ref, gate_ref, cw_ref, cb_ref, wg_ref, bg_ref, lam_ref,
                out_ref, conv_ref, h_ref, xx_ref, a_ref, b_ref, hc_ref, *, tl):
    @pl.when(pl.program_id(1) == 0)
    def _():
        xx_ref[0:SUBLANES, :] = jnp.zeros((SUBLANES, D_RNN), F32)
        hc_ref[...] = jnp.zeros((SUBLANES, D_RNN), F32)

    x = x_ref[...]
    xx_ref[SUBLANES:SUBLANES + tl, :] = x
    cw = cw_ref[...]
    xc = (cw[3:4] * x + cw[2:3] * xx_ref[7:7 + tl, :] + cw[1:2] * xx_ref[6:6 + tl, :]
          + cw[0:1] * xx_ref[5:5 + tl, :] + cb_ref[...])
    xx_ref[0:SUBLANES, :] = xx_ref[tl:tl + SUBLANES, :]
    conv_ref[0] = x_ref[tl - 3:tl, :]

    a, b = _lru_gates(xc, wg_ref, bg_ref, lam_ref[...])
    nt = tl // SUBLANES
    a = a.reshape(nt, SUBLANES, D_RNN)
    b = b.reshape(nt, SUBLANES, D_RNN)
    rows = lax.broadcasted_iota(jnp.int32, (nt, SUBLANES, D_RNN), 1)
    for d in (1, 2, 4):
        m = rows >= d
        b = jnp.where(m, a * pltpu.roll(b, d, 1) + b, b)
        a = jnp.where(m, a * pltpu.roll(a, d, 1), a)
    a_ref[...] = a
    b_ref[...] = b

    def carry(k, hprev):
        h = a_ref[k] * hprev + b_ref[k]
        b_ref[k] = h
        return jnp.broadcast_to(h[SUBLANES - 1:SUBLANES, :], (SUBLANES, D_RNN))

    hlast = lax.fori_loop(0, nt, carry, hc_ref[...])
    hc_ref[...] = hlast
    h_ref[0] = hlast[0:1, :]
    h_all = b_ref[...].reshape(tl, D_RNN)
    out_ref[...] = (_gelu_tanh(gate_ref[...]) * h_all).astype(BF16)


def _lru_prompt(proj, cw, cb, wg, bg, lam, nb, seq, tl):
    nl = seq // tl
    return pl.pallas_call(
        functools.partial(_lru_kernel, tl=tl),
        grid=(nb, nl),
        in_specs=[
            pl.BlockSpec((tl, 1024), lambda b, l: (b * nl + l, COL_LRU)),
            pl.BlockSpec((tl, 1024), lambda b, l: (b * nl + l, COL_GATE)),
            pl.BlockSpec((CONV_W, D_RNN), lambda b, l: (0, 0)),
            pl.BlockSpec((1, D_RNN), lambda b, l: (0, 0)),
            pl.BlockSpec((4, 256, 512), lambda b, l: (0, 0, 0)),
            pl.BlockSpec((4, 1, 512), lambda b, l: (0, 0, 0)),
            pl.BlockSpec((1, D_RNN), lambda b, l: (0, 0)),
        ],
        out_specs=[
            pl.BlockSpec((tl, D_RNN), lambda b, l: (b * nl + l, 0)),
            pl.BlockSpec((1, CONV_W - 1, D_RNN), lambda b, l: (b, 0, 0)),
            pl.BlockSpec((1, 1, D_RNN), lambda b, l: (b, 0, 0)),
        ],
        out_shape=[
            jax.ShapeDtypeStruct((nb * seq, D_RNN), BF16),
            jax.ShapeDtypeStruct((nb, CONV_W - 1, D_RNN), F32),
            jax.ShapeDtypeStruct((nb, 1, D_RNN), F32),
        ],
        scratch_shapes=[
            pltpu.VMEM((tl + SUBLANES, D_RNN), F32),
            pltpu.VMEM((tl // SUBLANES, SUBLANES, D_RNN), F32),
            pltpu.VMEM((tl // SUBLANES, SUBLANES, D_RNN), F32),
            pltpu.VMEM((SUBLANES, D_RNN), F32),
        ],
        compiler_params=_cparams(("parallel", "arbitrary")),
        name="lru_prompt",
    )(proj, proj, cw, cb, wg, bg, lam)


def _group_norm_gate(y, z, nrm):
    y = y * (z * _sigmoid(z))
    gw = D_INNER // SSD_GROUPS
    parts = []
    for g in range(SSD_GROUPS):
        yg = y[:, gw * g:gw * (g + 1)]
        ms = jnp.mean(yg * yg, axis=-1, keepdims=True)
        parts.append(yg * lax.rsqrt(ms + EPS))
    return jnp.concatenate(parts, axis=1) * nrm


def _ssd_kernel(xbc_ref, z_ref, dt_ref, cw_ref, cb_ref, dtb_ref, a_ref, dsk_ref, nrm_ref, e_ref,
                y_ref, conv_ref, st_ref, xx_ref, stt_ref, yacc_ref):
    q = CHUNK

    @pl.when(pl.program_id(1) == 0)
    def _():
        xx_ref[0:SUBLANES, :] = jnp.zeros((SUBLANES, SSD_CONV_DIM), F32)
        stt_ref[...] = jnp.zeros((D_STATE, D_INNER), F32)

    xbc = xbc_ref[...]
    xx_ref[SUBLANES:SUBLANES + q, :] = xbc
    cw = cw_ref[...]
    conv = (cw[3:4] * xbc + cw[2:3] * xx_ref[7:7 + q, :] + cw[1:2] * xx_ref[6:6 + q, :]
            + cw[0:1] * xx_ref[5:5 + q, :] + cb_ref[...])
    xx_ref[0:SUBLANES, :] = xx_ref[q:q + SUBLANES, :]
    conv_ref[0] = xbc_ref[q - 3:q, :]
    act = conv * _sigmoid(conv)
    xs = act[:, :D_INNER]
    bm = act[:, D_INNER:D_INNER + SSD_GROUPS * D_STATE]
    cm = act[:, D_INNER + SSD_GROUPS * D_STATE:]

    dt = _softplus(dt_ref[...] + dtb_ref[...])
    dta = dt * a_ref[...]
    iq = lax.broadcasted_iota(jnp.int32, (q, q), 0)
    ik = lax.broadcasted_iota(jnp.int32, (q, q), 1)
    causal = iq >= ik
    a_cum = _dot_exact(causal.astype(F32), dta)
    a_cum_t = a_cum.T
    dt_t = dt.T
    e_cum = jnp.exp(a_cum)
    w_t = jnp.exp(a_cum_t[:, q - 1:q] - a_cum_t) * dt_t
    a_last = jnp.broadcast_to(a_cum[q - 1:q, :], (SUBLANES, LANES))
    da_e = jnp.exp(_dot_exact(a_last, e_ref[...])[0:1, :])
    lo = ik < SSD_HEAD_DIM

    for g in range(SSD_GROUPS):
        bg = bm[:, D_STATE * g:D_STATE * (g + 1)]
        cg = cm[:, D_STATE * g:D_STATE * (g + 1)]
        cb = lax.dot_general(cg.astype(BF16), bg.astype(BF16), (((1,), (1,)), ((), ())),
                             preferred_element_type=F32)
        bg_t = bg.T
        for pp in range(4):
            j = 4 * g + pp
            ms, cs, bw = [], [], []
            for s in range(2):
                h = 2 * j + s
                col = a_cum[:, h:h + 1]
                row = a_cum_t[h:h + 1, :]
                dec = jnp.exp(jnp.where(causal, col - row, -jnp.inf))
                ms.append(cb * dec * dt_t[h:h + 1, :])
                cs.append(cg * e_cum[:, h:h + 1])
                bw.append(bg_t * w_t[h:h + 1, :])
            sl = slice(LANES * j, LANES * (j + 1))
            xp = xs[:, sl]
            rhs_x = jnp.concatenate([jnp.where(lo, xp, 0.0), jnp.where(lo, 0.0, xp)], axis=0).astype(BF16)
            stp = stt_ref[:, sl]
            rhs_s = jnp.concatenate([jnp.where(lo, stp, 0.0), jnp.where(lo, 0.0, stp)], axis=0).astype(BF16)
            l_m = jnp.concatenate(ms, axis=1).astype(BF16)
            l_c = jnp.concatenate(cs, axis=1).astype(BF16)
            l_b = jnp.concatenate(bw, axis=1).astype(BF16)
            yacc_ref[:, sl] = _dot(l_m, rhs_x) + _dot(l_c, rhs_s)
            stt_ref[:, sl] = stp * da_e[:, sl] + _dot(l_b, rhs_x)

    y = yacc_ref[...] + dsk_ref[...] * xs
    y_ref[...] = _group_norm_gate(y, z_ref[...], nrm_ref[...]).astype(BF16)

    @pl.when(pl.program_id(1) == pl.num_programs(1) - 1)
    def _():
        st_ref[0] = stt_ref[...].T


def _ssd_prompt(proj, dt, cw, cb, dtb, a_row, dsk, nrm, e_mat, nb, seq):
    nc = seq // CHUNK
    return pl.pallas_call(
        _ssd_kernel,
        grid=(nb, nc),
        in_specs=[
            pl.BlockSpec((CHUNK, SSD_CONV_DIM), lambda b, c: (b * nc + c, COL_XBC)),
            pl.BlockSpec((CHUNK, D_INNER), lambda b, c: (b * nc + c, COL_Z)),
            pl.BlockSpec((CHUNK, LANES), lambda b, c: (b * nc + c, 0)),
            pl.BlockSpec((CONV_W, SSD_CONV_DIM), lambda b, c: (0, 0)),
            pl.BlockSpec((1, SSD_CONV_DIM), lambda b, c: (0, 0)),
            pl.BlockSpec((1, LANES), lambda b, c: (0, 0)),
            pl.BlockSpec((1, LANES), lambda b, c: (0, 0)),
            pl.BlockSpec((1, D_INNER), lambda b, c: (0, 0)),
            pl.BlockSpec((1, D_INNER), lambda b, c: (0, 0)),
            pl.BlockSpec((LANES, D_INNER), lambda b, c: (0, 0)),
        ],
        out_specs=[
            pl.BlockSpec((CHUNK, D_INNER), lambda b, c: (b * nc + c, 0)),
            pl.BlockSpec((1, CONV_W - 1, SSD_CONV_DIM), lambda b, c: (b, 0, 0)),
            pl.BlockSpec((1, D_INNER, D_STATE), lambda b, c: (b, 0, 0)),
        ],
        out_shape=[
            jax.ShapeDtypeStruct((nb * seq, D_INNER), BF16),
            jax.ShapeDtypeStruct((nb, CONV_W - 1, SSD_CONV_DIM), F32),
            jax.ShapeDtypeStruct((nb, D_INNER, D_STATE), F32),
        ],
        scratch_shapes=[
            pltpu.VMEM((CHUNK + SUBLANES, SSD_CONV_DIM), F32),
            pltpu.VMEM((D_STATE, D_INNER), F32),
            pltpu.VMEM((CHUNK, D_INNER), F32),
        ],
        compiler_params=_cparams(("parallel", "arbitrary")),
        name="ssd_prompt",
    )(proj, proj, dt, cw, cb, dtb, a_row, dsk, nrm, e_mat)


def _sample_pre_kernel(proj_ref, dt_ref, lconv_ref, h0_ref, sconv_ref,
                       lcw_ref, lcb_ref, wg_ref, bg_ref, lam_ref,
                       scw_ref, scb_ref, dtb_ref, a_ref, e_ref,
                       ga_ref, lconv_o, h_o, sconv_o, xdt_t_o, da_o, b_o, c_o, xs_o):
    x = proj_ref[:, 2048:3072]
    gate = proj_ref[:, 6144:7168]
    cw = lcw_ref[...]
    c0 = lconv_ref[:, 0:1024]
    c1 = lconv_ref[:, 1024:2048]
    c2 = lconv_ref[:, 2048:3072]
    xc = cw[3:4] * x + cw[2:3] * c2 + cw[1:2] * c1 + cw[0:1] * c0 + lcb_ref[...]
    lconv_o[:, 0:1024] = c1
    lconv_o[:, 1024:2048] = c2
    lconv_o[:, 2048:3072] = x
    a, bt = _lru_gates(xc, wg_ref, bg_ref, lam_ref[...])
    h = a * h0_ref[...] + bt
    h_o[...] = h
    ga_ref[...] = (_gelu_tanh(gate) * h).astype(BF16)

    xbc = proj_ref[:, 3072:6144]
    sw = scw_ref[...]
    w = SSD_CONV_DIM
    s0 = sconv_ref[:, 0:w]
    s1 = sconv_ref[:, w:2 * w]
    s2 = sconv_ref[:, 2 * w:3 * w]
    conv = sw[3:4] * xbc + sw[2:3] * s2 + sw[1:2] * s1 + sw[0:1] * s0 + scb_ref[...]
    sconv_o[:, 0:w] = s1
    sconv_o[:, w:2 * w] = s2
    sconv_o[:, 2 * w:3 * w] = xbc
    act = conv * _sigmoid(conv)
    xs = act[:, :D_INNER]
    xs_o[...] = xs
    b_o[...] = act[:, D_INNER:D_INNER + SSD_GROUPS * D_STATE]
    c_o[...] = act[:, D_INNER + SSD_GROUPS * D_STATE:]
    dt = _softplus(dt_ref[...] + dtb_ref[...])
    da_o[...] = jnp.exp(dt * a_ref[...])
    dt_e = _dot_exact(dt, e_ref[...])
    xdt_t_o[...] = (xs * dt_e).T


def _sample_pre(proj, dt, lconv, h0, sconv, lcw, lcb, wg, bg, lam, scw, scb, dtb, a_row, e_mat):
    nb = proj.shape[0]
    out_shape = [
        jax.ShapeDtypeStruct((nb, D_RNN), BF16),
        jax.ShapeDtypeStruct((nb, 3 * D_RNN), F32),
        jax.ShapeDtypeStruct((nb, D_RNN), F32),
        jax.ShapeDtypeStruct((nb, 3 * SSD_CONV_DIM), F32),
        jax.ShapeDtypeStruct((D_INNER, nb), F32),
        jax.ShapeDtypeStruct((nb, LANES), F32),
        jax.ShapeDtypeStruct((nb, SSD_GROUPS * D_STATE), F32),
        jax.ShapeDtypeStruct((nb, SSD_GROUPS * D_STATE), F32),
        jax.ShapeDtypeStruct((nb, D_INNER), F32),
    ]
    return pl.pallas_call(
        _sample_pre_kernel,
        out_shape=out_shape,
        compiler_params=pltpu.CompilerParams(vmem_limit_bytes=VMEM_LIMIT),
        name="sample_pre",
    )(proj, dt, lconv, h0, sconv, lcw, lcb, wg, bg, lam, scw, scb, dtb, a_row, e_mat)


def _sample_state_kernel(da_ref, s0_ref, xdt_t_ref, b_ref, c_ref, s1_ref, y_t_ref):
    b = pl.program_id(0)
    nb = xdt_t_ref.shape[1]

    @pl.when(b == 0)
    def _():
        y_t_ref[...] = jnp.zeros(y_t_ref.shape, F32)

    lane = lax.broadcasted_iota(jnp.int32, (D_INNER, nb), 1)
    sel = lane == b
    xcol = jnp.sum(jnp.where(sel, xdt_t_ref[...], 0.0), axis=1, keepdims=True)
    ycols = []
    for g in range(SSD_GROUPS):
        brow = b_ref[0, :, D_STATE * g:D_STATE * (g + 1)]
        crow = c_ref[0, :, D_STATE * g:D_STATE * (g + 1)]
        for hh in range(SSD_HEADS // SSD_GROUPS):
            h = (SSD_HEADS // SSD_GROUPS) * g + hh
            sl = slice(SSD_HEAD_DIM * h, SSD_HEAD_DIM * (h + 1))
            s1 = s0_ref[0, sl, :] * da_ref[b * SSD_HEADS + h] + xcol[sl, :] * brow
            s1_ref[0, sl, :] = s1
            ycols.append(jnp.sum(s1 * crow, axis=1, keepdims=True))
    ycol = jnp.concatenate(ycols, axis=0)
    y_t_ref[...] = jnp.where(sel, ycol, y_t_ref[...])


def _sample_state(da_flat, s0, xdt_t, bmat, cmat):
    nb = s0.shape[0]
    return pl.pallas_call(
        _sample_state_kernel,
        grid_spec=pltpu.PrefetchScalarGridSpec(
            num_scalar_prefetch=1,
            grid=(nb,),
            in_specs=[
                pl.BlockSpec((1, D_INNER, D_STATE), lambda b, da: (b, 0, 0)),
                pl.BlockSpec((D_INNER, nb), lambda b, da: (0, 0)),
                pl.BlockSpec((1, 1, SSD_GROUPS * D_STATE), lambda b, da: (b, 0, 0)),
                pl.BlockSpec((1, 1, SSD_GROUPS * D_STATE), lambda b, da: (b, 0, 0)),
            ],
            out_specs=[
                pl.BlockSpec((1, D_INNER, D_STATE), lambda b, da: (b, 0, 0)),
                pl.BlockSpec((D_INNER, nb), lambda b, da: (0, 0)),
            ],
        ),
        out_shape=[
            jax.ShapeDtypeStruct((nb, D_INNER, D_STATE), F32),
            jax.ShapeDtypeStruct((D_INNER, nb), F32),
        ],
        compiler_params=_cparams(("arbitrary",)),
        name="sample_state",
    )(da_flat, s0, xdt_t, bmat, cmat)


def _sample_post_kernel(y_t_ref, xs_ref, z_ref, dsk_ref, nrm_ref, y_ref):
    y = y_t_ref[...].T + dsk_ref[...] * xs_ref[...]
    y_ref[...] = _group_norm_gate(y, z_ref[:, 0:D_INNER], nrm_ref[...]).astype(BF16)


def _sample_post(y_t, xs, proj, dsk, nrm):
    nb = xs.shape[0]
    return pl.pallas_call(
        _sample_post_kernel,
        grid=(1,),
        in_specs=[
            pl.BlockSpec((D_INNER, nb), lambda i: (0, 0)),
            pl.BlockSpec((nb, D_INNER), lambda i: (0, 0)),
            pl.BlockSpec((nb, D_INNER), lambda i: (0, COL_Z)),
            pl.BlockSpec((1, D_INNER), lambda i: (0, 0)),
            pl.BlockSpec((1, D_INNER), lambda i: (0, 0)),
        ],
        out_specs=pl.BlockSpec((nb, D_INNER), lambda i: (0, 0)),
        out_shape=jax.ShapeDtypeStruct((nb, D_INNER), BF16),
        compiler_params=_cparams(("arbitrary",)),
        name="sample_post",
    )(y_t, xs, proj, dsk, nrm)


def _merge_kernel(ga_ref, yb_ref, gta_ref, gtb_ref, x_ref, wa_ref, wb_ref, wo_ref, nf_ref, wr_ref, br_ref,
                  cnt0_ref, *refs, n_real):
    x1_ref, u2_ref, rf_ref, ri_ref, cnt_ref, carry_ref = refs[-6:]

    @pl.when(pl.program_id(0) == 0)
    def _():
        carry_ref[...] = cnt0_ref[...]

    @pl.when(pl.program_id(0) >= n_real)
    def _():
        x1_ref[...] = jnp.zeros(x1_ref.shape, x1_ref.dtype)
        u2_ref[...] = jnp.zeros(u2_ref.shape, u2_ref.dtype)
        rf_ref[...] = jnp.zeros(rf_ref.shape, rf_ref.dtype)
        ri_ref[...] = jnp.zeros(ri_ref.shape, ri_ref.dtype)

    @pl.when(pl.program_id(0) < n_real)
    def _():
        _merge_body(ga_ref, yb_ref, gta_ref, gtb_ref, x_ref, wa_ref, wb_ref, wo_ref, nf_ref, wr_ref, br_ref,
                    x1_ref, u2_ref, rf_ref, ri_ref, carry_ref)

    cnt_ref[...] = carry_ref[...]


def _merge_body(ga_ref, yb_ref, gta_ref, gtb_ref, x_ref, wa_ref, wb_ref, wo_ref, nf_ref, wr_ref, br_ref,
                x1_ref, u2_ref, rf_ref, ri_ref, carry_ref):
    br_a = _dot(ga_ref[...], wa_ref[...])
    br_b = _dot(yb_ref[...], wb_ref[...])
    merged = _sigmoid(gta_ref[...]) * br_a + _sigmoid(gtb_ref[...]) * br_b
    x1 = x_ref[...] + _dot(merged.astype(BF16), wo_ref[...])
    x1_ref[...] = x1
    ms = jnp.mean(x1 * x1, axis=-1, keepdims=True)
    u2f = x1 * lax.rsqrt(ms + EPS) * nf_ref[...]
    u2_ref[...] = u2f.reshape(u2_ref.shape)
    u2 = u2f.astype(BF16)
    logits = _dot(u2, wr_ref[...]) + br_ref[...]

    tm = logits.shape[0]
    lane = lax.broadcasted_iota(jnp.int32, (tm, LANES), 1)
    lane_f = lane.astype(F32)
    v = jnp.where(lane < N_EXPERTS, logits, -jnp.inf)
    sel = jnp.zeros((tm, LANES), F32)
    hots, vals, idxs = [], [], []
    for _ in range(TOP_K):
        m = jnp.max(v, axis=1, keepdims=True)
        idx = jnp.min(jnp.where(v == m, lane_f, float(LANES)), axis=1, keepdims=True)
        hot = lane_f == idx
        hots.append(hot)
        vals.append(m)
        idxs.append(idx)
        v = jnp.where(hot, -jnp.inf, v)
        sel = sel + hot.astype(F32)
    exps = [jnp.exp(m - vals[0]) for m in vals]
    den = exps[0] + exps[1] + exps[2] + exps[3]
    ir = lax.broadcasted_iota(jnp.int32, (tm, tm), 0)
    ic = lax.broadcasted_iota(jnp.int32, (tm, tm), 1)
    before = _dot((ir > ic).astype(BF16), sel.astype(BF16)) + carry_ref[...]
    rf = jnp.zeros((tm, LANES), F32)
    ri = jnp.zeros((tm, LANES), F32)
    for k in range(TOP_K):
        rank = jnp.sum(jnp.where(hots[k], before, 0.0), axis=1, keepdims=True)
        rf = jnp.where(lane == k, exps[k] / den, rf)
        ri = jnp.where(lane == k, rank, jnp.where(lane == TOP_K + k, idxs[k], ri))
    rf_ref[...] = rf
    ri_ref[...] = ri.astype(jnp.int32)
    carry_ref[...] = carry_ref[...] + jnp.sum(sel, axis=0, keepdims=True)


def _merge(ga, yb, proj, x, wa, wb, wo, nf, wr, br, cnt0, tm, t_all, row0, prev):
    t = x.shape[0]
    blk0 = row0 // tm
    n_real = t // tm
    n_fill = 0 if prev is not None else pl.cdiv(t_all - row0 - t, tm)
    const = lambda i: (0, 0)
    rows = lambda i: jnp.minimum(i, n_real - 1)
    in_specs = [
        pl.BlockSpec((tm, D_RNN), lambda i: (rows(i), 0)),
        pl.BlockSpec((tm, D_INNER), lambda i: (rows(i), 0)),
        pl.BlockSpec((tm, 1024), lambda i: (rows(i), COL_GA)),
        pl.BlockSpec((tm, 1024), lambda i: (rows(i), COL_GB)),
        pl.BlockSpec((tm, D_MODEL), lambda i: (rows(i), 0)),
        pl.BlockSpec((D_RNN, D_MODEL), const),
        pl.BlockSpec((D_INNER, D_MODEL), const),
        pl.BlockSpec((D_MODEL, D_MODEL), const),
        pl.BlockSpec((1, D_MODEL), const),
        pl.BlockSpec((D_MODEL, LANES), const),
        pl.BlockSpec((1, LANES), const),
        pl.BlockSpec((1, LANES), const),
    ]
    args = [ga, yb, proj, proj, x, wa, wb, wo, nf, wr, br, cnt0]
    aliases = {}
    if prev is not None:
        in_specs += [pl.BlockSpec(memory_space=pl.ANY)] * 4
        aliases = {len(args) + k: k for k in range(4)}
        args += list(prev)
    return pl.pallas_call(
        functools.partial(_merge_kernel, n_real=n_real),
        grid=(n_real + n_fill,),
        in_specs=in_specs,
        out_specs=[
            pl.BlockSpec((tm, D_MODEL), lambda i: (blk0 + i, 0)),
            pl.BlockSpec((tm, SUBLANES, LANES), lambda i: (blk0 + i, 0, 0)),
            pl.BlockSpec((tm, LANES), lambda i: (blk0 + i, 0)),
            pl.BlockSpec((tm, LANES), lambda i: (blk0 + i, 0)),
            pl.BlockSpec((1, LANES), const),
        ],
        out_shape=[
            jax.ShapeDtypeStruct((t_all, D_MODEL), F32),
            jax.ShapeDtypeStruct((t_all, SUBLANES, LANES), F32),
            jax.ShapeDtypeStruct((t_all, LANES), F32),
            jax.ShapeDtypeStruct((t_all, LANES), jnp.int32),
            jax.ShapeDtypeStruct((1, LANES), F32),
        ],
        scratch_shapes=[pltpu.VMEM((1, LANES), F32)],
        input_output_aliases=aliases,
        compiler_params=_cparams(("arbitrary",)),
        name="merge_sample" if prev is not None else "merge_prompt",
    )(*args)


def _dispatch_kernel(dest_ref, tail_ref, has_ref, nu_ref, u2_hbm, xr_hbm, zero_ref, sem, zsem, *, tm, n_blocks):
    i = pl.program_id(0)
    n = pl.num_programs(0)
    slot = i % 2

    def zero_copy(row0):
        return pltpu.make_async_copy(zero_ref, xr_hbm.at[pl.ds(row0, MOE_BLOCK)], zsem)

    @pl.when(i == 0)
    def _():
        zero_ref[...] = jnp.zeros(zero_ref.shape, F32)
        for e in range(N_EXPERTS):
            @pl.when(has_ref[e] == 1)
            def _():
                zero_copy(tail_ref[e]).start()

        def fill(j, c):
            zero_copy(j * MOE_BLOCK).start()
            return c
        lax.fori_loop(nu_ref[0], n_blocks, fill, 0)
        for e in range(N_EXPERTS):
            @pl.when(has_ref[e] == 1)
            def _():
                zero_copy(0).wait()

        def fill_wait(j, c):
            zero_copy(0).wait()
            return c
        lax.fori_loop(nu_ref[0], n_blocks, fill_wait, 0)

    def row_copy(tok, row, s):
        return pltpu.make_async_copy(u2_hbm.at[tok], xr_hbm.at[row], sem.at[s])

    def issue(g, c):
        rows = [dest_ref[g * ISSUE_GROUP + u] for u in range(ISSUE_GROUP)]
        for u in range(ISSUE_GROUP):
            row_copy(i * tm + g * (ISSUE_GROUP // TOP_K) + u // TOP_K, rows[u], slot).start()
        return c
    lax.fori_loop(0, tm * TOP_K // ISSUE_GROUP, issue, 0)

    def drain(s):
        def body(t, c):
            for k in range(TOP_K):
                row_copy(0, 0, s).wait()
            return c
        lax.fori_loop(0, tm, body, 0, unroll=8)

    @pl.when(i > 0)
    def _():
        drain(1 - slot)

    @pl.when(i == n - 1)
    def _():
        drain(slot)


def _dispatch(dest_flat, tail, has, n_used, u2t, n_blocks, tm):
    t_all = u2t.shape[0]
    smem = functools.partial(pl.BlockSpec, memory_space=pltpu.SMEM)
    return pl.pallas_call(
        functools.partial(_dispatch_kernel, tm=tm, n_blocks=n_blocks),
        grid=(t_all // tm,),
        in_specs=[
            smem((tm * TOP_K,), lambda i: (i,)),
            smem((N_EXPERTS,), lambda i: (0,)),
            smem((N_EXPERTS,), lambda i: (0,)),
            smem((1,), lambda i: (0,)),
            pl.BlockSpec(memory_space=pl.ANY),
        ],
        out_specs=pl.BlockSpec(memory_space=pl.ANY),
        out_shape=jax.ShapeDtypeStruct((n_blocks * MOE_BLOCK, SUBLANES, LANES), F32),
        scratch_shapes=[pltpu.VMEM((MOE_BLOCK, SUBLANES, LANES), F32), pltpu.SemaphoreType.DMA((2,)),
                        pltpu.SemaphoreType.DMA],
        compiler_params=_cparams(("arbitrary",)),
        name="dispatch",
    )(dest_flat, tail, has, n_used, u2t)


def _expert_kernel(be_ref, first_ref, nu_ref, x_ref, wgu_ref, bgu_ref, wd_ref, bd_ref, o_ref, wgu_b, wd_b):
    i = pl.program_id(0)

    @pl.when(first_ref[i] == 1)
    def _():
        wgu_b[...] = wgu_ref[0].astype(BF16)
        wd_b[...] = wd_ref[0].astype(BF16)

    @pl.when(i < nu_ref[0])
    def _():
        x = x_ref[...].reshape(MOE_BLOCK, D_MODEL).astype(BF16)
        gu = _dot(x, wgu_b[...]) + bgu_ref[0]
        gate = jnp.minimum(gu[:, :D_EXPERT], SWIGLU_LIMIT)
        up = jnp.clip(gu[:, D_EXPERT:], -SWIGLU_LIMIT, SWIGLU_LIMIT)
        act = (up + 1.0) * (gate * _sigmoid(SWIGLU_ALPHA * gate))
        y = _dot(act.astype(BF16), wd_b[...]) + bd_ref[0]
        o_ref[...] = y.reshape(o_ref.shape)

    @pl.when(i >= nu_ref[0])
    def _():
        o_ref[...] = jnp.zeros(o_ref.shape, F32)


def _experts(block_e, first, n_used, xr, wgu, bgu, wd, bd):
    n_rows = xr.shape[0]
    n_blocks = n_rows // MOE_BLOCK
    row_block = (MOE_BLOCK, SUBLANES, LANES)
    return pl.pallas_call(
        _expert_kernel,
        grid_spec=pltpu.PrefetchScalarGridSpec(
            num_scalar_prefetch=3,
            grid=(n_blocks,),
            in_specs=[
                pl.BlockSpec(row_block, lambda i, be, fi, nu: (jnp.minimum(i, nu[0] - 1), 0, 0)),
                pl.BlockSpec((1, D_MODEL, 2 * D_EXPERT), lambda i, be, fi, nu: (be[i], 0, 0)),
                pl.BlockSpec((1, 1, 2 * D_EXPERT), lambda i, be, fi, nu: (be[i], 0, 0)),
                pl.BlockSpec((1, D_EXPERT, D_MODEL), lambda i, be, fi, nu: (be[i], 0, 0)),
                pl.BlockSpec((1, 1, D_MODEL), lambda i, be, fi, nu: (be[i], 0, 0)),
            ],
            out_specs=pl.BlockSpec(row_block, lambda i, be, fi, nu: (i, 0, 0)),
            scratch_shapes=[pltpu.VMEM((D_MODEL, 2 * D_EXPERT), BF16), pltpu.VMEM((D_EXPERT, D_MODEL), BF16)],
        ),
        out_shape=jax.ShapeDtypeStruct((n_rows, SUBLANES, LANES), F32),
        compiler_params=pltpu.CompilerParams(dimension_semantics=("arbitrary",),
                                             vmem_limit_bytes=56 * 1024 * 1024),
        name="experts",
    )(block_e, first, n_used, xr, wgu, bgu, wd, bd)


def _final_kernel(dcur_ref, dnxt_ref, x1_ref, g_ref, nf_ref, yr_hbm, op_ref, os_ref, buf, sem,
                  *, n_prompt_blocks, tm):
    i = pl.program_id(0)
    n = pl.num_programs(0)
    slot = i % 2

    def row_copy(row, s, k, t):
        dst = buf.at[s, k, pl.ds(pl.multiple_of(t * SUBLANES, SUBLANES), SUBLANES)]
        return pltpu.make_async_copy(yr_hbm.at[row], dst, sem.at[s])

    def issue(dest_ref, s):
        def body(g, c):
            rows = [dest_ref[g * ISSUE_GROUP + u] for u in range(ISSUE_GROUP)]
            for u in range(ISSUE_GROUP):
                t = g * (ISSUE_GROUP // TOP_K) + u // TOP_K
                row_copy(rows[u], s, u % TOP_K, t).start(priority=u % 2)
            return c
        lax.fori_loop(0, tm * TOP_K // ISSUE_GROUP, body, 0)

    @pl.when(i == 0)
    def _():
        issue(dcur_ref, 0)

    @pl.when(i + 1 < n)
    def _():
        issue(dnxt_ref, 1 - slot)

    def drain(t, c):
        for k in range(TOP_K):
            row_copy(0, slot, k, t).wait()
        return c
    lax.fori_loop(0, tm, drain, 0, unroll=8)

    g = g_ref[...]
    parts = []
    for c in range(D_MODEL // LANES):
        moe = buf[slot, 0, pl.ds(c, tm, stride=SUBLANES), :] * g[:, 0:1]
        for k in range(1, TOP_K):
            moe = moe + buf[slot, k, pl.ds(c, tm, stride=SUBLANES), :] * g[:, k:k + 1]
        parts.append(x1_ref[:, LANES * c:LANES * (c + 1)] + moe)
    x2 = jnp.concatenate(parts, axis=1)
    ms = jnp.mean(x2 * x2, axis=-1, keepdims=True)
    y = x2 * lax.rsqrt(ms + EPS) * nf_ref[...]

    @pl.when(i < n_prompt_blocks)
    def _():
        op_ref[...] = y

    @pl.when(i >= n_prompt_blocks)
    def _():
        os_ref[...] = y


def _final(dest_flat, x1, yr, gates, nf, t_prompt, tm):
    t_all = x1.shape[0]
    npb = t_prompt // tm
    nsteps = t_all // tm
    return pl.pallas_call(
        functools.partial(_final_kernel, n_prompt_blocks=npb, tm=tm),
        grid=(nsteps,),
        in_specs=[
            pl.BlockSpec((tm * TOP_K,), lambda i: (i,), memory_space=pltpu.SMEM),
            pl.BlockSpec((tm * TOP_K,), lambda i: (jnp.minimum(i + 1, nsteps - 1),), memory_space=pltpu.SMEM),
            pl.BlockSpec((tm, D_MODEL), lambda i: (i, 0)),
            pl.BlockSpec((tm, LANES), lambda i: (i, 0)),
            pl.BlockSpec((1, D_MODEL), lambda i: (0, 0)),
            pl.BlockSpec(memory_space=pl.ANY),
        ],
        out_specs=[
            pl.BlockSpec((tm, D_MODEL), lambda i: (jnp.minimum(i, npb - 1), 0)),
            pl.BlockSpec((tm, D_MODEL), lambda i: (jnp.maximum(i - npb, 0), 0)),
        ],
        out_shape=[
            jax.ShapeDtypeStruct((t_prompt, D_MODEL), F32),
            jax.ShapeDtypeStruct((t_all - t_prompt, D_MODEL), F32),
        ],
        scratch_shapes=[pltpu.VMEM((2, TOP_K, tm * SUBLANES, LANES), F32), pltpu.SemaphoreType.DMA((2,))],
        compiler_params=_cparams(("arbitrary",)),
        name="final",
    )(dest_flat, dest_flat, x1, gates, nf, yr)


def _block_diag4(w):
    w4 = w.reshape(4, 4, RNN_BLOCK, RNN_BLOCK)
    eye = jnp.eye(4, dtype=w.dtype)
    return jnp.einsum('ghij,hk->ghikj', w4, eye).reshape(4, 256, 256)


def _route_tables(route_i, cnt, t_all):
    rank = route_i[:, 0:TOP_K]
    expert = route_i[:, TOP_K:2 * TOP_K]
    counts = cnt[0, :N_EXPERTS].astype(jnp.int32)
    padded = (counts + MOE_BLOCK - 1) // MOE_BLOCK * MOE_BLOCK
    pend = jnp.cumsum(padded)
    pstart = pend - padded
    dest = jnp.take(pstart, expert) + rank
    n_blocks = _moe_blocks(t_all)
    n_used = (pend[-1] // MOE_BLOCK).astype(jnp.int32)
    blk = jnp.minimum(jnp.arange(n_blocks, dtype=jnp.int32), n_used - 1) * MOE_BLOCK
    block_e = jnp.minimum(jnp.sum((pend[None, :] <= blk[:, None]).astype(jnp.int32), axis=1), N_EXPERTS - 1)
    first = jnp.concatenate([jnp.ones((1,), jnp.int32), (block_e[1:] != block_e[:-1]).astype(jnp.int32)])
    tail = jnp.maximum(pend - MOE_BLOCK, 0)
    has = (counts > 0).astype(jnp.int32)
    return dest.reshape(-1), tail, has, block_e, first, n_used.reshape(1)


def _moe_blocks(t_all):
    return pl.cdiv(t_all * TOP_K, MOE_BLOCK) + N_EXPERTS


def kernel(x_prompt, x_sample, state_lru_conv, state_lru_h, state_ssd_conv, state_ssd, norm_mix, w_in, conv_lru_w, conv_lru_b, w_rg, b_rg, w_ig, b_ig, lam, w_proj_a, conv_ssd_w, conv_ssd_b, dt_bias, a_log, d_skip, ssd_norm, w_proj_b, w_out, norm_ffn, w_router, b_router, w_gate_up, b_gate_up, w_down, b_down, norm_final):
    nbp, seq, _ = x_prompt.shape
    nbs = x_sample.shape[0]
    t_p = nbp * seq
    t_all = t_p + nbs
    l = 0

    wi = w_in[l]
    s = (0, 1024, 2048, 4096, 7168, 7200, 8224, 9248)
    w_main = jnp.concatenate([wi[:, s[2]:s[3]], wi[:, s[0]:s[1]], wi[:, s[3]:s[4]], wi[:, s[1]:s[2]],
                              wi[:, s[5]:s[6]], wi[:, s[6]:s[7]]], axis=1).astype(BF16)
    w_dt = jnp.pad(wi[:, s[4]:s[5]], ((0, 0), (0, LANES - SSD_HEADS))).astype(BF16)
    g_mix = norm_mix[l].reshape(1, D_MODEL)
    wg = jnp.concatenate([_block_diag4(w_rg[l]), _block_diag4(w_ig[l])], axis=2).astype(BF16)
    bg = jnp.concatenate([b_rg[l].reshape(4, 1, 256), b_ig[l].reshape(4, 1, 256)], axis=2)
    lam_r = lam[l].reshape(1, D_RNN)
    lcw, lcb = conv_lru_w[l], conv_lru_b[l].reshape(1, D_RNN)
    scw, scb = conv_ssd_w[l], conv_ssd_b[l].reshape(1, SSD_CONV_DIM)
    dtb = jnp.pad(dt_bias[l], (0, LANES - SSD_HEADS)).reshape(1, LANES)
    a_row = jnp.pad(-jnp.exp(a_log[l]), (0, LANES - SSD_HEADS)).reshape(1, LANES)
    dsk = jnp.repeat(d_skip[l], SSD_HEAD_DIM).reshape(1, D_INNER)
    nrm = ssd_norm[l].reshape(1, D_INNER)
    e_mat = (jnp.arange(LANES)[:, None] == (jnp.arange(D_INNER) // SSD_HEAD_DIM)[None, :]).astype(F32)
    wa, wb, wo = w_proj_a[l].astype(BF16), w_proj_b[l].astype(BF16), w_out[l].astype(BF16)
    nf = norm_ffn[l].reshape(1, D_MODEL)
    wr = jnp.pad(w_router[l], ((0, 0), (0, LANES - N_EXPERTS))).astype(BF16)
    br = jnp.pad(b_router[l], (0, LANES - N_EXPERTS)).reshape(1, LANES)
    bgu =b_gate_up[l].reshape(N_EXPERTS, 1, 2 * D_EXPERT)
    bd = b_down[l].reshape(N_EXPERTS, 1, D_MODEL)

    xp = x_prompt.reshape(t_p, D_MODEL)
    proj_p, dt_p = _inproj(xp, g_mix, w_main, w_dt, tm=2048)
    ga_p, p_lc, p_lh = _lru_prompt(proj_p, lcw, lcb, wg, bg, lam_r, nbp, seq, tl=256)
    yb_p, p_sc, p_ss = _ssd_prompt(proj_p, dt_p, scw, scb, dtb, a_row, dsk, nrm, e_mat, nbp, seq)

    xs_in = x_sample.reshape(nbs, D_MODEL)
    proj_s, dt_s = _inproj(xs_in, g_mix, w_main, w_dt, tm=nbs)
    (ga_s, s_lc, s_lh, s_sc, xdt_t, da, b_s, c_s, xs_s) = _sample_pre(
        proj_s, dt_s, state_lru_conv[l].reshape(nbs, 3 * D_RNN), state_lru_h[l],
        state_ssd_conv[l].reshape(nbs, 3 * SSD_CONV_DIM), lcw, lcb, wg, bg, lam_r, scw, scb, dtb, a_row, e_mat)
    s_ss, y_t = _sample_state(da[:, :SSD_HEADS].reshape(-1), state_ssd[l].reshape(nbs, D_INNER, D_STATE),
                              xdt_t, b_s.reshape(nbs, 1, -1), c_s.reshape(nbs, 1, -1))
    yb_s = _sample_post(y_t, xs_s, proj_s, dsk, nrm)

    cnt0 = jnp.zeros((1, LANES), F32)
    *prev, cnt_p = _merge(ga_p, yb_p, proj_p, xp, wa, wb, wo, nf, wr, br, cnt0, tm=512, t_all=t_all, row0=0,
                          prev=None)
    x1, u2, route_f, route_i, cnt = _merge(ga_s, yb_s, proj_s, xs_in, wa, wb, wo, nf, wr, br, cnt_p, tm=nbs,
                                           t_all=t_all, row0=t_p, prev=prev)

    dest, tail, has, block_e, first, n_used = _route_tables(route_i, cnt, t_all)
    xr = _dispatch(dest, tail, has, n_used, u2, _moe_blocks(t_all), tm=nbs)
    yr = _experts(block_e, first, n_used, xr, w_gate_up[l], bgu, w_down[l], bd)
    y_p, y_s = _final(dest, x1, yr, route_f, norm_final.reshape(1, D_MODEL), t_p, tm=nbs)

    return (y_p.reshape(nbp, seq, D_MODEL), y_s.reshape(nbs, 1, D_MODEL),
            p_lc[None], p_lh.reshape(1, nbp, D_RNN), p_sc[None],
            p_ss.reshape(1, nbp, SSD_HEADS, SSD_HEAD_DIM, D_STATE),
            s_lc.reshape(1, nbs, CONV_W - 1, D_RNN), s_lh[None],
            s_sc.reshape(1, nbs, CONV_W - 1, SSD_CONV_DIM),
            s_ss.reshape(1, nbs, SSD_HEADS, SSD_HEAD_DIM, D_STATE))
```

```python
import functools

import jax
import jax.numpy as jnp
from jax import lax
from jax.experimental import pallas as pl
from jax.experimental.pallas import tpu as pltpu

F32 = jnp.float32
BF16 = jnp.bfloat16

D_MODEL = 1024
D_RNN = 1024
RNN_HEADS = 16
RNN_BLOCK = 64
CONV_W = 4
RG_C = 8.0
D_INNER = 2048
SSD_HEAD_DIM = 64
SSD_HEADS = 32
SSD_GROUPS = 4
D_STATE = 128
SSD_CONV_DIM = D_INNER + 2 * SSD_GROUPS * D_STATE
CHUNK = 128
N_EXPERTS = 32
TOP_K = 4
D_EXPERT = 1024
SWIGLU_LIMIT = 7.0
SWIGLU_ALPHA = 1.702
EPS = 1e-6

LANES = 128
SUBLANES = 8
N_MAIN = 9 * 1024
COL_Z, COL_LRU, COL_XBC, COL_GATE, COL_GA, COL_GB = 0, 2, 1, 6, 7, 8
MOE_BLOCK = 512
ISSUE_GROUP = 32
VMEM_LIMIT = 48 * 1024 * 1024


def _cparams(sem):
    return pltpu.CompilerParams(dimension_semantics=sem, vmem_limit_bytes=VMEM_LIMIT)


def _sigmoid(x):
    return jax.nn.sigmoid(x)


def _softplus(x):
    return jnp.maximum(x, 0.0) + jnp.log1p(jnp.exp(-jnp.abs(x)))


def _gelu_tanh(x):
    return 0.5 * x * (1.0 + jnp.tanh(0.7978845608028654 * (x + 0.044715 * (x * x * x))))


def _dot(a, b):
    return jnp.dot(a, b, preferred_element_type=F32)


def _dot_exact(a, b):
    return jnp.dot(a, b, preferred_element_type=F32, precision=lax.Precision.HIGHEST)


def _inproj_kernel(x_ref, g_ref, w_ref, wdt_ref, o_ref, dt_ref, u_ref):
    @pl.when(pl.program_id(1) == 0)
    def _():
        x = x_ref[...]
        ms = jnp.mean(x * x, axis=-1, keepdims=True)
        u = (x * lax.rsqrt(ms + EPS) * g_ref[...]).astype(BF16)
        u_ref[...] = u
        dt_ref[...] = _dot(u, wdt_ref[...])

    o_ref[...] = _dot(u_ref[...], w_ref[...])


def _inproj(x, g, w_main, w_dt, tm):
    t = x.shape[0]
    return pl.pallas_call(
        _inproj_kernel,
        grid=(t // tm, N_MAIN // 1024),
        in_specs=[
            pl.BlockSpec((tm, D_MODEL), lambda i, j: (i, 0)),
            pl.BlockSpec((1, D_MODEL), lambda i, j: (0, 0)),
            pl.BlockSpec((D_MODEL, 1024), lambda i, j: (0, j)),
            pl.BlockSpec((D_MODEL, LANES), lambda i, j: (0, 0)),
        ],
        out_specs=[
            pl.BlockSpec((tm, 1024), lambda i, j: (i, j)),
            pl.BlockSpec((tm, LANES), lambda i, j: (i, 0)),
        ],
        out_shape=[jax.ShapeDtypeStruct((t, N_MAIN), F32), jax.ShapeDtypeStruct((t, LANES), F32)],
        scratch_shapes=[pltpu.VMEM((tm, D_MODEL), BF16)],
        compiler_params=pltpu.CompilerParams(dimension_semantics=("parallel", "arbitrary"),
                                             vmem_limit_bytes=56 * 1024 * 1024),
        name="inproj",
    )(x, g, w_main, w_dt)


def _lru_gates(xc, wg_ref, bg_ref, lam):
    xcb = xc.astype(BF16)
    sp = _softplus(-lam)
    a_parts, b_parts = [], []
    for g in range(4):
        sl = slice(256 * g, 256 * (g + 1))
        pre = _dot(xcb[:, sl], wg_ref[g]) + bg_ref[g]
        r = _sigmoid(pre[:, :256])
        i = _sigmoid(pre[:, 256:])
        log_a = (-RG_C * r) * sp[:, sl]
        a_parts.append(jnp.exp(log_a))
        th = jnp.tanh(log_a)
        mult = jnp.sqrt(-2.0 * th / (1.0 - th))
        b_parts.append(mult * (i * xc[:, sl]))
    return jnp.concatenate(a_parts, axis=1), jnp.concatenate(b_parts, axis=1)


def _lru_kernel(x_ref, gate_ref, cw_ref, cb_ref, wg_ref, bg_ref, lam_ref,
                out_ref, conv_ref, h_ref, xx_ref, a_ref, b_ref, hc_ref, *, tl):
    @pl.when(pl.program_id(1) == 0)
    def _():
        xx_ref[0:SUBLANES, :] = jnp.zeros((SUBLANES, D_RNN), F32)
        hc_ref[...] = jnp.zeros((SUBLANES, D_RNN), F32)

    x = x_ref[...]
    xx_ref[SUBLANES:SUBLANES + tl, :] = x
    cw = cw_ref[...]
    xc = (cw[3:4] * x + cw[2:3] * xx_ref[7:7 + tl, :] + cw[1:2] * xx_ref[6:6 + tl, :]
          + cw[0:1] * xx_ref[5:5 + tl, :] + cb_ref[...])
    xx_ref[0:SUBLANES, :] = xx_ref[tl:tl + SUBLANES, :]
    conv_ref[0] = x_ref[tl - 3:tl, :]

    a, b = _lru_gates(xc, wg_ref, bg_ref, lam_ref[...])
    nt = tl // SUBLANES
    a = a.reshape(nt, SUBLANES, D_RNN)
    b = b.reshape(nt, SUBLANES, D_RNN)
    rows = lax.broadcasted_iota(jnp.int32, (nt, SUBLANES, D_RNN), 1)
    for d in (1, 2, 4):
        m = rows >= d
        b = jnp.where(m, a * pltpu.roll(b, d, 1) + b, b)
        a = jnp.where(m, a * pltpu.roll(a, d, 1), a)
    a_ref[...] = a
    b_ref[...] = b

    def carry(k, hprev):
        h = a_ref[k] * hprev + b_ref[k]
        b_ref[k] = h
        return jnp.broadcast_to(h[SUBLANES - 1:SUBLANES, :], (SUBLANES, D_RNN))

    hlast = lax.fori_loop(0, nt, carry, hc_ref[...])
    hc_ref[...] = hlast
    h_ref[0] = hlast[0:1, :]
    h_all = b_ref[...].reshape(tl, D_RNN)
    out_ref[...] = (_gelu_tanh(gate_ref[...]) * h_all).astype(BF16)


def _lru_prompt(proj, cw, cb, wg, bg, lam, nb, seq, tl):
    nl = seq // tl
    return pl.pallas_call(
        functools.partial(_lru_kernel, tl=tl),
        grid=(nb, nl),
        in_specs=[
            pl.BlockSpec((tl, 1024), lambda b, l: (b * nl + l, COL_LRU)),
            pl.BlockSpec((tl, 1024), lambda b, l: (b * nl + l, COL_GATE)),
            pl.BlockSpec((CONV_W, D_RNN), lambda b, l: (0, 0)),
            pl.BlockSpec((1, D_RNN), lambda b, l: (0, 0)),
            pl.BlockSpec((4, 256, 512), lambda b, l: (0, 0, 0)),
            pl.BlockSpec((4, 1, 512), lambda b, l: (0, 0, 0)),
            pl.BlockSpec((1, D_RNN), lambda b, l: (0, 0)),
        ],
        out_specs=[
            pl.BlockSpec((tl, D_RNN), lambda b, l: (b * nl + l, 0)),
            pl.BlockSpec((1, CONV_W - 1, D_RNN), lambda b, l: (b, 0, 0)),
            pl.BlockSpec((1, 1, D_RNN), lambda b, l: (b, 0, 0)),
        ],
        out_shape=[
            jax.ShapeDtypeStruct((nb * seq, D_RNN), BF16),
            jax.ShapeDtypeStruct((nb, CONV_W - 1, D_RNN), F32),
            jax.ShapeDtypeStruct((nb, 1, D_RNN), F32),
        ],
        scratch_shapes=[
            pltpu.VMEM((tl + SUBLANES, D_RNN), F32),
            pltpu.VMEM((tl // SUBLANES, SUBLANES, D_RNN), F32),
            pltpu.VMEM((tl // SUBLANES, SUBLANES, D_RNN), F32),
            pltpu.VMEM((SUBLANES, D_RNN), F32),
        ],
        compiler_params=_cparams(("parallel", "arbitrary")),
        name="lru_prompt",
    )(proj, proj, cw, cb, wg, bg, lam)


def _group_norm_gate(y, z, nrm):
    y = y * (z * _sigmoid(z))
    gw = D_INNER // SSD_GROUPS
    parts = []
    for g in range(SSD_GROUPS):
        yg = y[:, gw * g:gw * (g + 1)]
        ms = jnp.mean(yg * yg, axis=-1, keepdims=True)
        parts.append(yg * lax.rsqrt(ms + EPS))
    return jnp.concatenate(parts, axis=1) * nrm


def _ssd_kernel(xbc_ref, z_ref, dt_ref, cw_ref, cb_ref, dtb_ref, a_ref, dsk_ref, nrm_ref, e_ref,
                y_ref, conv_ref, st_ref, xx_ref, stt_ref, yacc_ref):
    q = CHUNK

    @pl.when(pl.program_id(1) == 0)
    def _():
        xx_ref[0:SUBLANES, :] = jnp.zeros((SUBLANES, SSD_CONV_DIM), F32)
        stt_ref[...] = jnp.zeros((D_STATE, D_INNER), F32)

    xbc = xbc_ref[...]
    xx_ref[SUBLANES:SUBLANES + q, :] = xbc
    cw = cw_ref[...]
    conv = (cw[3:4] * xbc + cw[2:3] * xx_ref[7:7 + q, :] + cw[1:2] * xx_ref[6:6 + q, :]
            + cw[0:1] * xx_ref[5:5 + q, :] + cb_ref[...])
    xx_ref[0:SUBLANES, :] = xx_ref[q:q + SUBLANES, :]
    conv_ref[0] = xbc_ref[q - 3:q, :]
    act = conv * _sigmoid(conv)
    xs = act[:, :D_INNER]
    bm = act[:, D_INNER:D_INNER + SSD_GROUPS * D_STATE]
    cm = act[:, D_INNER + SSD_GROUPS * D_STATE:]

    dt = _softplus(dt_ref[...] + dtb_ref[...])
    dta = dt * a_ref[...]
    iq = lax.broadcasted_iota(jnp.int32, (q, q), 0)
    ik = lax.broadcasted_iota(jnp.int32, (q, q), 1)
    causal = iq >= ik
    a_cum = _dot_exact(causal.astype(F32), dta)
    a_cum_t = a_cum.T
    dt_t = dt.T
    e_cum = jnp.exp(a_cum)
    w_t = jnp.exp(a_cum_t[:, q - 1:q] - a_cum_t) * dt_t
    a_last = jnp.broadcast_to(a_cum[q - 1:q, :], (SUBLANES, LANES))
    da_e = jnp.exp(_dot_exact(a_last, e_ref[...])[0:1, :])
    lo = ik < SSD_HEAD_DIM

    for g in range(SSD_GROUPS):
        bg = bm[:, D_STATE * g:D_STATE * (g + 1)]
        cg = cm[:, D_STATE * g:D_STATE * (g + 1)]
        cb = lax.dot_general(cg.astype(BF16), bg.astype(BF16), (((1,), (1,)), ((), ())),
                             preferred_element_type=F32)
        bg_t = bg.T
        for pp in range(4):
            j = 4 * g + pp
            ms, cs, bw = [], [], []
            for s in range(2):
                h = 2 * j + s
                col = a_cum[:, h:h + 1]
                row = a_cum_t[h:h + 1, :]
                dec = jnp.exp(jnp.where(causal, col - row, -jnp.inf))
                ms.append(cb * dec * dt_t[h:h + 1, :])
                cs.append(cg * e_cum[:, h:h + 1])
                bw.append(bg_t * w_t[h:h + 1, :])
            sl = slice(LANES * j, LANES * (j + 1))
            xp = xs[:, sl]
            rhs_x = jnp.concatenate([jnp.where(lo, xp, 0.0), jnp.where(lo, 0.0, xp)], axis=0).astype(BF16)
            stp = stt_ref[:, sl]
            rhs_s = jnp.concatenate([jnp.where(lo, stp, 0.0), jnp.where(lo, 0.0, stp)], axis=0).astype(BF16)
            l_m = jnp.concatenate(ms, axis=1).astype(BF16)
            l_c = jnp.concatenate(cs, axis=1).astype(BF16)
            l_b = jnp.concatenate(bw, axis=1).astype(BF16)
            yacc_ref[:, sl] = _dot(l_m, rhs_x) + _dot(l_c, rhs_s)
            stt_ref[:, sl] = stp * da_e[:, sl] + _dot(l_b, rhs_x)

    y = yacc_ref[...] + dsk_ref[...] * xs
    y_ref[...] = _group_norm_gate(y, z_ref[...], nrm_ref[...]).astype(BF16)

    @pl.when(pl.program_id(1) == pl.num_programs(1) - 1)
    def _():
        st_ref[0] = stt_ref[...].T


def _ssd_prompt(proj, dt, cw, cb, dtb, a_row, dsk, nrm, e_mat, nb, seq):
    nc = seq // CHUNK
    return pl.pallas_call(
        _ssd_kernel,
        grid=(nb, nc),
        in_specs=[
            pl.BlockSpec((CHUNK, SSD_CONV_DIM), lambda b, c: (b * nc + c, COL_XBC)),
            pl.BlockSpec((CHUNK, D_INNER), lambda b, c: (b * nc + c, COL_Z)),
            pl.BlockSpec((CHUNK, LANES), lambda b, c: (b * nc + c, 0)),
            pl.BlockSpec((CONV_W, SSD_CONV_DIM), lambda b, c: (0, 0)),
            pl.BlockSpec((1, SSD_CONV_DIM), lambda b, c: (0, 0)),
            pl.BlockSpec((1, LANES), lambda b, c: (0, 0)),
            pl.BlockSpec((1, LANES), lambda b, c: (0, 0)),
            pl.BlockSpec((1, D_INNER), lambda b, c: (0, 0)),
            pl.BlockSpec((1, D_INNER), lambda b, c: (0, 0)),
            pl.BlockSpec((LANES, D_INNER), lambda b, c: (0, 0)),
        ],
        out_specs=[
            pl.BlockSpec((CHUNK, D_INNER), lambda b, c: (b * nc + c, 0)),
            pl.BlockSpec((1, CONV_W - 1, SSD_CONV_DIM), lambda b, c: (b, 0, 0)),
            pl.BlockSpec((1, D_INNER, D_STATE), lambda b, c: (b, 0, 0)),
        ],
        out_shape=[
            jax.ShapeDtypeStruct((nb * seq, D_INNER), BF16),
            jax.ShapeDtypeStruct((nb, CONV_W - 1, SSD_CONV_DIM), F32),
            jax.ShapeDtypeStruct((nb, D_INNER, D_STATE), F32),
        ],
        scratch_shapes=[
            pltpu.VMEM((CHUNK + SUBLANES, SSD_CONV_DIM), F32),
            pltpu.VMEM((D_STATE, D_INNER), F32),
            pltpu.VMEM((CHUNK, D_INNER), F32),
        ],
        compiler_params=_cparams(("parallel", "arbitrary")),
        name="ssd_prompt",
    )(proj, proj, dt, cw, cb, dtb, a_row, dsk, nrm, e_mat)


def _sample_pre_kernel(proj_ref, dt_ref, lconv_ref, h0_ref, sconv_ref,
                       lcw_ref, lcb_ref, wg_ref, bg_ref, lam_ref,
                       scw_ref, scb_ref, dtb_ref, a_ref, e_ref,
                       ga_ref, lconv_o, h_o, sconv_o, xdt_t_o, da_o, b_o, c_o, xs_o):
    x = proj_ref[:, 2048:3072]
    gate = proj_ref[:, 6144:7168]
    cw = lcw_ref[...]
    c0 = lconv_ref[:, 0:1024]
    c1 = lconv_ref[:, 1024:2048]
    c2 = lconv_ref[:, 2048:3072]
    xc = cw[3:4] * x + cw[2:3] * c2 + cw[1:2] * c1 + cw[0:1] * c0 + lcb_ref[...]
    lconv_o[:, 0:1024] = c1
    lconv_o[:, 1024:2048] = c2
    lconv_o[:, 2048:3072] = x
    a, bt = _lru_gates(xc, wg_ref, bg_ref, lam_ref[...])
    h = a * h0_ref[...] + bt
    h_o[...] = h
    ga_ref[...] = (_gelu_tanh(gate) * h).astype(BF16)

    xbc = proj_ref[:, 3072:6144]
    sw = scw_ref[...]
    w = SSD_CONV_DIM
    s0 = sconv_ref[:, 0:w]
    s1 = sconv_ref[:, w:2 * w]
    s2 = sconv_ref[:, 2 * w:3 * w]
    conv = sw[3:4] * xbc + sw[2:3] * s2 + sw[1:2] * s1 + sw[0:1] * s0 + scb_ref[...]
    sconv_o[:, 0:w] = s1
    sconv_o[:, w:2 * w] = s2
    sconv_o[:, 2 * w:3 * w] = xbc
    act = conv * _sigmoid(conv)
    xs = act[:, :D_INNER]
    xs_o[...] = xs
    b_o[...] = act[:, D_INNER:D_INNER + SSD_GROUPS * D_STATE]
    c_o[...] = act[:, D_INNER + SSD_GROUPS * D_STATE:]
    dt = _softplus(dt_ref[...] + dtb_ref[...])
    da_o[...] = jnp.exp(dt * a_ref[...])
    dt_e = _dot_exact(dt, e_ref[...])
    xdt_t_o[...] = (xs * dt_e).T


def _sample_pre(proj, dt, lconv, h0, sconv, lcw, lcb, wg, bg, lam, scw, scb, dtb, a_row, e_mat):
    nb = proj.shape[0]
    out_shape = [
        jax.ShapeDtypeStruct((nb, D_RNN), BF16),
        jax.ShapeDtypeStruct((nb, 3 * D_RNN), F32),
        jax.ShapeDtypeStruct((nb, D_RNN), F32),
        jax.ShapeDtypeStruct((nb, 3 * SSD_CONV_DIM), F32),
        jax.ShapeDtypeStruct((D_INNER, nb), F32),
        jax.ShapeDtypeStruct((nb, LANES), F32),
        jax.ShapeDtypeStruct((nb, SSD_GROUPS * D_STATE), F32),
        jax.ShapeDtypeStruct((nb, SSD_GROUPS * D_STATE), F32),
        jax.ShapeDtypeStruct((nb, D_INNER), F32),
    ]
    return pl.pallas_call(
        _sample_pre_kernel,
        out_shape=out_shape,
        compiler_params=pltpu.CompilerParams(vmem_limit_bytes=VMEM_LIMIT),
        name="sample_pre",
    )(proj, dt, lconv, h0, sconv, lcw, lcb, wg, bg, lam, scw, scb, dtb, a_row, e_mat)


def _sample_state_kernel(da_ref, s0_ref, xdt_t_ref, b_ref, c_ref, s1_ref, y_t_ref):
    b = pl.program_id(0)
    nb = xdt_t_ref.shape[1]

    @pl.when(b == 0)
    def _():
        y_t_ref[...] = jnp.zeros(y_t_ref.shape, F32)

    lane = lax.broadcasted_iota(jnp.int32, (D_INNER, nb), 1)
    sel = lane == b
    xcol = jnp.sum(jnp.where(sel, xdt_t_ref[...], 0.0), axis=1, keepdims=True)
    ycols = []
    for g in range(SSD_GROUPS):
        brow = b_ref[0, :, D_STATE * g:D_STATE * (g + 1)]
        crow = c_ref[0, :, D_STATE * g:D_STATE * (g + 1)]
        for hh in range(SSD_HEADS // SSD_GROUPS):
            h = (SSD_HEADS // SSD_GROUPS) * g + hh
            sl = slice(SSD_HEAD_DIM * h, SSD_HEAD_DIM * (h + 1))
            s1 = s0_ref[0, sl, :] * da_ref[b * SSD_HEADS + h] + xcol[sl, :] * brow
            s1_ref[0, sl, :] = s1
            ycols.append(jnp.sum(s1 * crow, axis=1, keepdims=True))
    ycol = jnp.concatenate(ycols, axis=0)
    y_t_ref[...] = jnp.where(sel, ycol, y_t_ref[...])


def _sample_state(da_flat, s0, xdt_t, bmat, cmat):
    nb = s0.shape[0]
    return pl.pallas_call(
        _sample_state_kernel,
        grid_spec=pltpu.PrefetchScalarGridSpec(
            num_scalar_prefetch=1,
            grid=(nb,),
            in_specs=[
                pl.BlockSpec((1, D_INNER, D_STATE), lambda b, da: (b, 0, 0)),
                pl.BlockSpec((D_INNER, nb), lambda b, da: (0, 0)),
                pl.BlockSpec((1, 1, SSD_GROUPS * D_STATE), lambda b, da: (b, 0, 0)),
                pl.BlockSpec((1, 1, SSD_GROUPS * D_STATE), lambda b, da: (b, 0, 0)),
            ],
            out_specs=[
                pl.BlockSpec((1, D_INNER, D_STATE), lambda b, da: (b, 0, 0)),
                pl.BlockSpec((D_INNER, nb), lambda b, da: (0, 0)),
            ],
        ),
        out_shape=[
            jax.ShapeDtypeStruct((nb, D_INNER, D_STATE), F32),
            jax.ShapeDtypeStruct((D_INNER, nb), F32),
        ],
        compiler_params=_cparams(("arbitrary",)),
        name="sample_state",
    )(da_flat, s0, xdt_t, bmat, cmat)


def _sample_post_kernel(y_t_ref, xs_ref, z_ref, dsk_ref, nrm_ref, y_ref):
    y = y_t_ref[...].T + dsk_ref[...] * xs_ref[...]
    y_ref[...] = _group_norm_gate(y, z_ref[:, 0:D_INNER], nrm_ref[...]).astype(BF16)


def _sample_post(y_t, xs, proj, dsk, nrm):
    nb = xs.shape[0]
    return pl.pallas_call(
        _sample_post_kernel,
        grid=(1,),
        in_specs=[
            pl.BlockSpec((D_INNER, nb), lambda i: (0, 0)),
            pl.BlockSpec((nb, D_INNER), lambda i: (0, 0)),
            pl.BlockSpec((nb, D_INNER), lambda i: (0, COL_Z)),
            pl.BlockSpec((1, D_INNER), lambda i: (0, 0)),
            pl.BlockSpec((1, D_INNER), lambda i: (0, 0)),
        ],
        out_specs=pl.BlockSpec((nb, D_INNER), lambda i: (0, 0)),
        out_shape=jax.ShapeDtypeStruct((nb, D_INNER), BF16),
        compiler_params=_cparams(("arbitrary",)),
        name="sample_post",
    )(y_t, xs, proj, dsk, nrm)


def _merge_kernel(ga_ref, yb_ref, gta_ref, gtb_ref, x_ref, wa_ref, wb_ref, wo_ref, nf_ref, wr_ref, br_ref,
                  cnt0_ref, *refs, n_real):
    x1_ref, u2_ref, rf_ref, ri_ref, cnt_ref, carry_ref = refs[-6:]

    @pl.when(pl.program_id(0) == 0)
    def _():
        carry_ref[...] = cnt0_ref[...]

    @pl.when(pl.program_id(0) >= n_real)
    def _():
        x1_ref[...] = jnp.zeros(x1_ref.shape, x1_ref.dtype)
        u2_ref[...] = jnp.zeros(u2_ref.shape, u2_ref.dtype)
        rf_ref[...] = jnp.zeros(rf_ref.shape, rf_ref.dtype)
        ri_ref[...] = jnp.zeros(ri_ref.shape, ri_ref.dtype)

    @pl.when(pl.program_id(0) < n_real)
    def _():
        _merge_body(ga_ref, yb_ref, gta_ref, gtb_ref, x_ref, wa_ref, wb_ref, wo_ref, nf_ref, wr_ref, br_ref,
                    x1_ref, u2_ref, rf_ref, ri_ref, carry_ref)

    cnt_ref[...] = carry_ref[...]


def _merge_body(ga_ref, yb_ref, gta_ref, gtb_ref, x_ref, wa_ref, wb_ref, wo_ref, nf_ref, wr_ref, br_ref,
                x1_ref, u2_ref, rf_ref, ri_ref, carry_ref):
    br_a = _dot(ga_ref[...], wa_ref[...])
    br_b = _dot(yb_ref[...], wb_ref[...])
    merged = _sigmoid(gta_ref[...]) * br_a + _sigmoid(gtb_ref[...]) * br_b
    x1 = x_ref[...] + _dot(merged.astype(BF16), wo_ref[...])
    x1_ref[...] = x1
    ms = jnp.mean(x1 * x1, axis=-1, keepdims=True)
    u2f = x1 * lax.rsqrt(ms + EPS) * nf_ref[...]
    u2_ref[...] = u2f.reshape(u2_ref.shape)
    u2 = u2f.astype(BF16)
    logits = _dot(u2, wr_ref[...]) + br_ref[...]

    tm = logits.shape[0]
    lane = lax.broadcasted_iota(jnp.int32, (tm, LANES), 1)
    lane_f = lane.astype(F32)
    v = jnp.where(lane < N_EXPERTS, logits, -jnp.inf)
    sel = jnp.zeros((tm, LANES), F32)
    hots, vals, idxs = [], [], []
    for _ in range(TOP_K):
        m = jnp.max(v, axis=1, keepdims=True)
        idx = jnp.min(jnp.where(v == m, lane_f, float(LANES)), axis=1, keepdims=True)
        hot = lane_f == idx
        hots.append(hot)
        vals.append(m)
        idxs.append(idx)
        v = jnp.where(hot, -jnp.inf, v)
        sel = sel + hot.astype(F32)
    exps = [jnp.exp(m - vals[0]) for m in vals]
    den = exps[0] + exps[1] + exps[2] + exps[3]
    ir = lax.broadcasted_iota(jnp.int32, (tm, tm), 0)
    ic = lax.broadcasted_iota(jnp.int32, (tm, tm), 1)
    before = _dot((ir > ic).astype(BF16), sel.astype(BF16)) + carry_ref[...]
    rf = jnp.zeros((tm, LANES), F32)
    ri = jnp.zeros((tm, LANES), F32)
    for k in range(TOP_K):
        rank = jnp.sum(jnp.where(hots[k], before, 0.0), axis=1, keepdims=True)
        rf = jnp.where(lane == k, exps[k] / den, rf)
        ri = jnp.where(lane == k, rank, jnp.where(lane == TOP_K + k, idxs[k], ri))
    rf_ref[...] = rf
    ri_ref[...] = ri.astype(jnp.int32)
    carry_ref[...] = carry_ref[...] + jnp.sum(sel, axis=0, keepdims=True)


def _merge(ga, yb, proj, x, wa, wb, wo, nf, wr, br, cnt0, tm, t_all, row0, prev):
    t = x.shape[0]
    blk0 = row0 // tm
    n_real = t // tm
    n_fill = 0 if prev is not None else pl.cdiv(t_all - row0 - t, tm)
    const = lambda i: (0, 0)
    rows = lambda i: jnp.minimum(i, n_real - 1)
    in_specs = [
        pl.BlockSpec((tm, D_RNN), lambda i: (rows(i), 0)),
        pl.BlockSpec((tm, D_INNER), lambda i: (rows(i), 0)),
        pl.BlockSpec((tm, 1024), lambda i: (rows(i), COL_GA)),
        pl.BlockSpec((tm, 1024), lambda i: (rows(i), COL_GB)),
        pl.BlockSpec((tm, D_MODEL), lambda i: (rows(i), 0)),
        pl.BlockSpec((D_RNN, D_MODEL), const),
        pl.BlockSpec((D_INNER, D_MODEL), const),
        pl.BlockSpec((D_MODEL, D_MODEL), const),
        pl.BlockSpec((1, D_MODEL), const),
        pl.BlockSpec((D_MODEL, LANES), const),
        pl.BlockSpec((1, LANES), const),
        pl.BlockSpec((1, LANES), const),
    ]
    args = [ga, yb, proj, proj, x, wa, wb, wo, nf, wr, br, cnt0]
    aliases = {}
    if prev is not None:
        in_specs += [pl.BlockSpec(memory_space=pl.ANY)] * 4
        aliases = {len(args) + k: k for k in range(4)}
        args += list(prev)
    return pl.pallas_call(
        functools.partial(_merge_kernel, n_real=n_real),
        grid=(n_real + n_fill,),
        in_specs=in_specs,
        out_specs=[
            pl.BlockSpec((tm, D_MODEL), lambda i: (blk0 + i, 0)),
            pl.BlockSpec((tm, SUBLANES, LANES), lambda i: (blk0 + i, 0, 0)),
            pl.BlockSpec((tm, LANES), lambda i: (blk0 + i, 0)),
            pl.BlockSpec((tm, LANES), lambda i: (blk0 + i, 0)),
            pl.BlockSpec((1, LANES), const),
        ],
        out_shape=[
            jax.ShapeDtypeStruct((t_all, D_MODEL), F32),
            jax.ShapeDtypeStruct((t_all, SUBLANES, LANES), F32),
            jax.ShapeDtypeStruct((t_all, LANES), F32),
            jax.ShapeDtypeStruct((t_all, LANES), jnp.int32),
            jax.ShapeDtypeStruct((1, LANES), F32),
        ],
        scratch_shapes=[pltpu.VMEM((1, LANES), F32)],
        input_output_aliases=aliases,
        compiler_params=_cparams(("arbitrary",)),
        name="merge_sample" if prev is not None else "merge_prompt",
    )(*args)


def _dispatch_kernel(dest_ref, tail_ref, has_ref, nu_ref, u2_ref, xr_hbm, stage, zero_ref, sem, zsem,
                     *, tm, n_blocks):
    i = pl.program_id(0)
    n = pl.num_programs(0)
    slot = i % 2

    def zero_copy(row0):
        return pltpu.make_async_copy(zero_ref, xr_hbm.at[pl.ds(row0, MOE_BLOCK)], zsem)

    @pl.when(i == 0)
    def _():
        zero_ref[...] = jnp.zeros(zero_ref.shape, F32)
        for e in range(N_EXPERTS):
            @pl.when(has_ref[e] == 1)
            def _():
                zero_copy(tail_ref[e]).start()

        def fill(j, c):
            zero_copy(j * MOE_BLOCK).start()
            return c
        lax.fori_loop(nu_ref[0], n_blocks, fill, 0)
        for e in range(N_EXPERTS):
            @pl.when(has_ref[e] == 1)
            def _():
                zero_copy(0).wait()

        def fill_wait(j, c):
            zero_copy(0).wait()
            return c
        lax.fori_loop(nu_ref[0], n_blocks, fill_wait, 0)

    def row_copy(tok, row, s):
        return pltpu.make_async_copy(stage.at[s, tok], xr_hbm.at[row], sem.at[s])

    stage[slot] = u2_ref[...]

    def issue(g, c):
        rows = [dest_ref[g * ISSUE_GROUP + u] for u in range(ISSUE_GROUP)]
        for u in range(ISSUE_GROUP):
            row_copy(g * (ISSUE_GROUP // TOP_K) + u // TOP_K, rows[u], slot).start(priority=u % 2)
        return c
    lax.fori_loop(0, tm * TOP_K // ISSUE_GROUP, issue, 0)

    def drain(s):
        def body(t, c):
            for k in range(TOP_K):
                row_copy(0, 0, s).wait()
            return c
        lax.fori_loop(0, tm, body, 0, unroll=8)

    @pl.when(i > 0)
    def _():
        drain(1 - slot)

    @pl.when(i == n - 1)
    def _():
        drain(slot)


def _dispatch(dest_flat, tail, has, n_used, u2t, n_blocks, tm):
    t_all = u2t.shape[0]
    smem = functools.partial(pl.BlockSpec, memory_space=pltpu.SMEM)
    return pl.pallas_call(
        functools.partial(_dispatch_kernel, tm=tm, n_blocks=n_blocks),
        grid=(t_all // tm,),
        in_specs=[
            smem((tm * TOP_K,), lambda i: (i,)),
            smem((N_EXPERTS,), lambda i: (0,)),
            smem((N_EXPERTS,), lambda i: (0,)),
            smem((1,), lambda i: (0,)),
            pl.BlockSpec((tm, SUBLANES, LANES), lambda i: (i, 0, 0)),
        ],
        out_specs=pl.BlockSpec(memory_space=pl.ANY),
        out_shape=jax.ShapeDtypeStruct((n_blocks * MOE_BLOCK, SUBLANES, LANES), F32),
        scratch_shapes=[pltpu.VMEM((2, tm, SUBLANES, LANES), F32), pltpu.VMEM((MOE_BLOCK, SUBLANES, LANES), F32),
                        pltpu.SemaphoreType.DMA((2,)), pltpu.SemaphoreType.DMA],
        compiler_params=_cparams(("arbitrary",)),
        name="dispatch",
    )(dest_flat, tail, has, n_used, u2t)


def _expert_kernel(be_ref, first_ref, nu_ref, x_ref, wgu_ref, bgu_ref, wd_ref, bd_ref, o_ref, wgu_b, wd_b):
    i = pl.program_id(0)

    @pl.when(first_ref[i] == 1)
    def _():
        wgu_b[...] = wgu_ref[0].astype(BF16)
        wd_b[...] = wd_ref[0].astype(BF16)

    @pl.when(i < nu_ref[0])
    def _():
        x = x_ref[...].reshape(MOE_BLOCK, D_MODEL).astype(BF16)
        gu = _dot(x, wgu_b[...]) + bgu_ref[0]
        gate = jnp.minimum(gu[:, :D_EXPERT], SWIGLU_LIMIT)
        up = jnp.clip(gu[:, D_EXPERT:], -SWIGLU_LIMIT, SWIGLU_LIMIT)
        act = (up + 1.0) * (gate * _sigmoid(SWIGLU_ALPHA * gate))
        y = _dot(act.astype(BF16), wd_b[...]) + bd_ref[0]
        o_ref[...] = y.reshape(o_ref.shape)

    @pl.when(i >= nu_ref[0])
    def _():
        o_ref[...] = jnp.zeros(o_ref.shape, F32)


def _experts(block_e, first, n_used, xr, wgu, bgu, wd, bd):
    n_rows = xr.shape[0]
    n_blocks = n_rows // MOE_BLOCK
    row_block = (MOE_BLOCK, SUBLANES, LANES)
    return pl.pallas_call(
        _expert_kernel,
        grid_spec=pltpu.PrefetchScalarGridSpec(
            num_scalar_prefetch=3,
            grid=(n_blocks,),
            in_specs=[
                pl.BlockSpec(row_block, lambda i, be, fi, nu: (jnp.minimum(i, nu[0] - 1), 0, 0)),
                pl.BlockSpec((1, D_MODEL, 2 * D_EXPERT), lambda i, be, fi, nu: (be[i], 0, 0)),
                pl.BlockSpec((1, 1, 2 * D_EXPERT), lambda i, be, fi, nu: (be[i], 0, 0)),
                pl.BlockSpec((1, D_EXPERT, D_MODEL), lambda i, be, fi, nu: (be[i], 0, 0)),
                pl.BlockSpec((1, 1, D_MODEL), lambda i, be, fi, nu: (be[i], 0, 0)),
            ],
            out_specs=pl.BlockSpec(row_block, lambda i, be, fi, nu: (i, 0, 0)),
            scratch_shapes=[pltpu.VMEM((D_MODEL, 2 * D_EXPERT), BF16), pltpu.VMEM((D_EXPERT, D_MODEL), BF16)],
        ),
        out_shape=jax.ShapeDtypeStruct((n_rows, SUBLANES, LANES), F32),
        compiler_params=pltpu.CompilerParams(dimension_semantics=("arbitrary",),
                                             vmem_limit_bytes=56 * 1024 * 1024),
        name="experts",
    )(block_e, first, n_used, xr, wgu, bgu, wd, bd)


def _final_kernel(dcur_ref, dnxt_ref, x1_ref, g_ref, nf_ref, yr_hbm, op_ref, os_ref, buf, sem,
                  *, n_prompt_blocks, tm):
    i = pl.program_id(0)
    n = pl.num_programs(0)
    slot = i % 2

    def row_copy(row, s, k, t):
        dst = buf.at[s, k, pl.ds(pl.multiple_of(t * SUBLANES, SUBLANES), SUBLANES)]
        return pltpu.make_async_copy(yr_hbm.at[row], dst, sem.at[s])

    def issue(dest_ref, s):
        def body(g, c):
            rows = [dest_ref[g * ISSUE_GROUP + u] for u in range(ISSUE_GROUP)]
            for u in range(ISSUE_GROUP):
                t = g * (ISSUE_GROUP // TOP_K) + u // TOP_K
                row_copy(rows[u], s, u % TOP_K, t).start(priority=u % 2)
            return c
        lax.fori_loop(0, tm * TOP_K // ISSUE_GROUP, body, 0)

    @pl.when(i == 0)
    def _():
        issue(dcur_ref, 0)

    @pl.when(i + 1 < n)
    def _():
        issue(dnxt_ref, 1 - slot)

    def drain(t, c):
        for k in range(TOP_K):
            row_copy(0, slot, k, t).wait()
        return c
    lax.fori_loop(0, tm, drain, 0, unroll=8)

    g = g_ref[...]
    parts = []
    for c in range(D_MODEL // LANES):
        moe = buf[slot, 0, pl.ds(c, tm, stride=SUBLANES), :] * g[:, 0:1]
        for k in range(1, TOP_K):
            moe = moe + buf[slot, k, pl.ds(c, tm, stride=SUBLANES), :] * g[:, k:k + 1]
        parts.append(x1_ref[:, LANES * c:LANES * (c + 1)] + moe)
    x2 = jnp.concatenate(parts, axis=1)
    ms = jnp.mean(x2 * x2, axis=-1, keepdims=True)
    y = x2 * lax.rsqrt(ms + EPS) * nf_ref[...]

    @pl.when(i < n_prompt_blocks)
    def _():
        op_ref[...] = y

    @pl.when(i >= n_prompt_blocks)
    def _():
        os_ref[...] = y


def _final(dest_flat, x1, yr, gates, nf, t_prompt, tm):
    t_all = x1.shape[0]
    npb = t_prompt // tm
    nsteps = t_all // tm
    return pl.pallas_call(
        functools.partial(_final_kernel, n_prompt_blocks=npb, tm=tm),
        grid=(nsteps,),
        in_specs=[
            pl.BlockSpec((tm * TOP_K,), lambda i: (i,), memory_space=pltpu.SMEM),
            pl.BlockSpec((tm * TOP_K,), lambda i: (jnp.minimum(i + 1, nsteps - 1),), memory_space=pltpu.SMEM),
            pl.BlockSpec((tm, D_MODEL), lambda i: (i, 0)),
            pl.BlockSpec((tm, LANES), lambda i: (i, 0)),
            pl.BlockSpec((1, D_MODEL), lambda i: (0, 0)),
            pl.BlockSpec(memory_space=pl.ANY),
        ],
        out_specs=[
            pl.BlockSpec((tm, D_MODEL), lambda i: (jnp.minimum(i, npb - 1), 0)),
            pl.BlockSpec((tm, D_MODEL), lambda i: (jnp.maximum(i - npb, 0), 0)),
        ],
        out_shape=[
            jax.ShapeDtypeStruct((t_prompt, D_MODEL), F32),
            jax.ShapeDtypeStruct((t_all - t_prompt, D_MODEL), F32),
        ],
        scratch_shapes=[pltpu.VMEM((2, TOP_K, tm * SUBLANES, LANES), F32), pltpu.SemaphoreType.DMA((2,))],
        compiler_params=_cparams(("arbitrary",)),
        name="final",
    )(dest_flat, dest_flat, x1, gates, nf, yr)


def _block_diag4(w):
    w4 = w.reshape(4, 4, RNN_BLOCK, RNN_BLOCK)
    eye = jnp.eye(4, dtype=w.dtype)
    return jnp.einsum('ghij,hk->ghikj', w4, eye).reshape(4, 256, 256)


def _route_tables(route_i, cnt, t_all):
    rank = route_i[:, 0:TOP_K]
    expert = route_i[:, TOP_K:2 * TOP_K]
    counts = cnt[0, :N_EXPERTS].astype(jnp.int32)
    padded = (counts + MOE_BLOCK - 1) // MOE_BLOCK * MOE_BLOCK
    pend = jnp.cumsum(padded)
    pstart = pend - padded
    dest = jnp.take(pstart, expert) + rank
    n_blocks = _moe_blocks(t_all)
    n_used = (pend[-1] // MOE_BLOCK).astype(jnp.int32)
    blk = jnp.minimum(jnp.arange(n_blocks, dtype=jnp.int32), n_used - 1) * MOE_BLOCK
    block_e = jnp.minimum(jnp.sum((pend[None, :] <= blk[:, None]).astype(jnp.int32), axis=1), N_EXPERTS - 1)
    first = jnp.concatenate([jnp.ones((1,), jnp.int32), (block_e[1:] != block_e[:-1]).astype(jnp.int32)])
    tail = jnp.maximum(pend - MOE_BLOCK, 0)
    has = (counts > 0).astype(jnp.int32)
    return dest.reshape(-1), tail, has, block_e, first, n_used.reshape(1)


def _moe_blocks(t_all):
    return pl.cdiv(t_all * TOP_K, MOE_BLOCK) + N_EXPERTS


def kernel(x_prompt, x_sample, state_lru_conv, state_lru_h, state_ssd_conv, state_ssd, norm_mix, w_in, conv_lru_w, conv_lru_b, w_rg, b_rg, w_ig, b_ig, lam, w_proj_a, conv_ssd_w, conv_ssd_b, dt_bias, a_log, d_skip, ssd_norm, w_proj_b, w_out, norm_ffn, w_router, b_router, w_gate_up, b_gate_up, w_down, b_down, norm_final):
    nbp, seq, _ = x_prompt.shape
    nbs = x_sample.shape[0]
    t_p = nbp * seq
    t_all = t_p + nbs
    l = 0

    wi = w_in[l]
    s = (0, 1024, 2048, 4096, 7168, 7200, 8224, 9248)
    w_main = jnp.concatenate([wi[:, s[2]:s[3]], wi[:, s[0]:s[1]], wi[:, s[3]:s[4]], wi[:, s[1]:s[2]],
                              wi[:, s[5]:s[6]], wi[:, s[6]:s[7]]], axis=1).astype(BF16)
    w_dt = jnp.pad(wi[:, s[4]:s[5]], ((0, 0), (0, LANES - SSD_HEADS))).astype(BF16)
    g_mix = norm_mix[l].reshape(1, D_MODEL)
    wg = jnp.concatenate([_block_diag4(w_rg[l]), _block_diag4(w_ig[l])], axis=2).astype(BF16)
    bg = jnp.concatenate([b_rg[l].reshape(4, 1, 256), b_ig[l].reshape(4, 1, 256)], axis=2)
    lam_r = lam[l].reshape(1, D_RNN)
    lcw, lcb = conv_lru_w[l], conv_lru_b[l].reshape(1, D_RNN)
    scw, scb = conv_ssd_w[l], conv_ssd_b[l].reshape(1, SSD_CONV_DIM)
    dtb = jnp.pad(dt_bias[l], (0, LANES - SSD_HEADS)).reshape(1, LANES)
    a_row = jnp.pad(-jnp.exp(a_log[l]), (0, LANES - SSD_HEADS)).reshape(1, LANES)
    dsk = jnp.repeat(d_skip[l], SSD_HEAD_DIM).reshape(1, D_INNER)
    nrm = ssd_norm[l].reshape(1, D_INNER)
    e_mat = (jnp.arange(LANES)[:, None] == (jnp.arange(D_INNER) // SSD_HEAD_DIM)[None, :]).astype(F32)
    wa, wb, wo = w_proj_a[l].astype(BF16), w_proj_b[l].astype(BF16), w_out[l].astype(BF16)
    nf = norm_ffn[l].reshape(1, D_MODEL)
    wr = jnp.pad(w_router[l], ((0, 0), (0, LANES - N_EXPERTS))).astype(BF16)
    br = jnp.pad(b_router[l], (0, LANES - N_EXPERTS)).reshape(1, LANES)
    bgu =b_gate_up[l].reshape(N_EXPERTS, 1, 2 * D_EXPERT)
    bd = b_down[l].reshape(N_EXPERTS, 1, D_MODEL)

    xp = x_prompt.reshape(t_p, D_MODEL)
    proj_p, dt_p = _inproj(xp, g_mix, w_main, w_dt, tm=2048)
    ga_p, p_lc, p_lh = _lru_prompt(proj_p, lcw, lcb, wg, bg, lam_r, nbp, seq, tl=512)
    yb_p, p_sc, p_ss = _ssd_prompt(proj_p, dt_p, scw, scb, dtb, a_row, dsk, nrm, e_mat, nbp, seq)

    xs_in = x_sample.reshape(nbs, D_MODEL)
    proj_s, dt_s = _inproj(xs_in, g_mix, w_main, w_dt, tm=nbs)
    (ga_s, s_lc, s_lh, s_sc, xdt_t, da, b_s, c_s, xs_s) = _sample_pre(
        proj_s, dt_s, state_lru_conv[l].reshape(nbs, 3 * D_RNN), state_lru_h[l],
        state_ssd_conv[l].reshape(nbs, 3 * SSD_CONV_DIM), lcw, lcb, wg, bg, lam_r, scw, scb, dtb, a_row, e_mat)
    s_ss, y_t = _sample_state(da[:, :SSD_HEADS].reshape(-1), state_ssd[l].reshape(nbs, D_INNER, D_STATE),
                              xdt_t, b_s.reshape(nbs, 1, -1), c_s.reshape(nbs, 1, -1))
    yb_s = _sample_post(y_t, xs_s, proj_s, dsk, nrm)

    cnt0 = jnp.zeros((1, LANES), F32)
    *prev, cnt_p = _merge(ga_p, yb_p, proj_p, xp, wa, wb, wo, nf, wr, br, cnt0, tm=512, t_all=t_all, row0=0,
                          prev=None)
    x1, u2, route_f, route_i, cnt = _merge(ga_s, yb_s, proj_s, xs_in, wa, wb, wo, nf, wr, br, cnt_p, tm=nbs,
                                           t_all=t_all, row0=t_p, prev=prev)

    dest, tail, has, block_e, first, n_used = _route_tables(route_i, cnt, t_all)
    xr = _dispatch(dest, tail, has, n_used, u2, _moe_blocks(t_all), tm=nbs)
    yr = _experts(block_e, first, n_used, xr, w_gate_up[l], bgu, w_down[l], bd)
    y_p, y_s = _final(dest, x1, yr, route_f, norm_final.reshape(1, D_MODEL), t_p, tm=nbs)

    return (y_p.reshape(nbp, seq, D_MODEL), y_s.reshape(nbs, 1, D_MODEL),
            p_lc[None], p_lh.reshape(1, nbp, D_RNN), p_sc[None],
            p_ss.reshape(1, nbp, SSD_HEADS, SSD_HEAD_DIM, D_STATE),
            s_lc.reshape(1, nbs, CONV_W - 1, D_RNN), s_lh[None],
            s_sc.reshape(1, nbs, CONV_W - 1, SSD_CONV_DIM),
            s_ss.reshape(1, nbs, SSD_HEADS, SSD_HEAD_DIM, D_STATE))
```

```python
import functools

import jax
import jax.numpy as jnp
from jax import lax
from jax.experimental import pallas as pl
from jax.experimental.pallas import tpu as pltpu

F32 = jnp.float32
BF16 = jnp.bfloat16

D_MODEL = 1024
D_RNN = 1024
RNN_HEADS = 16
RNN_BLOCK = 64
CONV_W = 4
RG_C = 8.0
D_INNER = 2048
SSD_HEAD_DIM = 64
SSD_HEADS = 32
SSD_GROUPS = 4
D_STATE = 128
SSD_CONV_DIM = D_INNER + 2 * SSD_GROUPS * D_STATE
CHUNK = 128
N_EXPERTS = 32
TOP_K = 4
D_EXPERT = 1024
SWIGLU_LIMIT = 7.0
SWIGLU_ALPHA = 1.702
EPS = 1e-6

LANES = 128
SUBLANES = 8
N_MAIN = 9 * 1024
NJ_F32 = 4
COL_XBC, COL_LRU = 0, 3
COL_Z, COL_GATE, COL_GA, COL_GB = 0, 2, 3, 4
MOE_BLOCK = 512
MOE_SUB = 256
SAMPLE_STATE_ROWS = 2
ISSUE_GROUP = 32
VMEM_LIMIT = 48 * 1024 * 1024


def _cparams(sem):
    return pltpu.CompilerParams(dimension_semantics=sem, vmem_limit_bytes=VMEM_LIMIT)


def _sigmoid(x):
    return jax.nn.sigmoid(x)


def _softplus(x):
    return jnp.maximum(x, 0.0) + jnp.log1p(jnp.exp(-jnp.abs(x)))


def _gelu_tanh(x):
    return 0.5 * x * (1.0 + jnp.tanh(0.7978845608028654 * (x + 0.044715 * (x * x * x))))


def _dot(a, b):
    return jnp.dot(a, b, preferred_element_type=F32)


def _dot_exact(a, b):
    return jnp.dot(a, b, preferred_element_type=F32, precision=lax.Precision.HIGHEST)


def _inproj_kernel(x_ref, g_ref, w_ref, wdt_ref, of_ref, ob_ref, dt_ref, u_ref):
    j = pl.program_id(1)

    @pl.when(j == 0)
    def _():
        x = x_ref[...]
        ms = jnp.mean(x * x, axis=-1, keepdims=True)
        u = (x * lax.rsqrt(ms + EPS) * g_ref[...]).astype(BF16)
        u_ref[...] = u
        dt_ref[...] = _dot(u, wdt_ref[...])

    acc = _dot(u_ref[...], w_ref[...])

    @pl.when(j < NJ_F32)
    def _():
        of_ref[...] = acc

    @pl.when(j >= NJ_F32)
    def _():
        ob_ref[...] = acc.astype(BF16)


def _inproj(x, g, w_main, w_dt, tm):
    t = x.shape[0]
    nj = N_MAIN // 1024
    return pl.pallas_call(
        _inproj_kernel,
        grid=(t // tm, nj),
        in_specs=[
            pl.BlockSpec((tm, D_MODEL), lambda i, j: (i, 0), pipeline_mode=pl.Buffered(1)),
            pl.BlockSpec((1, D_MODEL), lambda i, j: (0, 0)),
            pl.BlockSpec((D_MODEL, 1024), lambda i, j: (0, j)),
            pl.BlockSpec((D_MODEL, LANES), lambda i, j: (0, 0)),
        ],
        out_specs=[
            pl.BlockSpec((tm, 1024), lambda i, j: (i, jnp.minimum(j, NJ_F32 - 1))),
            pl.BlockSpec((tm, 1024), lambda i, j: (i, jnp.maximum(j - NJ_F32, 0))),
            pl.BlockSpec((tm, LANES), lambda i, j: (i, 0)),
        ],
        out_shape=[jax.ShapeDtypeStruct((t, NJ_F32 * 1024), F32),
                   jax.ShapeDtypeStruct((t, (nj - NJ_F32) * 1024), BF16),
                   jax.ShapeDtypeStruct((t, LANES), F32)],
        scratch_shapes=[pltpu.VMEM((tm, D_MODEL), BF16)],
        compiler_params=pltpu.CompilerParams(dimension_semantics=("parallel", "arbitrary"),
                                             vmem_limit_bytes=56 * 1024 * 1024),
        name="inproj",
    )(x, g, w_main, w_dt)


def _lru_gates(xc, wg_ref, bg_ref, lam):
    xcb = xc.astype(BF16)
    sp = _softplus(-lam)
    a_parts, b_parts = [], []
    for g in range(4):
        sl = slice(256 * g, 256 * (g + 1))
        pre = _dot(xcb[:, sl], wg_ref[g]) + bg_ref[g]
        r = _sigmoid(pre[:, :256])
        i = _sigmoid(pre[:, 256:])
        log_a = (-RG_C * r) * sp[:, sl]
        a_parts.append(jnp.exp(log_a))
        th = jnp.tanh(log_a)
        mult = jnp.sqrt(-2.0 * th / (1.0 - th))
        b_parts.append(mult * (i * xc[:, sl]))
    return jnp.concatenate(a_parts, axis=1), jnp.concatenate(b_parts, axis=1)


def _lru_kernel(x_ref, gate_ref, cw_ref, cb_ref, wg_ref, bg_ref, lam_ref,
                out_ref, conv_ref, h_ref, xx_ref, a_ref, b_ref, hc_ref, *, tl):
    @pl.when(pl.program_id(1) == 0)
    def _():
        xx_ref[0:SUBLANES, :] = jnp.zeros((SUBLANES, D_RNN), F32)
        hc_ref[...] = jnp.zeros((SUBLANES, D_RNN), F32)

    x = x_ref[...]
    xx_ref[SUBLANES:SUBLANES + tl, :] = x
    cw = cw_ref[...]
    xc = (cw[3:4] * x + cw[2:3] * xx_ref[7:7 + tl, :] + cw[1:2] * xx_ref[6:6 + tl, :]
          + cw[0:1] * xx_ref[5:5 + tl, :] + cb_ref[...])
    xx_ref[0:SUBLANES, :] = xx_ref[tl:tl + SUBLANES, :]
    conv_ref[0] = x_ref[tl - 3:tl, :]

    a, b = _lru_gates(xc, wg_ref, bg_ref, lam_ref[...])
    nt = tl // SUBLANES
    a = a.reshape(nt, SUBLANES, D_RNN)
    b = b.reshape(nt, SUBLANES, D_RNN)
    rows = lax.broadcasted_iota(jnp.int32, (nt, SUBLANES, D_RNN), 1)
    for d in (1, 2, 4):
        m = rows >= d
        b = jnp.where(m, a * pltpu.roll(b, d, 1) + b, b)
        a = jnp.where(m, a * pltpu.roll(a, d, 1), a)
    a_ref[...] = a
    b_ref[...] = b

    def carry(k, hprev):
        h = a_ref[k] * hprev + b_ref[k]
        b_ref[k] = h
        return jnp.broadcast_to(h[SUBLANES - 1:SUBLANES, :], (SUBLANES, D_RNN))

    hlast = lax.fori_loop(0, nt, carry, hc_ref[...])
    hc_ref[...] = hlast
    h_ref[0] = hlast[0:1, :]
    h_all = b_ref[...].reshape(tl, D_RNN)
    out_ref[...] = (_gelu_tanh(gate_ref[...].astype(F32)) * h_all).astype(BF16)


def _lru_prompt(proj, projb, cw, cb, wg, bg, lam, nb, seq, tl):
    nl = seq // tl
    return pl.pallas_call(
        functools.partial(_lru_kernel, tl=tl),
        grid=(nb, nl),
        in_specs=[
            pl.BlockSpec((tl, 1024), lambda b, l: (b * nl + l, COL_LRU)),
            pl.BlockSpec((tl, 1024), lambda b, l: (b * nl + l, COL_GATE)),
            pl.BlockSpec((CONV_W, D_RNN), lambda b, l: (0, 0)),
            pl.BlockSpec((1, D_RNN), lambda b, l: (0, 0)),
            pl.BlockSpec((4, 256, 512), lambda b, l: (0, 0, 0)),
            pl.BlockSpec((4, 1, 512), lambda b, l: (0, 0, 0)),
            pl.BlockSpec((1, D_RNN), lambda b, l: (0, 0)),
        ],
        out_specs=[
            pl.BlockSpec((tl, D_RNN), lambda b, l: (b * nl + l, 0)),
            pl.BlockSpec((1, CONV_W - 1, D_RNN), lambda b, l: (b, 0, 0)),
            pl.BlockSpec((1, 1, D_RNN), lambda b, l: (b, 0, 0)),
        ],
        out_shape=[
            jax.ShapeDtypeStruct((nb * seq, D_RNN), BF16),
            jax.ShapeDtypeStruct((nb, CONV_W - 1, D_RNN), F32),
            jax.ShapeDtypeStruct((nb, 1, D_RNN), F32),
        ],
        scratch_shapes=[
            pltpu.VMEM((tl + SUBLANES, D_RNN), F32),
            pltpu.VMEM((tl // SUBLANES, SUBLANES, D_RNN), F32),
            pltpu.VMEM((tl // SUBLANES, SUBLANES, D_RNN), F32),
            pltpu.VMEM((SUBLANES, D_RNN), F32),
        ],
        compiler_params=_cparams(("parallel", "arbitrary")),
        name="lru_prompt",
    )(proj, projb, cw, cb, wg, bg, lam)


def _group_norm_gate(y, z, nrm):
    y = y * (z * _sigmoid(z))
    gw = D_INNER // SSD_GROUPS
    parts = []
    for g in range(SSD_GROUPS):
        yg = y[:, gw * g:gw * (g + 1)]
        ms = jnp.mean(yg * yg, axis=-1, keepdims=True)
        parts.append(yg * lax.rsqrt(ms + EPS))
    return jnp.concatenate(parts, axis=1) * nrm


def _ssd_kernel(xbc_ref, z_ref, dt_ref, cw_ref, cb_ref, dtb_ref, a_ref, dsk_ref, nrm_ref, e_ref,
                y_ref, conv_ref, st_ref, xx_ref, stt_ref, yacc_ref):
    q = CHUNK

    @pl.when(pl.program_id(1) == 0)
    def _():
        xx_ref[0:SUBLANES, :] = jnp.zeros((SUBLANES, SSD_CONV_DIM), F32)
        stt_ref[...] = jnp.zeros((D_STATE, D_INNER), F32)

    xbc = xbc_ref[...]
    xx_ref[SUBLANES:SUBLANES + q, :] = xbc
    cw = cw_ref[...]
    conv = (cw[3:4] * xbc + cw[2:3] * xx_ref[7:7 + q, :] + cw[1:2] * xx_ref[6:6 + q, :]
            + cw[0:1] * xx_ref[5:5 + q, :] + cb_ref[...])
    xx_ref[0:SUBLANES, :] = xx_ref[q:q + SUBLANES, :]
    conv_ref[0] = xbc_ref[q - 3:q, :]
    act = conv * _sigmoid(conv)
    xs = act[:, :D_INNER]
    bm = act[:, D_INNER:D_INNER + SSD_GROUPS * D_STATE]
    cm = act[:, D_INNER + SSD_GROUPS * D_STATE:]

    dt = _softplus(dt_ref[...] + dtb_ref[...])
    dta = dt * a_ref[...]
    iq = lax.broadcasted_iota(jnp.int32, (q, q), 0)
    ik = lax.broadcasted_iota(jnp.int32, (q, q), 1)
    causal = iq >= ik
    a_cum = _dot_exact(causal.astype(F32), dta)
    a_cum_t = a_cum.T
    dt_t = dt.T
    e_cum = jnp.exp(a_cum)
    w_t = jnp.exp(a_cum_t[:, q - 1:q] - a_cum_t) * dt_t
    a_last = jnp.broadcast_to(a_cum[q - 1:q, :], (SUBLANES, LANES))
    da_e = jnp.exp(_dot_exact(a_last, e_ref[...])[0:1, :])
    lo = ik < SSD_HEAD_DIM

    for g in range(SSD_GROUPS):
        bg = bm[:, D_STATE * g:D_STATE * (g + 1)]
        cg = cm[:, D_STATE * g:D_STATE * (g + 1)]
        cb = lax.dot_general(cg.astype(BF16), bg.astype(BF16), (((1,), (1,)), ((), ())),
                             preferred_element_type=F32)
        bg_t = bg.T
        for pp in range(4):
            j = 4 * g + pp
            ms, cs, bw = [], [], []
            for s in range(2):
                h = 2 * j + s
                col = a_cum[:, h:h + 1]
                row = a_cum_t[h:h + 1, :]
                dec = jnp.exp(jnp.where(causal, col - row, -jnp.inf))
                ms.append(cb * dec * dt_t[h:h + 1, :])
                cs.append(cg * e_cum[:, h:h + 1])
                bw.append(bg_t * w_t[h:h + 1, :])
            sl = slice(LANES * j, LANES * (j + 1))
            xp = xs[:, sl]
            rhs_x = jnp.concatenate([jnp.where(lo, xp, 0.0), jnp.where(lo, 0.0, xp)], axis=0).astype(BF16)
            stp = stt_ref[:, sl]
            rhs_s = jnp.concatenate([jnp.where(lo, stp, 0.0), jnp.where(lo, 0.0, stp)], axis=0).astype(BF16)
            l_m = jnp.concatenate(ms, axis=1).astype(BF16)
            l_c = jnp.concatenate(cs, axis=1).astype(BF16)
            l_b = jnp.concatenate(bw, axis=1).astype(BF16)
            yacc_ref[:, sl] = _dot(l_m, rhs_x) + _dot(l_c, rhs_s)
            stt_ref[:, sl] = stp * da_e[:, sl] + _dot(l_b, rhs_x)

    y = yacc_ref[...] + dsk_ref[...] * xs
    y_ref[...] = _group_norm_gate(y, z_ref[...].astype(F32), nrm_ref[...]).astype(BF16)

    @pl.when(pl.program_id(1) == pl.num_programs(1) - 1)
    def _():
        st_ref[0] = stt_ref[...].T


def _ssd_prompt(proj, projb, dt, cw, cb, dtb, a_row, dsk, nrm, e_mat, nb, seq):
    nc = seq // CHUNK
    return pl.pallas_call(
        _ssd_kernel,
        grid=(nb, nc),
        in_specs=[
            pl.BlockSpec((CHUNK, SSD_CONV_DIM), lambda b, c: (b * nc + c, COL_XBC)),
            pl.BlockSpec((CHUNK, D_INNER), lambda b, c: (b * nc + c, COL_Z)),
            pl.BlockSpec((CHUNK, LANES), lambda b, c: (b * nc + c, 0)),
            pl.BlockSpec((CONV_W, SSD_CONV_DIM), lambda b, c: (0, 0)),
            pl.BlockSpec((1, SSD_CONV_DIM), lambda b, c: (0, 0)),
            pl.BlockSpec((1, LANES), lambda b, c: (0, 0)),
            pl.BlockSpec((1, LANES), lambda b, c: (0, 0)),
            pl.BlockSpec((1, D_INNER), lambda b, c: (0, 0)),
            pl.BlockSpec((1, D_INNER), lambda b, c: (0, 0)),
            pl.BlockSpec((LANES, D_INNER), lambda b, c: (0, 0)),
        ],
        out_specs=[
            pl.BlockSpec((CHUNK, D_INNER), lambda b, c: (b * nc + c, 0)),
            pl.BlockSpec((1, CONV_W - 1, SSD_CONV_DIM), lambda b, c: (b, 0, 0)),
            pl.BlockSpec((1, D_INNER, D_STATE), lambda b, c: (b, 0, 0)),
        ],
        out_shape=[
            jax.ShapeDtypeStruct((nb * seq, D_INNER), BF16),
            jax.ShapeDtypeStruct((nb, CONV_W - 1, SSD_CONV_DIM), F32),
            jax.ShapeDtypeStruct((nb, D_INNER, D_STATE), F32),
        ],
        scratch_shapes=[
            pltpu.VMEM((CHUNK + SUBLANES, SSD_CONV_DIM), F32),
            pltpu.VMEM((D_STATE, D_INNER), F32),
            pltpu.VMEM((CHUNK, D_INNER), F32),
        ],
        compiler_params=_cparams(("parallel", "arbitrary")),
        name="ssd_prompt",
    )(proj, projb, dt, cw, cb, dtb, a_row, dsk, nrm, e_mat)


def _sample_pre_kernel(proj_ref, projb_ref, dt_ref, lconv_ref, h0_ref, sconv_ref,
                       lcw_ref, lcb_ref, wg_ref, bg_ref, lam_ref,
                       scw_ref, scb_ref, dtb_ref, a_ref, e_ref,
                       ga_ref, lconv_o, h_o, sconv_o, xdt_t_o, da_o, b_o, c_o, xs_o):
    x = proj_ref[:, SSD_CONV_DIM:SSD_CONV_DIM + D_RNN]
    gate = projb_ref[:, D_INNER:D_INNER + D_RNN].astype(F32)
    cw = lcw_ref[...]
    c0 = lconv_ref[:, 0:1024]
    c1 = lconv_ref[:, 1024:2048]
    c2 = lconv_ref[:, 2048:3072]
    xc = cw[3:4] * x + cw[2:3] * c2 + cw[1:2] * c1 + cw[0:1] * c0 + lcb_ref[...]
    lconv_o[:, 0:1024] = c1
    lconv_o[:, 1024:2048] = c2
    lconv_o[:, 2048:3072] = x
    a, bt = _lru_gates(xc, wg_ref, bg_ref, lam_ref[...])
    h = a * h0_ref[...] + bt
    h_o[...] = h
    ga_ref[...] = (_gelu_tanh(gate) * h).astype(BF16)

    xbc = proj_ref[:, 0:SSD_CONV_DIM]
    sw = scw_ref[...]
    w = SSD_CONV_DIM
    s0 = sconv_ref[:, 0:w]
    s1 = sconv_ref[:, w:2 * w]
    s2 = sconv_ref[:, 2 * w:3 * w]
    conv = sw[3:4] * xbc + sw[2:3] * s2 + sw[1:2] * s1 + sw[0:1] * s0 + scb_ref[...]
    sconv_o[:, 0:w] = s1
    sconv_o[:, w:2 * w] = s2
    sconv_o[:, 2 * w:3 * w] = xbc
    act = conv * _sigmoid(conv)
    xs = act[:, :D_INNER]
    xs_o[...] = xs
    b_o[...] = act[:, D_INNER:D_INNER + SSD_GROUPS * D_STATE]
    c_o[...] = act[:, D_INNER + SSD_GROUPS * D_STATE:]
    dt = _softplus(dt_ref[...] + dtb_ref[...])
    da_o[...] = jnp.exp(dt * a_ref[...])
    dt_e = _dot_exact(dt, e_ref[...])
    xdt_t_o[...] = (xs * dt_e).T


def _sample_pre(proj, projb, dt, lconv, h0, sconv, lcw, lcb, wg, bg, lam, scw, scb, dtb, a_row, e_mat):
    nb = proj.shape[0]
    out_shape = [
        jax.ShapeDtypeStruct((nb, D_RNN), BF16),
        jax.ShapeDtypeStruct((nb, 3 * D_RNN), F32),
        jax.ShapeDtypeStruct((nb, D_RNN), F32),
        jax.ShapeDtypeStruct((nb, 3 * SSD_CONV_DIM), F32),
        jax.ShapeDtypeStruct((D_INNER, nb), F32),
        jax.ShapeDtypeStruct((nb, LANES), F32),
        jax.ShapeDtypeStruct((nb, SSD_GROUPS * D_STATE), F32),
        jax.ShapeDtypeStruct((nb, SSD_GROUPS * D_STATE), F32),
        jax.ShapeDtypeStruct((nb, D_INNER), F32),
    ]
    return pl.pallas_call(
        _sample_pre_kernel,
        out_shape=out_shape,
        compiler_params=pltpu.CompilerParams(vmem_limit_bytes=VMEM_LIMIT),
        name="sample_pre",
    )(proj, projb, dt, lconv, h0, sconv, lcw, lcb, wg, bg, lam, scw, scb, dtb, a_row, e_mat)


def _sample_state_kernel(da_ref, s0_ref, xdt_t_ref, b_ref, c_ref, s1_ref, y_t_ref):
    nb = xdt_t_ref.shape[1]

    @pl.when(pl.program_id(0) == 0)
    def _():
        y_t_ref[...] = jnp.zeros(y_t_ref.shape, F32)

    lane = lax.broadcasted_iota(jnp.int32, (D_INNER, nb), 1)
    for j in range(SAMPLE_STATE_ROWS):
        b = pl.program_id(0) * SAMPLE_STATE_ROWS + j
        sel = lane == b
        xcol = jnp.sum(jnp.where(sel, xdt_t_ref[...], 0.0), axis=1, keepdims=True)
        ycols = []
        for g in range(SSD_GROUPS):
            brow = b_ref[j, :, D_STATE * g:D_STATE * (g + 1)]
            crow = c_ref[j, :, D_STATE * g:D_STATE * (g + 1)]
            for hh in range(SSD_HEADS // SSD_GROUPS):
                h = (SSD_HEADS // SSD_GROUPS) * g + hh
                sl = slice(SSD_HEAD_DIM * h, SSD_HEAD_DIM * (h + 1))
                s1 = s0_ref[j, sl, :] * da_ref[b * SSD_HEADS + h] + xcol[sl, :] * brow
                s1_ref[j, sl, :] = s1
                ycols.append(jnp.sum(s1 * crow, axis=1, keepdims=True))
        ycol = jnp.concatenate(ycols, axis=0)
        y_t_ref[...] = jnp.where(sel, ycol, y_t_ref[...])


def _sample_state(da_flat, s0, xdt_t, bmat, cmat):
    nb = s0.shape[0]
    return pl.pallas_call(
        _sample_state_kernel,
        grid_spec=pltpu.PrefetchScalarGridSpec(
            num_scalar_prefetch=1,
            grid=(nb // SAMPLE_STATE_ROWS,),
            in_specs=[
                pl.BlockSpec((SAMPLE_STATE_ROWS, D_INNER, D_STATE), lambda b, da: (b, 0, 0)),
                pl.BlockSpec((D_INNER, nb), lambda b, da: (0, 0)),
                pl.BlockSpec((SAMPLE_STATE_ROWS, 1, SSD_GROUPS * D_STATE), lambda b, da: (b, 0, 0)),
                pl.BlockSpec((SAMPLE_STATE_ROWS, 1, SSD_GROUPS * D_STATE), lambda b, da: (b, 0, 0)),
            ],
            out_specs=[
                pl.BlockSpec((SAMPLE_STATE_ROWS, D_INNER, D_STATE), lambda b, da: (b, 0, 0)),
                pl.BlockSpec((D_INNER, nb), lambda b, da: (0, 0)),
            ],
        ),
        out_shape=[
            jax.ShapeDtypeStruct((nb, D_INNER, D_STATE), F32),
            jax.ShapeDtypeStruct((D_INNER, nb), F32),
        ],
        compiler_params=_cparams(("arbitrary",)),
        name="sample_state",
    )(da_flat, s0, xdt_t, bmat, cmat)


def _sample_post_kernel(y_t_ref, xs_ref, z_ref, dsk_ref, nrm_ref, y_ref):
    y = y_t_ref[...].T + dsk_ref[...] * xs_ref[...]
    y_ref[...] = _group_norm_gate(y, z_ref[...].astype(F32), nrm_ref[...]).astype(BF16)


def _sample_post(y_t, xs, projb, dsk, nrm):
    nb = xs.shape[0]
    return pl.pallas_call(
        _sample_post_kernel,
        grid=(1,),
        in_specs=[
            pl.BlockSpec((D_INNER, nb), lambda i: (0, 0)),
            pl.BlockSpec((nb, D_INNER), lambda i: (0, 0)),
            pl.BlockSpec((nb, D_INNER), lambda i: (0, COL_Z)),
            pl.BlockSpec((1, D_INNER), lambda i: (0, 0)),
            pl.BlockSpec((1, D_INNER), lambda i: (0, 0)),
        ],
        out_specs=pl.BlockSpec((nb, D_INNER), lambda i: (0, 0)),
        out_shape=jax.ShapeDtypeStruct((nb, D_INNER), BF16),
        compiler_params=_cparams(("arbitrary",)),
        name="sample_post",
    )(y_t, xs, projb, dsk, nrm)


def _merge_kernel(ga_ref, yb_ref, gta_ref, gtb_ref, x_ref, wa_ref, wb_ref, wo_ref, nf_ref, wr_ref, br_ref,
                  cnt0_ref, *refs, n_real):
    x1_ref, u2_ref, rf_ref, ri_ref, cnt_ref, carry_ref = refs[-6:]

    @pl.when(pl.program_id(0) == 0)
    def _():
        carry_ref[...] = cnt0_ref[...]

    @pl.when(pl.program_id(0) >= n_real)
    def _():
        x1_ref[...] = jnp.zeros(x1_ref.shape, x1_ref.dtype)
        u2_ref[...] = jnp.zeros(u2_ref.shape, u2_ref.dtype)
        rf_ref[...] = jnp.zeros(rf_ref.shape, rf_ref.dtype)
        ri_ref[...] = jnp.zeros(ri_ref.shape, ri_ref.dtype)

    @pl.when(pl.program_id(0) < n_real)
    def _():
        _merge_body(ga_ref, yb_ref, gta_ref, gtb_ref, x_ref, wa_ref, wb_ref, wo_ref, nf_ref, wr_ref, br_ref,
                    x1_ref, u2_ref, rf_ref, ri_ref, carry_ref)

    cnt_ref[...] = carry_ref[...]


def _merge_body(ga_ref, yb_ref, gta_ref, gtb_ref, x_ref, wa_ref, wb_ref, wo_ref, nf_ref, wr_ref, br_ref,
                x1_ref, u2_ref, rf_ref, ri_ref, carry_ref):
    br_a = _dot(ga_ref[...], wa_ref[...])
    br_b = _dot(yb_ref[...], wb_ref[...])
    merged = _sigmoid(gta_ref[...].astype(F32)) * br_a + _sigmoid(gtb_ref[...].astype(F32)) * br_b
    x1 = x_ref[...] + _dot(merged.astype(BF16), wo_ref[...])
    x1_ref[...] = x1
    ms = jnp.mean(x1 * x1, axis=-1, keepdims=True)
    u2f = x1 * lax.rsqrt(ms + EPS) * nf_ref[...]
    u2_ref[...] = u2f.reshape(u2_ref.shape)
    u2 = u2f.astype(BF16)
    logits = _dot(u2, wr_ref[...]) + br_ref[...]

    tm = logits.shape[0]
    lane = lax.broadcasted_iota(jnp.int32, (tm, LANES), 1)
    lane_f = lane.astype(F32)
    v = jnp.where(lane < N_EXPERTS, logits, -jnp.inf)
    sel = jnp.zeros((tm, LANES), F32)
    hots, vals, idxs = [], [], []
    for _ in range(TOP_K):
        m = jnp.max(v, axis=1, keepdims=True)
        idx = jnp.min(jnp.where(v == m, lane_f, float(LANES)), axis=1, keepdims=True)
        hot = lane_f == idx
        hots.append(hot)
        vals.append(m)
        idxs.append(idx)
        v = jnp.where(hot, -jnp.inf, v)
        sel = sel + hot.astype(F32)
    exps = [jnp.exp(m - vals[0]) for m in vals]
    den = exps[0] + exps[1] + exps[2] + exps[3]
    ir = lax.broadcasted_iota(jnp.int32, (tm, tm), 0)
    ic = lax.broadcasted_iota(jnp.int32, (tm, tm), 1)
    before = _dot((ir > ic).astype(BF16), sel.astype(BF16)) + carry_ref[...]
    rf = jnp.zeros((tm, LANES), F32)
    ri = jnp.zeros((tm, LANES), F32)
    for k in range(TOP_K):
        rank = jnp.sum(jnp.where(hots[k], before, 0.0), axis=1, keepdims=True)
        rf = jnp.where(lane == k, exps[k] / den, rf)
        ri = jnp.where(lane == k, rank, jnp.where(lane == TOP_K + k, idxs[k], ri))
    rf_ref[...] = rf
    ri_ref[...] = ri.astype(jnp.int32)
    carry_ref[...] = carry_ref[...] + jnp.sum(sel, axis=0, keepdims=True)


def _merge(ga, yb, projb, x, wa, wb, wo, nf, wr, br, cnt0, tm, t_all, row0, prev):
    t = x.shape[0]
    blk0 = row0 // tm
    n_real = t // tm
    n_fill = 0 if prev is not None else pl.cdiv(t_all - row0 - t, tm)
    const = lambda i: (0, 0)
    rows = lambda i: jnp.minimum(i, n_real - 1)
    in_specs = [
        pl.BlockSpec((tm, D_RNN), lambda i: (rows(i), 0)),
        pl.BlockSpec((tm, D_INNER), lambda i: (rows(i), 0)),
        pl.BlockSpec((tm, 1024), lambda i: (rows(i), COL_GA)),
        pl.BlockSpec((tm, 1024), lambda i: (rows(i), COL_GB)),
        pl.BlockSpec((tm, D_MODEL), lambda i: (rows(i), 0)),
        pl.BlockSpec((D_RNN, D_MODEL), const),
        pl.BlockSpec((D_INNER, D_MODEL), const),
        pl.BlockSpec((D_MODEL, D_MODEL), const),
        pl.BlockSpec((1, D_MODEL), const),
        pl.BlockSpec((D_MODEL, LANES), const),
        pl.BlockSpec((1, LANES), const),
        pl.BlockSpec((1, LANES), const),
    ]
    args = [ga, yb, projb, projb, x, wa, wb, wo, nf, wr, br, cnt0]
    aliases = {}
    if prev is not None:
        in_specs += [pl.BlockSpec(memory_space=pl.ANY)] * 4
        aliases = {len(args) + k: k for k in range(4)}
        args += list(prev)
    return pl.pallas_call(
        functools.partial(_merge_kernel, n_real=n_real),
        grid=(n_real + n_fill,),
        in_specs=in_specs,
        out_specs=[
            pl.BlockSpec((tm, D_MODEL), lambda i: (blk0 + i, 0)),
            pl.BlockSpec((tm, SUBLANES, LANES), lambda i: (blk0 + i, 0, 0)),
            pl.BlockSpec((tm, LANES), lambda i: (blk0 + i, 0)),
            pl.BlockSpec((tm, LANES), lambda i: (blk0 + i, 0)),
            pl.BlockSpec((1, LANES), const),
        ],
        out_shape=[
            jax.ShapeDtypeStruct((t_all, D_MODEL), F32),
            jax.ShapeDtypeStruct((t_all, SUBLANES, LANES), F32),
            jax.ShapeDtypeStruct((t_all, LANES), F32),
            jax.ShapeDtypeStruct((t_all, LANES), jnp.int32),
            jax.ShapeDtypeStruct((1, LANES), F32),
        ],
        scratch_shapes=[pltpu.VMEM((1, LANES), F32)],
        input_output_aliases=aliases,
        compiler_params=_cparams(("arbitrary",)),
        name="merge_sample" if prev is not None else "merge_prompt",
    )(*args)


def _dispatch_kernel(dest_ref, tail_ref, has_ref, nu_ref, u2_ref, xr_hbm, stage, zero_ref, sem, zsem,
                     *, tm, n_blocks):
    i = pl.program_id(0)
    n = pl.num_programs(0)
    slot = i % 2

    def zero_copy(row0):
        return pltpu.make_async_copy(zero_ref, xr_hbm.at[pl.ds(row0, MOE_BLOCK)], zsem)

    @pl.when(i == 0)
    def _():
        zero_ref[...] = jnp.zeros(zero_ref.shape, F32)
        for e in range(N_EXPERTS):
            @pl.when(has_ref[e] == 1)
            def _():
                zero_copy(tail_ref[e]).start()

        def fill(j, c):
            zero_copy(j * MOE_BLOCK).start()
            return c
        lax.fori_loop(nu_ref[0], n_blocks, fill, 0)
        for e in range(N_EXPERTS):
            @pl.when(has_ref[e] == 1)
            def _():
                zero_copy(0).wait()

        def fill_wait(j, c):
            zero_copy(0).wait()
            return c
        lax.fori_loop(nu_ref[0], n_blocks, fill_wait, 0)

    def row_copy(tok, row, s):
        return pltpu.make_async_copy(stage.at[s, tok], xr_hbm.at[row], sem.at[s])

    stage[slot] = u2_ref[...]

    def issue(g, c):
        rows = [dest_ref[g * ISSUE_GROUP + u] for u in range(ISSUE_GROUP)]
        for u in range(ISSUE_GROUP):
            row_copy(g * (ISSUE_GROUP // TOP_K) + u // TOP_K, rows[u], slot).start(priority=u % 2)
        return c
    lax.fori_loop(0, tm * TOP_K // ISSUE_GROUP, issue, 0)

    def drain(s):
        def body(t, c):
            for k in range(TOP_K):
                row_copy(0, 0, s).wait()
            return c
        lax.fori_loop(0, tm, body, 0, unroll=8)

    @pl.when(i > 0)
    def _():
        drain(1 - slot)

    @pl.when(i == n - 1)
    def _():
        drain(slot)


def _dispatch(dest_flat, tail, has, n_used, u2t, n_blocks, tm):
    t_all = u2t.shape[0]
    smem = functools.partial(pl.BlockSpec, memory_space=pltpu.SMEM)
    return pl.pallas_call(
        functools.partial(_dispatch_kernel, tm=tm, n_blocks=n_blocks),
        grid=(t_all // tm,),
        in_specs=[
            smem((tm * TOP_K,), lambda i: (i,)),
            smem((N_EXPERTS,), lambda i: (0,)),
            smem((N_EXPERTS,), lambda i: (0,)),
            smem((1,), lambda i: (0,)),
            pl.BlockSpec((tm, SUBLANES, LANES), lambda i: (i, 0, 0)),
        ],
        out_specs=pl.BlockSpec(memory_space=pl.ANY),
        out_shape=jax.ShapeDtypeStruct((n_blocks * MOE_BLOCK, SUBLANES, LANES), F32),
        scratch_shapes=[pltpu.VMEM((2, tm, SUBLANES, LANES), F32), pltpu.VMEM((MOE_BLOCK, SUBLANES, LANES), F32),
                        pltpu.SemaphoreType.DMA((2,)), pltpu.SemaphoreType.DMA],
        compiler_params=_cparams(("arbitrary",)),
        name="dispatch",
    )(dest_flat, tail, has, n_used, u2t)


def _expert_kernel(be_ref, first_ref, nu_ref, vr_ref, x_ref, wgu_ref, bgu_ref, wd_ref, bd_ref, o_ref, wgu_b, wd_b):
    i = pl.program_id(0)

    @pl.when(first_ref[i] == 1)
    def _():
        wgu_b[...] = wgu_ref[0].astype(BF16)
        wd_b[...] = wd_ref[0].astype(BF16)

    for h in range(MOE_BLOCK // MOE_SUB):
        rows = slice(MOE_SUB * h, MOE_SUB * (h + 1))

        @pl.when(vr_ref[i] > MOE_SUB * h)
        def _():
            x = x_ref[rows].reshape(MOE_SUB, D_MODEL).astype(BF16)
            gu = _dot(x, wgu_b[...]) + bgu_ref[0]
            gate = jnp.minimum(gu[:, :D_EXPERT], SWIGLU_LIMIT)
            up = jnp.clip(gu[:, D_EXPERT:], -SWIGLU_LIMIT, SWIGLU_LIMIT)
            act = (up + 1.0) * (gate * _sigmoid(SWIGLU_ALPHA * gate))
            y = _dot(act.astype(BF16), wd_b[...]) + bd_ref[0]
            o_ref[rows] = y.reshape(MOE_SUB, SUBLANES, LANES)

        @pl.when(vr_ref[i] <= MOE_SUB * h)
        def _():
            o_ref[rows] = jnp.zeros((MOE_SUB, SUBLANES, LANES), F32)


def _experts(block_e, first, n_used, vrows, xr, wgu, bgu, wd, bd):
    n_rows = xr.shape[0]
    n_blocks = n_rows // MOE_BLOCK
    row_block = (MOE_BLOCK, SUBLANES, LANES)
    return pl.pallas_call(
        _expert_kernel,
        grid_spec=pltpu.PrefetchScalarGridSpec(
            num_scalar_prefetch=4,
            grid=(n_blocks,),
            in_specs=[
                pl.BlockSpec(row_block, lambda i, be, fi, nu, vr: (jnp.minimum(i, nu[0] - 1), 0, 0)),
                pl.BlockSpec((1, D_MODEL, 2 * D_EXPERT), lambda i, be, fi, nu, vr: (be[i], 0, 0)),
                pl.BlockSpec((1, 1, 2 * D_EXPERT), lambda i, be, fi, nu, vr: (be[i], 0, 0)),
                pl.BlockSpec((1, D_EXPERT, D_MODEL), lambda i, be, fi, nu, vr: (be[i], 0, 0)),
                pl.BlockSpec((1, 1, D_MODEL), lambda i, be, fi, nu, vr: (be[i], 0, 0)),
            ],
            out_specs=pl.BlockSpec(row_block, lambda i, be, fi, nu, vr: (i, 0, 0)),
            scratch_shapes=[pltpu.VMEM((D_MODEL, 2 * D_EXPERT), BF16), pltpu.VMEM((D_EXPERT, D_MODEL), BF16)],
        ),
        out_shape=jax.ShapeDtypeStruct((n_rows, SUBLANES, LANES), F32),
        compiler_params=pltpu.CompilerParams(dimension_semantics=("arbitrary",),
                                             vmem_limit_bytes=56 * 1024 * 1024),
        name="experts",
    )(block_e, first, n_used, vrows, xr, wgu, bgu, wd, bd)


def _final_kernel(dcur_ref, dnxt_ref, x1_ref, g_ref, nf_ref, yr_hbm, op_ref, os_ref, buf, sem,
                  *, n_prompt_blocks, tm):
    i = pl.program_id(0)
    n = pl.num_programs(0)
    slot = i % 2

    def row_copy(row, s, k, t):
        dst = buf.at[s, k, pl.ds(pl.multiple_of(t * SUBLANES, SUBLANES), SUBLANES)]
        return pltpu.make_async_copy(yr_hbm.at[row], dst, sem.at[s])

    def issue(dest_ref, s):
        def body(g, c):
            rows = [dest_ref[g * ISSUE_GROUP + u] for u in range(ISSUE_GROUP)]
            for u in range(ISSUE_GROUP):
                t = g * (ISSUE_GROUP // TOP_K) + u // TOP_K
                row_copy(rows[u], s, u % TOP_K, t).start(priority=u % 2)
            return c
        lax.fori_loop(0, tm * TOP_K // ISSUE_GROUP, body, 0)

    @pl.when(i == 0)
    def _():
        issue(dcur_ref, 0)

    @pl.when(i + 1 < n)
    def _():
        issue(dnxt_ref, 1 - slot)

    def drain(t, c):
        for k in range(TOP_K):
            row_copy(0, slot, k, t).wait()
        return c
    lax.fori_loop(0, tm, drain, 0, unroll=8)

    g = g_ref[...]
    parts = []
    for c in range(D_MODEL // LANES):
        moe = buf[slot, 0, pl.ds(c, tm, stride=SUBLANES), :] * g[:, 0:1]
        for k in range(1, TOP_K):
            moe = moe + buf[slot, k, pl.ds(c, tm, stride=SUBLANES), :] * g[:, k:k + 1]
        parts.append(x1_ref[:, LANES * c:LANES * (c + 1)] + moe)
    x2 = jnp.concatenate(parts, axis=1)
    ms = jnp.mean(x2 * x2, axis=-1, keepdims=True)
    y = x2 * lax.rsqrt(ms + EPS) * nf_ref[...]

    @pl.when(i < n_prompt_blocks)
    def _():
        op_ref[...] = y

    @pl.when(i >= n_prompt_blocks)
    def _():
        os_ref[...] = y


def _final(dest_flat, x1, yr, gates, nf, t_prompt, tm):
    t_all = x1.shape[0]
    npb = t_prompt // tm
    nsteps = t_all // tm
    return pl.pallas_call(
        functools.partial(_final_kernel, n_prompt_blocks=npb, tm=tm),
        grid=(nsteps,),
        in_specs=[
            pl.BlockSpec((tm * TOP_K,), lambda i: (i,), memory_space=pltpu.SMEM),
            pl.BlockSpec((tm * TOP_K,), lambda i: (jnp.minimum(i + 1, nsteps - 1),), memory_space=pltpu.SMEM),
            pl.BlockSpec((tm, D_MODEL), lambda i: (i, 0)),
            pl.BlockSpec((tm, LANES), lambda i: (i, 0)),
            pl.BlockSpec((1, D_MODEL), lambda i: (0, 0)),
            pl.BlockSpec(memory_space=pl.ANY),
        ],
        out_specs=[
            pl.BlockSpec((tm, D_MODEL), lambda i: (jnp.minimum(i, npb - 1), 0)),
            pl.BlockSpec((tm, D_MODEL), lambda i: (jnp.maximum(i - npb, 0), 0)),
        ],
        out_shape=[
            jax.ShapeDtypeStruct((t_prompt, D_MODEL), F32),
            jax.ShapeDtypeStruct((t_all - t_prompt, D_MODEL), F32),
        ],
        scratch_shapes=[pltpu.VMEM((2, TOP_K, tm * SUBLANES, LANES), F32), pltpu.SemaphoreType.DMA((2,))],
        compiler_params=_cparams(("arbitrary",)),
        name="final",
    )(dest_flat, dest_flat, x1, gates, nf, yr)


def _block_diag4(w):
    w4 = w.reshape(4, 4, RNN_BLOCK, RNN_BLOCK)
    eye = jnp.eye(4, dtype=w.dtype)
    return jnp.einsum('ghij,hk->ghikj', w4, eye).reshape(4, 256, 256)


def _route_tables(route_i, cnt, t_all):
    counts = cnt[0, :N_EXPERTS].astype(jnp.int32)
    padded = (counts + MOE_BLOCK - 1) // MOE_BLOCK * MOE_BLOCK
    pend = jnp.cumsum(padded)
    pstart = pend - padded
    dense = route_i[:, 0:2 * TOP_K].reshape(-1, LANES)
    start = jnp.take(pstart, jnp.clip(dense, 0, N_EXPERTS - 1))
    dest = (dense + jnp.roll(start, -TOP_K, axis=1)).reshape(t_all, 2 * TOP_K)[:, 0:TOP_K]
    n_blocks = _moe_blocks(t_all)
    n_used = (pend[-1] // MOE_BLOCK).astype(jnp.int32)
    blk = jnp.minimum(jnp.arange(n_blocks, dtype=jnp.int32), n_used - 1) * MOE_BLOCK
    block_e = jnp.minimum(jnp.sum((pend[None, :] <= blk[:, None]).astype(jnp.int32), axis=1), N_EXPERTS - 1)
    first = jnp.concatenate([jnp.ones((1,), jnp.int32), (block_e[1:] != block_e[:-1]).astype(jnp.int32)])
    tail = jnp.maximum(pend - MOE_BLOCK, 0)
    has = (counts > 0).astype(jnp.int32)
    blk_all = jnp.arange(n_blocks, dtype=jnp.int32)
    real_end = jnp.take(pstart + counts, block_e)
    vrows = jnp.where(blk_all < n_used, jnp.clip(real_end - blk_all * MOE_BLOCK, 0, MOE_BLOCK), 0)
    return dest.reshape(-1), tail, has, block_e, first, n_used.reshape(1), vrows


def _moe_blocks(t_all):
    return pl.cdiv(t_all * TOP_K, MOE_BLOCK) + N_EXPERTS


def kernel(x_prompt, x_sample, state_lru_conv, state_lru_h, state_ssd_conv, state_ssd, norm_mix, w_in, conv_lru_w, conv_lru_b, w_rg, b_rg, w_ig, b_ig, lam, w_proj_a, conv_ssd_w, conv_ssd_b, dt_bias, a_log, d_skip, ssd_norm, w_proj_b, w_out, norm_ffn, w_router, b_router, w_gate_up, b_gate_up, w_down, b_down, norm_final):
    nbp, seq, _ = x_prompt.shape
    nbs = x_sample.shape[0]
    t_p = nbp * seq
    t_all = t_p + nbs
    l = 0

    wi = w_in[l]
    s = (0, 1024, 2048, 4096, 7168, 7200, 8224, 9248)
    w_main = jnp.concatenate([wi[:, s[3]:s[4]], wi[:, s[0]:s[1]], wi[:, s[2]:s[3]], wi[:, s[1]:s[2]],
                              wi[:, s[5]:s[6]], wi[:, s[6]:s[7]]], axis=1).astype(BF16)
    w_dt = jnp.pad(wi[:, s[4]:s[5]], ((0, 0), (0, LANES - SSD_HEADS))).astype(BF16)
    g_mix = norm_mix[l].reshape(1, D_MODEL)
    wg = jnp.concatenate([_block_diag4(w_rg[l]), _block_diag4(w_ig[l])], axis=2).astype(BF16)
    bg = jnp.concatenate([b_rg[l].reshape(4, 1, 256), b_ig[l].reshape(4, 1, 256)], axis=2)
    lam_r = lam[l].reshape(1, D_RNN)
    lcw, lcb = conv_lru_w[l], conv_lru_b[l].reshape(1, D_RNN)
    scw, scb = conv_ssd_w[l], conv_ssd_b[l].reshape(1, SSD_CONV_DIM)
    dtb = jnp.pad(dt_bias[l], (0, LANES - SSD_HEADS)).reshape(1, LANES)
    a_row = jnp.pad(-jnp.exp(a_log[l]), (0, LANES - SSD_HEADS)).reshape(1, LANES)
    dsk = jnp.repeat(d_skip[l], SSD_HEAD_DIM).reshape(1, D_INNER)
    nrm = ssd_norm[l].reshape(1, D_INNER)
    e_mat = (jnp.arange(LANES)[:, None] == (jnp.arange(D_INNER) // SSD_HEAD_DIM)[None, :]).astype(F32)
    wa, wb, wo = w_proj_a[l].astype(BF16), w_proj_b[l].astype(BF16), w_out[l].astype(BF16)
    nf = norm_ffn[l].reshape(1, D_MODEL)
    wr = jnp.pad(w_router[l], ((0, 0), (0, LANES - N_EXPERTS))).astype(BF16)
    br = jnp.pad(b_router[l], (0, LANES - N_EXPERTS)).reshape(1, LANES)
    bgu =b_gate_up[l].reshape(N_EXPERTS, 1, 2 * D_EXPERT)
    bd = b_down[l].reshape(N_EXPERTS, 1, D_MODEL)

    xp = x_prompt.reshape(t_p, D_MODEL)
    proj_p, projb_p, dt_p = _inproj(xp, g_mix, w_main, w_dt, tm=2048)
    ga_p, p_lc, p_lh = _lru_prompt(proj_p, projb_p, lcw, lcb, wg, bg, lam_r, nbp, seq, tl=512)
    yb_p, p_sc, p_ss = _ssd_prompt(proj_p, projb_p, dt_p, scw, scb, dtb, a_row, dsk, nrm, e_mat, nbp, seq)

    xs_in = x_sample.reshape(nbs, D_MODEL)
    proj_s, projb_s, dt_s = _inproj(xs_in, g_mix, w_main, w_dt, tm=nbs)
    (ga_s, s_lc, s_lh, s_sc, xdt_t, da, b_s, c_s, xs_s) = _sample_pre(
        proj_s, projb_s, dt_s,state_lru_conv[l].reshape(nbs, 3 * D_RNN), state_lru_h[l],
        state_ssd_conv[l].reshape(nbs, 3 * SSD_CONV_DIM), lcw, lcb, wg, bg, lam_r, scw, scb, dtb, a_row, e_mat)
    s_ss, y_t = _sample_state(da[:, :SSD_HEADS].reshape(-1), state_ssd[l].reshape(nbs, D_INNER, D_STATE),
                              xdt_t, b_s.reshape(nbs, 1, -1), c_s.reshape(nbs, 1, -1))
    yb_s = _sample_post(y_t, xs_s, projb_s, dsk, nrm)

    cnt0 = jnp.zeros((1, LANES), F32)
    *prev, cnt_p = _merge(ga_p, yb_p, projb_p, xp, wa, wb, wo, nf, wr, br, cnt0, tm=512, t_all=t_all, row0=0,
                          prev=None)
    x1, u2, route_f, route_i, cnt = _merge(ga_s, yb_s, projb_s, xs_in, wa, wb, wo, nf, wr, br, cnt_p, tm=nbs,
                                           t_all=t_all, row0=t_p, prev=prev)

    dest, tail, has, block_e, first, n_used, vrows = _route_tables(route_i, cnt, t_all)
    xr = _dispatch(dest, tail, has, n_used, u2, _moe_blocks(t_all), tm=nbs)
    yr = _experts(block_e, first, n_used, vrows, xr, w_gate_up[l], bgu, w_down[l], bd)
    y_p, y_s = _final(dest, x1, yr, route_f, norm_final.reshape(1, D_MODEL), t_p, tm=nbs)

    return (y_p.reshape(nbp, seq, D_MODEL), y_s.reshape(nbs, 1, D_MODEL),
            p_lc[None], p_lh.reshape(1, nbp, D_RNN), p_sc[None],
            p_ss.reshape(1, nbp, SSD_HEADS, SSD_HEAD_DIM, D_STATE),
            s_lc.reshape(1, nbs, CONV_W - 1, D_RNN), s_lh[None],
            s_sc.reshape(1, nbs, CONV_W - 1, SSD_CONV_DIM),
            s_ss.reshape(1, nbs, SSD_HEADS, SSD_HEAD_DIM, D_STATE))
```

```python
import functools

import jax
import jax.numpy as jnp
from jax import lax
from jax.experimental import pallas as pl
from jax.experimental.pallas import tpu as pltpu

F32 = jnp.float32
BF16 = jnp.bfloat16

D_MODEL = 1024
D_RNN = 1024
RNN_HEADS = 16
RNN_BLOCK = 64
CONV_W = 4
RG_C = 8.0
D_INNER = 2048
SSD_HEAD_DIM = 64
SSD_HEADS = 32
SSD_GROUPS = 4
D_STATE = 128
SSD_CONV_DIM = D_INNER + 2 * SSD_GROUPS * D_STATE
CHUNK = 128
N_EXPERTS = 32
TOP_K = 4
D_EXPERT = 1024
SWIGLU_LIMIT = 7.0
SWIGLU_ALPHA = 1.702
EPS = 1e-6

LANES = 128
SUBLANES = 8
N_MAIN = 9 * 1024
NJ_F32 = 4
COL_XBC, COL_LRU = 0, 3
COL_Z, COL_GATE, COL_GA, COL_GB = 0, 2, 3, 4
MOE_BLOCK = 512
MOE_SUB = 512
SAMPLE_STATE_ROWS = 2
ISSUE_GROUP = 32
VMEM_LIMIT = 48 * 1024 * 1024


def _cparams(sem):
    return pltpu.CompilerParams(dimension_semantics=sem, vmem_limit_bytes=VMEM_LIMIT)


def _sigmoid(x):
    return jax.nn.sigmoid(x)


def _softplus(x):
    return jnp.maximum(x, 0.0) + jnp.log1p(jnp.exp(-jnp.abs(x)))


def _gelu_tanh(x):
    return 0.5 * x * (1.0 + jnp.tanh(0.7978845608028654 * (x + 0.044715 * (x * x * x))))


def _dot(a, b):
    return jnp.dot(a, b, preferred_element_type=F32)


def _dot_exact(a, b):
    return jnp.dot(a, b, preferred_element_type=F32, precision=lax.Precision.HIGHEST)


def _inproj_kernel(x_ref, g_ref, w_ref, wdt_ref, of_ref, ob_ref, dt_ref, u_ref):
    j = pl.program_id(1)

    @pl.when(j == 0)
    def _():
        x = x_ref[...]
        ms = jnp.mean(x * x, axis=-1, keepdims=True)
        u = (x * lax.rsqrt(ms + EPS) * g_ref[...]).astype(BF16)
        u_ref[...] = u
        dt_ref[...] = _dot(u, wdt_ref[...])

    @pl.when(j < NJ_F32)
    def _():
        of_ref[...] = _dot(u_ref[...], w_ref[...])

    @pl.when(j >= NJ_F32)
    def _():
        ob_ref[...] = _dot(u_ref[...], w_ref[...]).astype(BF16)


def _inproj(x, g, w_main, w_dt, tm):
    t = x.shape[0]
    nj = N_MAIN // 1024
    return pl.pallas_call(
        _inproj_kernel,
        grid=(t // tm, nj),
        in_specs=[
            pl.BlockSpec((tm, D_MODEL), lambda i, j: (i, 0), pipeline_mode=pl.Buffered(1)),
            pl.BlockSpec((1, D_MODEL), lambda i, j: (0, 0)),
            pl.BlockSpec((D_MODEL, 1024), lambda i, j: (0, j)),
            pl.BlockSpec((D_MODEL, LANES), lambda i, j: (0, 0)),
        ],
        out_specs=[
            pl.BlockSpec((tm, 1024), lambda i, j: (i, jnp.minimum(j, NJ_F32 - 1))),
            pl.BlockSpec((tm, 1024), lambda i, j: (i, jnp.maximum(j - NJ_F32, 0))),
            pl.BlockSpec((tm, LANES), lambda i, j: (i, 0)),
        ],
        out_shape=[jax.ShapeDtypeStruct((t, NJ_F32 * 1024), F32),
                   jax.ShapeDtypeStruct((t, (nj - NJ_F32) * 1024), BF16),
                   jax.ShapeDtypeStruct((t, LANES), F32)],
        scratch_shapes=[pltpu.VMEM((tm, D_MODEL), BF16)],
        compiler_params=pltpu.CompilerParams(dimension_semantics=("parallel", "arbitrary"),
                                             vmem_limit_bytes=56 * 1024 * 1024),
        name="inproj",
    )(x, g, w_main, w_dt)


def _lru_gates(xc, wg_ref, bg_ref, lam):
    xcb = xc.astype(BF16)
    sp = _softplus(-lam)
    a_parts, b_parts = [], []
    for g in range(4):
        sl = slice(256 * g, 256 * (g + 1))
        pre = _dot(xcb[:, sl], wg_ref[g]) + bg_ref[g]
        r = _sigmoid(pre[:, :256])
        i = _sigmoid(pre[:, 256:])
        log_a = (-RG_C * r) * sp[:, sl]
        a_parts.append(jnp.exp(log_a))
        th = jnp.tanh(log_a)
        mult = jnp.sqrt(-2.0 * th / (1.0 - th))
        b_parts.append(mult * (i * xc[:, sl]))
    return jnp.concatenate(a_parts, axis=1), jnp.concatenate(b_parts, axis=1)


def _lru_kernel(x_ref, gate_ref, cw_ref, cb_ref, wg_ref, bg_ref, lam_ref,
                out_ref, conv_ref, h_ref, xx_ref, a_ref, b_ref, hc_ref, *, tl):
    @pl.when(pl.program_id(1) == 0)
    def _():
        xx_ref[0:SUBLANES, :] = jnp.zeros((SUBLANES, D_RNN), F32)
        hc_ref[...] = jnp.zeros((SUBLANES, D_RNN), F32)

    x = x_ref[...]
    xx_ref[SUBLANES:SUBLANES + tl, :] = x
    cw = cw_ref[...]
    xc = (cw[3:4] * x + cw[2:3] * xx_ref[7:7 + tl, :] + cw[1:2] * xx_ref[6:6 + tl, :]
          + cw[0:1] * xx_ref[5:5 + tl, :] + cb_ref[...])
    xx_ref[0:SUBLANES, :] = xx_ref[tl:tl + SUBLANES, :]
    conv_ref[0] = x_ref[tl - 3:tl, :]

    a, b = _lru_gates(xc, wg_ref, bg_ref, lam_ref[...])
    nt = tl // SUBLANES
    a = a.reshape(nt, SUBLANES, D_RNN)
    b = b.reshape(nt, SUBLANES, D_RNN)
    rows = lax.broadcasted_iota(jnp.int32, (nt, SUBLANES, D_RNN), 1)
    for d in (1, 2, 4):
        m = rows >= d
        b = jnp.where(m, a * pltpu.roll(b, d, 1) + b, b)
        a = jnp.where(m, a * pltpu.roll(a, d, 1), a)
    a_ref[...] = a
    b_ref[...] = b

    def carry(k, hprev):
        h = a_ref[k] * hprev + b_ref[k]
        b_ref[k] = h
        return jnp.broadcast_to(h[SUBLANES - 1:SUBLANES, :], (SUBLANES, D_RNN))

    hlast = lax.fori_loop(0, nt, carry, hc_ref[...])
    hc_ref[...] = hlast
    h_ref[0] = hlast[0:1, :]
    h_all = b_ref[...].reshape(tl, D_RNN)
    out_ref[...] = (_gelu_tanh(gate_ref[...].astype(F32)) * h_all).astype(BF16)


def _lru_prompt(proj, projb, cw, cb, wg, bg, lam, nb, seq, tl):
    nl = seq // tl
    return pl.pallas_call(
        functools.partial(_lru_kernel, tl=tl),
        grid=(nb, nl),
        in_specs=[
            pl.BlockSpec((tl, 1024), lambda b, l: (b * nl + l, COL_LRU)),
            pl.BlockSpec((tl, 1024), lambda b, l: (b * nl + l, COL_GATE)),
            pl.BlockSpec((CONV_W, D_RNN), lambda b, l: (0, 0)),
            pl.BlockSpec((1, D_RNN), lambda b, l: (0, 0)),
            pl.BlockSpec((4, 256, 512), lambda b, l: (0, 0, 0)),
            pl.BlockSpec((4, 1, 512), lambda b, l: (0, 0, 0)),
            pl.BlockSpec((1, D_RNN), lambda b, l: (0, 0)),
        ],
        out_specs=[
            pl.BlockSpec((tl, D_RNN), lambda b, l: (b * nl + l, 0)),
            pl.BlockSpec((1, CONV_W - 1, D_RNN), lambda b, l: (b, 0, 0)),
            pl.BlockSpec((1, 1, D_RNN), lambda b, l: (b, 0, 0)),
        ],
        out_shape=[
            jax.ShapeDtypeStruct((nb * seq, D_RNN), BF16),
            jax.ShapeDtypeStruct((nb, CONV_W - 1, D_RNN), F32),
            jax.ShapeDtypeStruct((nb, 1, D_RNN), F32),
        ],
        scratch_shapes=[
            pltpu.VMEM((tl + SUBLANES, D_RNN), F32),
            pltpu.VMEM((tl // SUBLANES, SUBLANES, D_RNN), F32),
            pltpu.VMEM((tl // SUBLANES, SUBLANES, D_RNN), F32),
            pltpu.VMEM((SUBLANES, D_RNN), F32),
        ],
        compiler_params=_cparams(("parallel", "arbitrary")),
        name="lru_prompt",
    )(proj, projb, cw, cb, wg, bg, lam)


def _group_norm_gate(y, z, nrm):
    y = y * (z * _sigmoid(z))
    gw = D_INNER // SSD_GROUPS
    parts = []
    for g in range(SSD_GROUPS):
        yg = y[:, gw * g:gw * (g + 1)]
        ms = jnp.mean(yg * yg, axis=-1, keepdims=True)
        parts.append(yg * lax.rsqrt(ms + EPS))
    return jnp.concatenate(parts, axis=1) * nrm


def _ssd_kernel(xbc_ref, z_ref, dt_ref, cw_ref, cb_ref, dtb_ref, a_ref, dsk_ref, nrm_ref, e_ref,
                y_ref, conv_ref, st_ref, xx_ref, stt_ref, yacc_ref):
    q = CHUNK

    @pl.when(pl.program_id(1) == 0)
    def _():
        xx_ref[0:SUBLANES, :] = jnp.zeros((SUBLANES, SSD_CONV_DIM), F32)
        stt_ref[...] = jnp.zeros((D_STATE, D_INNER), F32)

    xbc = xbc_ref[...]
    xx_ref[SUBLANES:SUBLANES + q, :] = xbc
    cw = cw_ref[...]
    conv = (cw[3:4] * xbc + cw[2:3] * xx_ref[7:7 + q, :] + cw[1:2] * xx_ref[6:6 + q, :]
            + cw[0:1] * xx_ref[5:5 + q, :] + cb_ref[...])
    xx_ref[0:SUBLANES, :] = xx_ref[q:q + SUBLANES, :]
    conv_ref[0] = xbc_ref[q - 3:q, :]
    act = conv * _sigmoid(conv)
    xs = act[:, :D_INNER]
    bm = act[:, D_INNER:D_INNER + SSD_GROUPS * D_STATE]
    cm = act[:, D_INNER + SSD_GROUPS * D_STATE:]

    dt = _softplus(dt_ref[...] + dtb_ref[...])
    dta = dt * a_ref[...]
    iq = lax.broadcasted_iota(jnp.int32, (q, q), 0)
    ik = lax.broadcasted_iota(jnp.int32, (q, q), 1)
    causal = iq >= ik
    a_cum = _dot_exact(causal.astype(F32), dta)
    a_cum_t = a_cum.T
    dt_t = dt.T
    e_cum = jnp.exp(a_cum)
    w_t = jnp.exp(a_cum_t[:, q - 1:q] - a_cum_t) * dt_t
    a_last = jnp.broadcast_to(a_cum[q - 1:q, :], (SUBLANES, LANES))
    da_e = jnp.exp(_dot_exact(a_last, e_ref[...])[0:1, :])
    lo = ik < SSD_HEAD_DIM

    for g in range(SSD_GROUPS):
        bg = bm[:, D_STATE * g:D_STATE * (g + 1)]
        cg = cm[:, D_STATE * g:D_STATE * (g + 1)]
        cb = lax.dot_general(cg.astype(BF16), bg.astype(BF16), (((1,), (1,)), ((), ())),
                             preferred_element_type=F32)
        bg_t = bg.T
        for pp in range(4):
            j = 4 * g + pp
            ms, cs, bw = [], [], []
            for s in range(2):
                h = 2 * j + s
                col = a_cum[:, h:h + 1]
                row = a_cum_t[h:h + 1, :]
                dec = jnp.exp(jnp.where(causal, col - row, -jnp.inf))
                ms.append(cb * dec * dt_t[h:h + 1, :])
                cs.append(cg * e_cum[:, h:h + 1])
                bw.append(bg_t * w_t[h:h + 1, :])
            sl = slice(LANES * j, LANES * (j + 1))
            xp = xs[:, sl]
            rhs_x = jnp.concatenate([jnp.where(lo, xp, 0.0), jnp.where(lo, 0.0, xp)], axis=0).astype(BF16)
            stp = stt_ref[:, sl]
            rhs_s = jnp.concatenate([jnp.where(lo, stp, 0.0), jnp.where(lo, 0.0, stp)], axis=0).astype(BF16)
            l_m = jnp.concatenate(ms, axis=1).astype(BF16)
            l_c = jnp.concatenate(cs, axis=1).astype(BF16)
            l_b = jnp.concatenate(bw, axis=1).astype(BF16)
            yacc_ref[:, sl] = _dot(l_m, rhs_x) + _dot(l_c, rhs_s)
            stt_ref[:, sl] = stp * da_e[:, sl] + _dot(l_b, rhs_x)

    y = yacc_ref[...] + dsk_ref[...] * xs
    y_ref[...] = _group_norm_gate(y, z_ref[...].astype(F32), nrm_ref[...]).astype(BF16)

    @pl.when(pl.program_id(1) == pl.num_programs(1) - 1)
    def _():
        st_ref[0] = stt_ref[...].T


def _ssd_prompt(proj, projb, dt, cw, cb, dtb, a_row, dsk, nrm, e_mat, nb, seq):
    nc = seq // CHUNK
    return pl.pallas_call(
        _ssd_kernel,
        grid=(nb, nc),
        in_specs=[
            pl.BlockSpec((CHUNK, SSD_CONV_DIM), lambda b, c: (b * nc + c, COL_XBC)),
            pl.BlockSpec((CHUNK, D_INNER), lambda b, c: (b * nc + c, COL_Z)),
            pl.BlockSpec((CHUNK, LANES), lambda b, c: (b * nc + c, 0)),
            pl.BlockSpec((CONV_W, SSD_CONV_DIM), lambda b, c: (0, 0)),
            pl.BlockSpec((1, SSD_CONV_DIM), lambda b, c: (0, 0)),
            pl.BlockSpec((1, LANES), lambda b, c: (0, 0)),
            pl.BlockSpec((1, LANES), lambda b, c: (0, 0)),
            pl.BlockSpec((1, D_INNER), lambda b, c: (0, 0)),
            pl.BlockSpec((1, D_INNER), lambda b, c: (0, 0)),
            pl.BlockSpec((LANES, D_INNER), lambda b, c: (0, 0)),
        ],
        out_specs=[
            pl.BlockSpec((CHUNK, D_INNER), lambda b, c: (b * nc + c, 0)),
            pl.BlockSpec((1, CONV_W - 1, SSD_CONV_DIM), lambda b, c: (b, 0, 0)),
            pl.BlockSpec((1, D_INNER, D_STATE), lambda b, c: (b, 0, 0)),
        ],
        out_shape=[
            jax.ShapeDtypeStruct((nb * seq, D_INNER), BF16),
            jax.ShapeDtypeStruct((nb, CONV_W - 1, SSD_CONV_DIM), F32),
            jax.ShapeDtypeStruct((nb, D_INNER, D_STATE), F32),
        ],
        scratch_shapes=[
            pltpu.VMEM((CHUNK + SUBLANES, SSD_CONV_DIM), F32),
            pltpu.VMEM((D_STATE, D_INNER), F32),
            pltpu.VMEM((CHUNK, D_INNER), F32),
        ],
        compiler_params=_cparams(("parallel", "arbitrary")),
        name="ssd_prompt",
    )(proj, projb, dt, cw, cb, dtb, a_row, dsk, nrm, e_mat)


def _sample_pre_kernel(proj_ref, projb_ref, dt_ref, lconv_ref, h0_ref, sconv_ref,
                       lcw_ref, lcb_ref, wg_ref, bg_ref, lam_ref,
                       scw_ref, scb_ref, dtb_ref, a_ref, e_ref,
                       ga_ref, lconv_o, h_o, sconv_o, xdt_t_o, da_o, b_o, c_o, xs_o):
    x = proj_ref[:, SSD_CONV_DIM:SSD_CONV_DIM + D_RNN]
    gate = projb_ref[:, D_INNER:D_INNER + D_RNN].astype(F32)
    cw = lcw_ref[...]
    c0 = lconv_ref[:, 0:1024]
    c1 = lconv_ref[:, 1024:2048]
    c2 = lconv_ref[:, 2048:3072]
    xc = cw[3:4] * x + cw[2:3] * c2 + cw[1:2] * c1 + cw[0:1] * c0 + lcb_ref[...]
    lconv_o[:, 0:1024] = c1
    lconv_o[:, 1024:2048] = c2
    lconv_o[:, 2048:3072] = x
    a, bt = _lru_gates(xc, wg_ref, bg_ref, lam_ref[...])
    h = a * h0_ref[...] + bt
    h_o[...] = h
    ga_ref[...] = (_gelu_tanh(gate) * h).astype(BF16)

    xbc = proj_ref[:, 0:SSD_CONV_DIM]
    sw = scw_ref[...]
    w = SSD_CONV_DIM
    s0 = sconv_ref[:, 0:w]
    s1 = sconv_ref[:, w:2 * w]
    s2 = sconv_ref[:, 2 * w:3 * w]
    conv = sw[3:4] * xbc + sw[2:3] * s2 + sw[1:2] * s1 + sw[0:1] * s0 + scb_ref[...]
    sconv_o[:, 0:w] = s1
    sconv_o[:, w:2 * w] = s2
    sconv_o[:, 2 * w:3 * w] = xbc
    act = conv * _sigmoid(conv)
    xs = act[:, :D_INNER]
    xs_o[...] = xs
    b_o[...] = act[:, D_INNER:D_INNER + SSD_GROUPS * D_STATE]
    c_o[...] = act[:, D_INNER + SSD_GROUPS * D_STATE:]
    dt = _softplus(dt_ref[...] + dtb_ref[...])
    da_o[...] = jnp.exp(dt * a_ref[...])
    dt_e = _dot_exact(dt, e_ref[...])
    xdt_t_o[...] = (xs * dt_e).T


def _sample_pre(proj, projb, dt, lconv, h0, sconv, lcw, lcb, wg, bg, lam, scw, scb, dtb, a_row, e_mat):
    nb = proj.shape[0]
    out_shape = [
        jax.ShapeDtypeStruct((nb, D_RNN), BF16),
        jax.ShapeDtypeStruct((nb, 3 * D_RNN), F32),
        jax.ShapeDtypeStruct((nb, D_RNN), F32),
        jax.ShapeDtypeStruct((nb, 3 * SSD_CONV_DIM), F32),
        jax.ShapeDtypeStruct((D_INNER, nb), F32),
        jax.ShapeDtypeStruct((nb, LANES), F32),
        jax.ShapeDtypeStruct((nb, SSD_GROUPS * D_STATE), F32),
        jax.ShapeDtypeStruct((nb, SSD_GROUPS * D_STATE), F32),
        jax.ShapeDtypeStruct((nb, D_INNER), F32),
    ]
    return pl.pallas_call(
        _sample_pre_kernel,
        out_shape=out_shape,
        compiler_params=pltpu.CompilerParams(vmem_limit_bytes=VMEM_LIMIT),
        name="sample_pre",
    )(proj, projb, dt, lconv, h0, sconv, lcw, lcb, wg, bg, lam, scw, scb, dtb, a_row, e_mat)


def _sample_state_kernel(da_ref, s0_ref, xdt_t_ref, b_ref, c_ref, s1_ref, y_t_ref):
    nb = xdt_t_ref.shape[1]

    @pl.when(pl.program_id(0) == 0)
    def _():
        y_t_ref[...] = jnp.zeros(y_t_ref.shape, F32)

    lane = lax.broadcasted_iota(jnp.int32, (D_INNER, nb), 1)
    for j in range(SAMPLE_STATE_ROWS):
        b = pl.program_id(0) * SAMPLE_STATE_ROWS + j
        sel = lane == b
        xcol = jnp.sum(jnp.where(sel, xdt_t_ref[...], 0.0), axis=1, keepdims=True)
        ycols = []
        for g in range(SSD_GROUPS):
            brow = b_ref[j, :, D_STATE * g:D_STATE * (g + 1)]
            crow = c_ref[j, :, D_STATE * g:D_STATE * (g + 1)]
            for hh in range(SSD_HEADS // SSD_GROUPS):
                h = (SSD_HEADS // SSD_GROUPS) * g + hh
                sl = slice(SSD_HEAD_DIM * h, SSD_HEAD_DIM * (h + 1))
                s1 = s0_ref[j, sl, :] * da_ref[b * SSD_HEADS + h] + xcol[sl, :] * brow
                s1_ref[j, sl, :] = s1
                ycols.append(jnp.sum(s1 * crow, axis=1, keepdims=True))
        ycol = jnp.concatenate(ycols, axis=0)
        y_t_ref[...] = jnp.where(sel, ycol, y_t_ref[...])


def _sample_state(da_flat, s0, xdt_t, bmat, cmat):
    nb = s0.shape[0]
    return pl.pallas_call(
        _sample_state_kernel,
        grid_spec=pltpu.PrefetchScalarGridSpec(
            num_scalar_prefetch=1,
            grid=(nb // SAMPLE_STATE_ROWS,),
            in_specs=[
                pl.BlockSpec((SAMPLE_STATE_ROWS, D_INNER, D_STATE), lambda b, da: (b, 0, 0)),
                pl.BlockSpec((D_INNER, nb), lambda b, da: (0, 0)),
                pl.BlockSpec((SAMPLE_STATE_ROWS, 1, SSD_GROUPS * D_STATE), lambda b, da: (b, 0, 0)),
                pl.BlockSpec((SAMPLE_STATE_ROWS, 1, SSD_GROUPS * D_STATE), lambda b, da: (b, 0, 0)),
            ],
            out_specs=[
                pl.BlockSpec((SAMPLE_STATE_ROWS, D_INNER, D_STATE), lambda b, da: (b, 0, 0)),
                pl.BlockSpec((D_INNER, nb), lambda b, da: (0, 0)),
            ],
        ),
        out_shape=[
            jax.ShapeDtypeStruct((nb, D_INNER, D_STATE), F32),
            jax.ShapeDtypeStruct((D_INNER, nb), F32),
        ],
        compiler_params=_cparams(("arbitrary",)),
        name="sample_state",
    )(da_flat, s0, xdt_t, bmat, cmat)


def _sample_post_kernel(y_t_ref, xs_ref, z_ref, dsk_ref, nrm_ref, y_ref):
    y = y_t_ref[...].T + dsk_ref[...] * xs_ref[...]
    y_ref[...] = _group_norm_gate(y, z_ref[...].astype(F32), nrm_ref[...]).astype(BF16)


def _sample_post(y_t, xs, projb, dsk, nrm):
    nb = xs.shape[0]
    return pl.pallas_call(
        _sample_post_kernel,
        grid=(1,),
        in_specs=[
            pl.BlockSpec((D_INNER, nb), lambda i: (0, 0)),
            pl.BlockSpec((nb, D_INNER), lambda i: (0, 0)),
            pl.BlockSpec((nb, D_INNER), lambda i: (0, COL_Z)),
            pl.BlockSpec((1, D_INNER), lambda i: (0, 0)),
            pl.BlockSpec((1, D_INNER), lambda i: (0, 0)),
        ],
        out_specs=pl.BlockSpec((nb, D_INNER), lambda i: (0, 0)),
        out_shape=jax.ShapeDtypeStruct((nb, D_INNER), BF16),
        compiler_params=_cparams(("arbitrary",)),
        name="sample_post",
    )(y_t, xs, projb, dsk, nrm)


def _merge_kernel(ga_ref, yb_ref, gta_ref, gtb_ref, x_ref, wa_ref, wb_ref, wo_ref, nf_ref, wr_ref, br_ref,
                  cnt0_ref, *refs, n_real):
    x1_ref, u2_ref, rf_ref, ri_ref, cnt_ref, carry_ref = refs[-6:]

    @pl.when(pl.program_id(0) == 0)
    def _():
        carry_ref[...] = cnt0_ref[...]

    @pl.when(pl.program_id(0) >= n_real)
    def _():
        x1_ref[...] = jnp.zeros(x1_ref.shape, x1_ref.dtype)
        u2_ref[...] = jnp.zeros(u2_ref.shape, u2_ref.dtype)
        rf_ref[...] = jnp.zeros(rf_ref.shape, rf_ref.dtype)
        ri_ref[...] = jnp.zeros(ri_ref.shape, ri_ref.dtype)

    @pl.when(pl.program_id(0) < n_real)
    def _():
        _merge_body(ga_ref, yb_ref, gta_ref, gtb_ref, x_ref, wa_ref, wb_ref, wo_ref, nf_ref, wr_ref, br_ref,
                    x1_ref, u2_ref, rf_ref, ri_ref, carry_ref)

    cnt_ref[...] = carry_ref[...]


def _merge_body(ga_ref, yb_ref, gta_ref, gtb_ref, x_ref, wa_ref, wb_ref, wo_ref, nf_ref, wr_ref, br_ref,
                x1_ref, u2_ref, rf_ref, ri_ref, carry_ref):
    br_a = _dot(ga_ref[...], wa_ref[...])
    br_b = _dot(yb_ref[...], wb_ref[...])
    merged = _sigmoid(gta_ref[...].astype(F32)) * br_a + _sigmoid(gtb_ref[...].astype(F32)) * br_b
    x1 = x_ref[...] + _dot(merged.astype(BF16), wo_ref[...])
    x1_ref[...] = x1
    ms = jnp.mean(x1 * x1, axis=-1, keepdims=True)
    u2f = x1 * lax.rsqrt(ms + EPS) * nf_ref[...]
    u2_ref[...] = u2f.reshape(u2_ref.shape)
    u2 = u2f.astype(BF16)
    logits = _dot(u2, wr_ref[...]) + br_ref[...]

    tm = logits.shape[0]
    lane = lax.broadcasted_iota(jnp.int32, (tm, LANES), 1)
    lane_f = lane.astype(F32)
    v = jnp.where(lane < N_EXPERTS, logits, -jnp.inf)
    sel = jnp.zeros((tm, LANES), F32)
    hots, vals, idxs = [], [], []
    for _ in range(TOP_K):
        m = jnp.max(v, axis=1, keepdims=True)
        idx = jnp.min(jnp.where(v == m, lane_f, float(LANES)), axis=1, keepdims=True)
        hot = lane_f == idx
        hots.append(hot)
        vals.append(m)
        idxs.append(idx)
        v = jnp.where(hot, -jnp.inf, v)
        sel = sel + hot.astype(F32)
    exps = [jnp.exp(m - vals[0]) for m in vals]
    den = exps[0] + exps[1] + exps[2] + exps[3]
    ir = lax.broadcasted_iota(jnp.int32, (tm, tm), 0)
    ic = lax.broadcasted_iota(jnp.int32, (tm, tm), 1)
    before = _dot((ir > ic).astype(BF16), sel.astype(BF16)) + carry_ref[...]
    rf = jnp.zeros((tm, LANES), F32)
    ri = jnp.zeros((tm, LANES), F32)
    for k in range(TOP_K):
        rank = jnp.sum(jnp.where(hots[k], before, 0.0), axis=1, keepdims=True)
        rf = jnp.where(lane == k, exps[k] / den, rf)
        ri = jnp.where(lane == k, rank, jnp.where(lane == TOP_K + k, idxs[k], ri))
    rf_ref[...] = rf
    ri_ref[...] = ri.astype(jnp.int32)
    carry_ref[...] = carry_ref[...] + jnp.sum(sel, axis=0, keepdims=True)


def _merge(ga, yb, projb, x, wa, wb, wo, nf, wr, br, cnt0, tm, t_all, row0, prev):
    t = x.shape[0]
    blk0 = row0 // tm
    n_real = t // tm
    n_fill = 0 if prev is not None else pl.cdiv(t_all - row0 - t, tm)
    const = lambda i: (0, 0)
    rows = lambda i: jnp.minimum(i, n_real - 1)
    in_specs = [
        pl.BlockSpec((tm, D_RNN), lambda i: (rows(i), 0)),
        pl.BlockSpec((tm, D_INNER), lambda i: (rows(i), 0)),
        pl.BlockSpec((tm, 1024), lambda i: (rows(i), COL_GA)),
        pl.BlockSpec((tm, 1024), lambda i: (rows(i), COL_GB)),
        pl.BlockSpec((tm, D_MODEL), lambda i: (rows(i), 0)),
        pl.BlockSpec((D_RNN, D_MODEL), const),
        pl.BlockSpec((D_INNER, D_MODEL), const),
        pl.BlockSpec((D_MODEL, D_MODEL), const),
        pl.BlockSpec((1, D_MODEL), const),
        pl.BlockSpec((D_MODEL, LANES), const),
        pl.BlockSpec((1, LANES), const),
        pl.BlockSpec((1, LANES), const),
    ]
    args = [ga, yb, projb, projb, x, wa, wb, wo, nf, wr, br, cnt0]
    aliases = {}
    if prev is not None:
        in_specs += [pl.BlockSpec(memory_space=pl.ANY)] * 4
        aliases = {len(args) + k: k for k in range(4)}
        args += list(prev)
    return pl.pallas_call(
        functools.partial(_merge_kernel, n_real=n_real),
        grid=(n_real + n_fill,),
        in_specs=in_specs,
        out_specs=[
            pl.BlockSpec((tm, D_MODEL), lambda i: (blk0 + i, 0)),
            pl.BlockSpec((tm, SUBLANES, LANES), lambda i: (blk0 + i, 0, 0)),
            pl.BlockSpec((tm, LANES), lambda i: (blk0 + i, 0)),
            pl.BlockSpec((tm, LANES), lambda i: (blk0 + i, 0)),
            pl.BlockSpec((1, LANES), const),
        ],
        out_shape=[
            jax.ShapeDtypeStruct((t_all, D_MODEL), F32),
            jax.ShapeDtypeStruct((t_all, SUBLANES, LANES), F32),
            jax.ShapeDtypeStruct((t_all, LANES), F32),
            jax.ShapeDtypeStruct((t_all, LANES), jnp.int32),
            jax.ShapeDtypeStruct((1, LANES), F32),
        ],
        scratch_shapes=[pltpu.VMEM((1, LANES), F32)],
        input_output_aliases=aliases,
        compiler_params=_cparams(("arbitrary",)),
        name="merge_sample" if prev is not None else "merge_prompt",
    )(*args)


def _dispatch_kernel(dest_ref, tail_ref, has_ref, nu_ref, u2_ref, xr_hbm, stage, zero_ref, sem, zsem,
                     *, tm, n_blocks):
    i = pl.program_id(0)
    n = pl.num_programs(0)
    slot = i % 2

    def zero_copy(row0):
        return pltpu.make_async_copy(zero_ref, xr_hbm.at[pl.ds(row0, MOE_BLOCK)], zsem)

    @pl.when(i == 0)
    def _():
        zero_ref[...] = jnp.zeros(zero_ref.shape, F32)
        for e in range(N_EXPERTS):
            @pl.when(has_ref[e] == 1)
            def _():
                zero_copy(tail_ref[e]).start()

        def fill(j, c):
            zero_copy(j * MOE_BLOCK).start()
            return c
        lax.fori_loop(nu_ref[0], n_blocks, fill, 0)
        for e in range(N_EXPERTS):
            @pl.when(has_ref[e] == 1)
            def _():
                zero_copy(0).wait()

        def fill_wait(j, c):
            zero_copy(0).wait()
            return c
        lax.fori_loop(nu_ref[0], n_blocks, fill_wait, 0)

    def row_copy(tok, row, s):
        return pltpu.make_async_copy(stage.at[s, tok], xr_hbm.at[row], sem.at[s])

    stage[slot] = u2_ref[...]

    def issue(g, c):
        rows = [dest_ref[g * ISSUE_GROUP + u] for u in range(ISSUE_GROUP)]
        for u in range(ISSUE_GROUP):
            row_copy(g * (ISSUE_GROUP // TOP_K) + u // TOP_K, rows[u], slot).start(priority=u % 2)
        return c
    lax.fori_loop(0, tm * TOP_K // ISSUE_GROUP, issue, 0)

    def drain(s):
        def body(t, c):
            for k in range(TOP_K):
                row_copy(0, 0, s).wait()
            return c
        lax.fori_loop(0, tm, body, 0, unroll=8)

    @pl.when(i > 0)
    def _():
        drain(1 - slot)

    @pl.when(i == n - 1)
    def _():
        drain(slot)


def _dispatch(dest_flat, tail, has, n_used, u2t, n_blocks, tm):
    t_all = u2t.shape[0]
    smem = functools.partial(pl.BlockSpec, memory_space=pltpu.SMEM)
    return pl.pallas_call(
        functools.partial(_dispatch_kernel, tm=tm, n_blocks=n_blocks),
        grid=(t_all // tm,),
        in_specs=[
            smem((tm * TOP_K,), lambda i: (i,)),
            smem((N_EXPERTS,), lambda i: (0,)),
            smem((N_EXPERTS,), lambda i: (0,)),
            smem((1,), lambda i: (0,)),
            pl.BlockSpec((tm, SUBLANES, LANES), lambda i: (i, 0, 0)),
        ],
        out_specs=pl.BlockSpec(memory_space=pl.ANY),
        out_shape=jax.ShapeDtypeStruct((n_blocks * MOE_BLOCK, SUBLANES, LANES), F32),
        scratch_shapes=[pltpu.VMEM((2, tm, SUBLANES, LANES), F32), pltpu.VMEM((MOE_BLOCK, SUBLANES, LANES), F32),
                        pltpu.SemaphoreType.DMA((2,)), pltpu.SemaphoreType.DMA],
        compiler_params=_cparams(("arbitrary",)),
        name="dispatch",
    )(dest_flat, tail, has, n_used, u2t)


def _expert_kernel(be_ref, first_ref, nu_ref, vr_ref, x_ref, wgu_ref, bgu_ref, wd_ref, bd_ref, o_ref, wgu_b, wd_b):
    i = pl.program_id(0)

    @pl.when(first_ref[i] == 1)
    def _():
        wgu_b[...] = wgu_ref[0].astype(BF16)
        wd_b[...] = wd_ref[0].astype(BF16)

    for h in range(MOE_BLOCK // MOE_SUB):
        rows = slice(MOE_SUB * h, MOE_SUB * (h + 1))

        @pl.when(vr_ref[i] > MOE_SUB * h)
        def _():
            x = x_ref[rows].reshape(MOE_SUB, D_MODEL).astype(BF16)
            gu = _dot(x, wgu_b[...]) + bgu_ref[0]
            gate = jnp.minimum(gu[:, :D_EXPERT], SWIGLU_LIMIT)
            up = jnp.clip(gu[:, D_EXPERT:], -SWIGLU_LIMIT, SWIGLU_LIMIT)
            act = (up + 1.0) * (gate * _sigmoid(SWIGLU_ALPHA * gate))
            y = _dot(act.astype(BF16), wd_b[...]) + bd_ref[0]
            o_ref[rows] = y.reshape(MOE_SUB, SUBLANES, LANES)

        @pl.when(vr_ref[i] <= MOE_SUB * h)
        def _():
            o_ref[rows] = jnp.zeros((MOE_SUB, SUBLANES, LANES), F32)


def _experts(block_e, first, n_used, vrows, xr, wgu, bgu, wd, bd):
    n_rows = xr.shape[0]
    n_blocks = n_rows // MOE_BLOCK
    row_block = (MOE_BLOCK, SUBLANES, LANES)
    return pl.pallas_call(
        _expert_kernel,
        grid_spec=pltpu.PrefetchScalarGridSpec(
            num_scalar_prefetch=4,
            grid=(n_blocks,),
            in_specs=[
                pl.BlockSpec(row_block, lambda i, be, fi, nu, vr: (jnp.minimum(i, nu[0] - 1), 0, 0)),
                pl.BlockSpec((1, D_MODEL, 2 * D_EXPERT), lambda i, be, fi, nu, vr: (be[i], 0, 0)),
                pl.BlockSpec((1, 1, 2 * D_EXPERT), lambda i, be, fi, nu, vr: (be[i], 0, 0)),
                pl.BlockSpec((1, D_EXPERT, D_MODEL), lambda i, be, fi, nu, vr: (be[i], 0, 0)),
                pl.BlockSpec((1, 1, D_MODEL), lambda i, be, fi, nu, vr: (be[i], 0, 0)),
            ],
            out_specs=pl.BlockSpec(row_block, lambda i, be, fi, nu, vr: (i, 0, 0)),
            scratch_shapes=[pltpu.VMEM((D_MODEL, 2 * D_EXPERT), BF16), pltpu.VMEM((D_EXPERT, D_MODEL), BF16)],
        ),
        out_shape=jax.ShapeDtypeStruct((n_rows, SUBLANES, LANES), F32),
        compiler_params=pltpu.CompilerParams(dimension_semantics=("arbitrary",),
                                             vmem_limit_bytes=56 * 1024 * 1024),
        name="experts",
    )(block_e, first, n_used, vrows, xr, wgu, bgu, wd, bd)


def _final_kernel(dcur_ref, dnxt_ref, x1_ref, g_ref, nf_ref, yr_hbm, op_ref, os_ref, buf, sem,
                  *, n_prompt_blocks, tm):
    i = pl.program_id(0)
    n = pl.num_programs(0)
    slot = i % 2

    def row_copy(row, s, k, t):
        dst = buf.at[s, k, pl.ds(pl.multiple_of(t * SUBLANES, SUBLANES), SUBLANES)]
        return pltpu.make_async_copy(yr_hbm.at[row], dst, sem.at[s])

    def issue(dest_ref, s):
        def body(g, c):
            rows = [dest_ref[g * ISSUE_GROUP + u] for u in range(ISSUE_GROUP)]
            for u in range(ISSUE_GROUP):
                t = g * (ISSUE_GROUP // TOP_K) + u // TOP_K
                row_copy(rows[u], s, u % TOP_K, t).start(priority=u % 2)
            return c
        lax.fori_loop(0, tm * TOP_K // ISSUE_GROUP, body, 0)

    @pl.when(i == 0)
    def _():
        issue(dcur_ref, 0)

    @pl.when(i + 1 < n)
    def _():
        issue(dnxt_ref, 1 - slot)

    def drain(t, c):
        for k in range(TOP_K):
            row_copy(0, slot, k, t).wait()
        return c
    lax.fori_loop(0, tm, drain, 0, unroll=8)

    g = g_ref[...]
    parts = []
    for c in range(D_MODEL // LANES):
        moe = buf[slot, 0, pl.ds(c, tm, stride=SUBLANES), :] * g[:, 0:1]
        for k in range(1, TOP_K):
            moe = moe + buf[slot, k, pl.ds(c, tm, stride=SUBLANES), :] * g[:, k:k + 1]
        parts.append(x1_ref[:, LANES * c:LANES * (c + 1)] + moe)
    x2 = jnp.concatenate(parts, axis=1)
    ms = jnp.mean(x2 * x2, axis=-1, keepdims=True)
    y = x2 * lax.rsqrt(ms + EPS) * nf_ref[...]

    @pl.when(i < n_prompt_blocks)
    def _():
        op_ref[...] = y

    @pl.when(i >= n_prompt_blocks)
    def _():
        os_ref[...] = y


def _final(dest_flat, x1, yr, gates, nf, t_prompt, tm):
    t_all = x1.shape[0]
    npb = t_prompt // tm
    nsteps = t_all // tm
    return pl.pallas_call(
        functools.partial(_final_kernel, n_prompt_blocks=npb, tm=tm),
        grid=(nsteps,),
        in_specs=[
            pl.BlockSpec((tm * TOP_K,), lambda i: (i,), memory_space=pltpu.SMEM),
            pl.BlockSpec((tm * TOP_K,), lambda i: (jnp.minimum(i + 1, nsteps - 1),), memory_space=pltpu.SMEM),
            pl.BlockSpec((tm, D_MODEL), lambda i: (i, 0)),
            pl.BlockSpec((tm, LANES), lambda i: (i, 0)),
            pl.BlockSpec((1, D_MODEL), lambda i: (0, 0)),
            pl.BlockSpec(memory_space=pl.ANY),
        ],
        out_specs=[
            pl.BlockSpec((tm, D_MODEL), lambda i: (jnp.minimum(i, npb - 1), 0)),
            pl.BlockSpec((tm, D_MODEL), lambda i: (jnp.maximum(i - npb, 0), 0)),
        ],
        out_shape=[
            jax.ShapeDtypeStruct((t_prompt, D_MODEL), F32),
            jax.ShapeDtypeStruct((t_all - t_prompt, D_MODEL), F32),
        ],
        scratch_shapes=[pltpu.VMEM((2, TOP_K, tm * SUBLANES, LANES), F32), pltpu.SemaphoreType.DMA((2,))],
        compiler_params=_cparams(("arbitrary",)),
        name="final",
    )(dest_flat, dest_flat, x1, gates, nf, yr)


def _block_diag4(w):
    w4 = w.reshape(4, 4, RNN_BLOCK, RNN_BLOCK)
    eye = jnp.eye(4, dtype=w.dtype)
    return jnp.einsum('ghij,hk->ghikj', w4, eye).reshape(4, 256, 256)


def _route_tables(route_i, cnt, t_all):
    counts = cnt[0, :N_EXPERTS].astype(jnp.int32)
    padded = (counts + MOE_BLOCK - 1) // MOE_BLOCK * MOE_BLOCK
    pend = jnp.cumsum(padded)
    pstart = pend - padded
    dense = route_i[:, 0:2 * TOP_K].reshape(-1, LANES)
    start = jnp.take(pstart, jnp.clip(dense, 0, N_EXPERTS - 1))
    dest = (dense + jnp.roll(start, -TOP_K, axis=1)).reshape(t_all, 2 * TOP_K)[:, 0:TOP_K]
    n_blocks = _moe_blocks(t_all)
    n_used = (pend[-1] // MOE_BLOCK).astype(jnp.int32)
    blk = jnp.minimum(jnp.arange(n_blocks, dtype=jnp.int32), n_used - 1) * MOE_BLOCK
    block_e = jnp.minimum(jnp.sum((pend[None, :] <= blk[:, None]).astype(jnp.int32), axis=1), N_EXPERTS - 1)
    first = jnp.concatenate([jnp.ones((1,), jnp.int32), (block_e[1:] != block_e[:-1]).astype(jnp.int32)])
    tail = jnp.maximum(pend - MOE_BLOCK, 0)
    has = (counts > 0).astype(jnp.int32)
    blk_all = jnp.arange(n_blocks, dtype=jnp.int32)
    real_end = jnp.take(pstart + counts, block_e)
    vrows = jnp.where(blk_all < n_used, jnp.clip(real_end - blk_all * MOE_BLOCK, 0, MOE_BLOCK), 0)
    return dest.reshape(-1), tail, has, block_e, first, n_used.reshape(1), vrows


def _moe_blocks(t_all):
    return pl.cdiv(t_all * TOP_K, MOE_BLOCK) + N_EXPERTS


def kernel(x_prompt, x_sample, state_lru_conv, state_lru_h, state_ssd_conv, state_ssd, norm_mix, w_in, conv_lru_w, conv_lru_b, w_rg, b_rg, w_ig, b_ig, lam, w_proj_a, conv_ssd_w, conv_ssd_b, dt_bias, a_log, d_skip, ssd_norm, w_proj_b, w_out, norm_ffn, w_router, b_router, w_gate_up, b_gate_up, w_down, b_down, norm_final):
    nbp, seq, _ = x_prompt.shape
    nbs = x_sample.shape[0]
    t_p = nbp * seq
    t_all = t_p + nbs
    l = 0

    wi = w_in[l]
    s = (0, 1024, 2048, 4096, 7168, 7200, 8224, 9248)
    w_main = jnp.concatenate([wi[:, s[3]:s[4]], wi[:, s[0]:s[1]], wi[:, s[2]:s[3]], wi[:, s[1]:s[2]],
                              wi[:, s[5]:s[6]], wi[:, s[6]:s[7]]], axis=1).astype(BF16)
    w_dt = jnp.pad(wi[:, s[4]:s[5]], ((0, 0), (0, LANES - SSD_HEADS))).astype(BF16)
    g_mix = norm_mix[l].reshape(1, D_MODEL)
    wg = jnp.concatenate([_block_diag4(w_rg[l]), _block_diag4(w_ig[l])], axis=2).astype(BF16)
    bg = jnp.concatenate([b_rg[l].reshape(4, 1, 256), b_ig[l].reshape(4, 1, 256)], axis=2)
    lam_r = lam[l].reshape(1, D_RNN)
    lcw, lcb = conv_lru_w[l], conv_lru_b[l].reshape(1, D_RNN)
    scw, scb = conv_ssd_w[l], conv_ssd_b[l].reshape(1, SSD_CONV_DIM)
    dtb = jnp.pad(dt_bias[l], (0, LANES - SSD_HEADS)).reshape(1, LANES)
    a_row = jnp.pad(-jnp.exp(a_log[l]), (0, LANES - SSD_HEADS)).reshape(1, LANES)
    dsk = jnp.repeat(d_skip[l], SSD_HEAD_DIM).reshape(1, D_INNER)
    nrm = ssd_norm[l].reshape(1, D_INNER)
    e_mat = (jnp.arange(LANES)[:, None] == (jnp.arange(D_INNER) // SSD_HEAD_DIM)[None, :]).astype(F32)
    wa, wb, wo = w_proj_a[l].astype(BF16), w_proj_b[l].astype(BF16), w_out[l].astype(BF16)
    nf = norm_ffn[l].reshape(1, D_MODEL)
    wr = jnp.pad(w_router[l], ((0, 0), (0, LANES - N_EXPERTS))).astype(BF16)
    br = jnp.pad(b_router[l], (0, LANES - N_EXPERTS)).reshape(1, LANES)
    bgu =b_gate_up[l].reshape(N_EXPERTS, 1, 2 * D_EXPERT)
    bd = b_down[l].reshape(N_EXPERTS, 1, D_MODEL)

    xp = x_prompt.reshape(t_p, D_MODEL)
    proj_p, projb_p, dt_p = _inproj(xp, g_mix, w_main, w_dt, tm=2048)
    ga_p, p_lc, p_lh = _lru_prompt(proj_p, projb_p, lcw, lcb, wg, bg, lam_r, nbp, seq, tl=512)
    yb_p, p_sc, p_ss = _ssd_prompt(proj_p, projb_p, dt_p, scw, scb, dtb, a_row, dsk, nrm, e_mat, nbp, seq)

    xs_in = x_sample.reshape(nbs, D_MODEL)
    proj_s, projb_s, dt_s = _inproj(xs_in, g_mix, w_main, w_dt, tm=nbs)
    (ga_s, s_lc, s_lh, s_sc, xdt_t, da, b_s, c_s, xs_s) = _sample_pre(
        proj_s, projb_s, dt_s,state_lru_conv[l].reshape(nbs, 3 * D_RNN), state_lru_h[l],
        state_ssd_conv[l].reshape(nbs, 3 * SSD_CONV_DIM), lcw, lcb, wg, bg, lam_r, scw, scb, dtb, a_row, e_mat)
    s_ss, y_t = _sample_state(da[:, :SSD_HEADS].reshape(-1), state_ssd[l].reshape(nbs, D_INNER, D_STATE),
                              xdt_t, b_s.reshape(nbs, 1, -1), c_s.reshape(nbs, 1, -1))
    yb_s = _sample_post(y_t, xs_s, projb_s, dsk, nrm)

    cnt0 = jnp.zeros((1, LANES), F32)
    *prev, cnt_p = _merge(ga_p, yb_p, projb_p, xp, wa, wb, wo, nf, wr, br, cnt0, tm=512, t_all=t_all, row0=0,
                          prev=None)
    x1, u2, route_f, route_i, cnt = _merge(ga_s, yb_s, projb_s, xs_in, wa, wb, wo, nf, wr, br, cnt_p, tm=nbs,
                                           t_all=t_all, row0=t_p, prev=prev)

    dest, tail, has, block_e, first, n_used, vrows = _route_tables(route_i, cnt, t_all)
    xr = _dispatch(dest, tail, has, n_used, u2, _moe_blocks(t_all), tm=nbs)
    yr = _experts(block_e, first, n_used, vrows, xr, w_gate_up[l], bgu, w_down[l], bd)
    y_p, y_s = _final(dest, x1, yr, route_f, norm_final.reshape(1, D_MODEL), t_p, tm=nbs)

    return (y_p.reshape(nbp, seq, D_MODEL), y_s.reshape(nbs, 1, D_MODEL),
            p_lc[None], p_lh.reshape(1, nbp, D_RNN), p_sc[None],
            p_ss.reshape(1, nbp, SSD_HEADS, SSD_HEAD_DIM, D_STATE),
            s_lc.reshape(1, nbs, CONV_W - 1, D_RNN), s_lh[None],
            s_sc.reshape(1, nbs, CONV_W - 1, SSD_CONV_DIM),
            s_ss.reshape(1, nbs, SSD_HEADS, SSD_HEAD_DIM, D_STATE))
```

```python
import functools

import jax
import jax.numpy as jnp
from jax import lax
from jax.experimental import pallas as pl
from jax.experimental.pallas import tpu as pltpu

F32 = jnp.float32
BF16 = jnp.bfloat16

D_MODEL = 1024
D_RNN = 1024
RNN_HEADS = 16
RNN_BLOCK = 64
CONV_W = 4
RG_C = 8.0
D_INNER = 2048
SSD_HEAD_DIM = 64
SSD_HEADS = 32
SSD_GROUPS = 4
D_STATE = 128
SSD_CONV_DIM = D_INNER + 2 * SSD_GROUPS * D_STATE
CHUNK = 128
N_EXPERTS = 32
TOP_K = 4
D_EXPERT = 1024
SWIGLU_LIMIT = 7.0
SWIGLU_ALPHA = 1.702
EPS = 1e-6

LANES = 128
SUBLANES = 8
N_MAIN = 9 * 1024
COL_Z, COL_LRU, COL_XBC, COL_GATE, COL_GA, COL_GB = 0, 2, 1, 6, 7, 8
MOE_BLOCK = 512
SAMPLE_STATE_ROWS = 2
ISSUE_GROUP = 32
VMEM_LIMIT = 48 * 1024 * 1024


def _cparams(sem):
    return pltpu.CompilerParams(dimension_semantics=sem, vmem_limit_bytes=VMEM_LIMIT)


def _sigmoid(x):
    return jax.nn.sigmoid(x)


def _softplus(x):
    return jnp.maximum(x, 0.0) + jnp.log1p(jnp.exp(-jnp.abs(x)))


def _gelu_tanh(x):
    return 0.5 * x * (1.0 + jnp.tanh(0.7978845608028654 * (x + 0.044715 * (x * x * x))))


def _dot(a, b):
    return jnp.dot(a, b, preferred_element_type=F32)


def _dot_exact(a, b):
    return jnp.dot(a, b, preferred_element_type=F32, precision=lax.Precision.HIGHEST)


def _inproj_kernel(x_ref, g_ref, w_ref, wdt_ref, o_ref, dt_ref, u_ref):
    @pl.when(pl.program_id(1) == 0)
    def _():
        x = x_ref[...]
        ms = jnp.mean(x * x, axis=-1, keepdims=True)
        u = (x * lax.rsqrt(ms + EPS) * g_ref[...]).astype(BF16)
        u_ref[...] = u
        dt_ref[...] = _dot(u, wdt_ref[...])

    o_ref[...] = _dot(u_ref[...], w_ref[...])


def _inproj(x, g, w_main, w_dt, tm):
    t = x.shape[0]
    return pl.pallas_call(
        _inproj_kernel,
        grid=(t // tm, N_MAIN // 1024),
        in_specs=[
            pl.BlockSpec((tm, D_MODEL), lambda i, j: (i, 0)),
            pl.BlockSpec((1, D_MODEL), lambda i, j: (0, 0)),
            pl.BlockSpec((D_MODEL, 1024), lambda i, j: (0, j)),
            pl.BlockSpec((D_MODEL, LANES), lambda i, j: (0, 0)),
        ],
        out_specs=[
            pl.BlockSpec((tm, 1024), lambda i, j: (i, j)),
            pl.BlockSpec((tm, LANES), lambda i, j: (i, 0)),
        ],
        out_shape=[jax.ShapeDtypeStruct((t, N_MAIN), F32), jax.ShapeDtypeStruct((t, LANES), F32)],
        scratch_shapes=[pltpu.VMEM((tm, D_MODEL), BF16)],
        compiler_params=pltpu.CompilerParams(dimension_semantics=("parallel", "arbitrary"),
                                             vmem_limit_bytes=56 * 1024 * 1024),
        name="inproj",
    )(x, g, w_main, w_dt)


def _lru_gates(xc, wg_ref, bg_ref, lam):
    xcb = xc.astype(BF16)
    sp = _softplus(-lam)
    a_parts, b_parts = [], []
    for g in range(4):
        sl = slice(256 * g, 256 * (g + 1))
        pre = _dot(xcb[:, sl], wg_ref[g]) + bg_ref[g]
        r = _sigmoid(pre[:, :256])
        i = _sigmoid(pre[:, 256:])
        log_a = (-RG_C * r) * sp[:, sl]
        a_parts.append(jnp.exp(log_a))
        th = jnp.tanh(log_a)
        mult = jnp.sqrt(-2.0 * th / (1.0 - th))
        b_parts.append(mult * (i * xc[:, sl]))
    return jnp.concatenate(a_parts, axis=1), jnp.concatenate(b_parts, axis=1)


def _lru_kernel(x_ref, gate_ref, cw_ref, cb_ref, wg_ref, bg_ref, lam_ref,
                out_ref, conv_ref, h_ref, xx_ref, a_ref, b_ref, hc_ref, *, tl):
    @pl.when(pl.program_id(1) == 0)
    def _():
        xx_ref[0:SUBLANES, :] = jnp.zeros((SUBLANES, D_RNN), F32)
        hc_ref[...] = jnp.zeros((SUBLANES, D_RNN), F32)

    x = x_ref[...]
    xx_ref[SUBLANES:SUBLANES + tl, :] = x
    cw = cw_ref[...]
    xc = (cw[3:4] * x + cw[2:3] * xx_ref[7:7 + tl, :] + cw[1:2] * xx_ref[6:6 + tl, :]
          + cw[0:1] * xx_ref[5:5 + tl, :] + cb_ref[...])
    xx_ref[0:SUBLANES, :] = xx_ref[tl:tl + SUBLANES, :]
    conv_ref[0] = x_ref[tl - 3:tl, :]

    a, b = _lru_gates(xc, wg_ref, bg_ref, lam_ref[...])
    nt = tl // SUBLANES
    a = a.reshape(nt, SUBLANES, D_RNN)
    b = b.reshape(nt, SUBLANES, D_RNN)
    rows = lax.broadcasted_iota(jnp.int32, (nt, SUBLANES, D_RNN), 1)
    for d in (1, 2, 4):
        m = rows >= d
        b = jnp.where(m, a * pltpu.roll(b, d, 1) + b, b)
        a = jnp.where(m, a * pltpu.roll(a, d, 1), a)
    a_ref[...] = a
    b_ref[...] = b

    def carry(k, hprev):
        h = a_ref[k] * hprev + b_ref[k]
        b_ref[k] = h
        return jnp.broadcast_to(h[SUBLANES - 1:SUBLANES, :], (SUBLANES, D_RNN))

    hlast = lax.fori_loop(0, nt, carry, hc_ref[...])
    hc_ref[...] = hlast
    h_ref[0] = hlast[0:1, :]
    h_all = b_ref[...].reshape(tl, D_RNN)
    out_ref[...] = (_gelu_tanh(gate_ref[...]) * h_all).astype(BF16)


def _lru_prompt(proj, cw, cb, wg, bg, lam, nb, seq, tl):
    nl = seq // tl
    return pl.pallas_call(
        functools.partial(_lru_kernel, tl=tl),
        grid=(nb, nl),
        in_specs=[
            pl.BlockSpec((tl, 1024), lambda b, l: (b * nl + l, COL_LRU)),
            pl.BlockSpec((tl, 1024), lambda b, l: (b * nl + l, COL_GATE)),
            pl.BlockSpec((CONV_W, D_RNN), lambda b, l: (0, 0)),
            pl.BlockSpec((1, D_RNN), lambda b, l: (0, 0)),
            pl.BlockSpec((4, 256, 512), lambda b, l: (0, 0, 0)),
            pl.BlockSpec((4, 1, 512), lambda b, l: (0, 0, 0)),
            pl.BlockSpec((1, D_RNN), lambda b, l: (0, 0)),
        ],
        out_specs=[
            pl.BlockSpec((tl, D_RNN), lambda b, l: (b * nl + l, 0)),
            pl.BlockSpec((1, CONV_W - 1, D_RNN), lambda b, l: (b, 0, 0)),
            pl.BlockSpec((1, 1, D_RNN), lambda b, l: (b, 0, 0)),
        ],
        out_shape=[
            jax.ShapeDtypeStruct((nb * seq, D_RNN), BF16),
            jax.ShapeDtypeStruct((nb, CONV_W - 1, D_RNN), F32),
            jax.ShapeDtypeStruct((nb, 1, D_RNN), F32),
        ],
        scratch_shapes=[
            pltpu.VMEM((tl + SUBLANES, D_RNN), F32),
            pltpu.VMEM((tl // SUBLANES, SUBLANES, D_RNN), F32),
            pltpu.VMEM((tl // SUBLANES, SUBLANES, D_RNN), F32),
            pltpu.VMEM((SUBLANES, D_RNN), F32),
        ],
        compiler_params=_cparams(("parallel", "arbitrary")),
        name="lru_prompt",
    )(proj, proj, cw, cb, wg, bg, lam)


def _group_norm_gate(y, z, nrm):
    y = y * (z * _sigmoid(z))
    gw = D_INNER // SSD_GROUPS
    parts = []
    for g in range(SSD_GROUPS):
        yg = y[:, gw * g:gw * (g + 1)]
        ms = jnp.mean(yg * yg, axis=-1, keepdims=True)
        parts.append(yg * lax.rsqrt(ms + EPS))
    return jnp.concatenate(parts, axis=1) * nrm


def _ssd_kernel(xbc_ref, z_ref, dt_ref, cw_ref, cb_ref, dtb_ref, a_ref, dsk_ref, nrm_ref, e_ref,
                y_ref, conv_ref, st_ref, xx_ref, stt_ref, yacc_ref):
    q = CHUNK

    @pl.when(pl.program_id(1) == 0)
    def _():
        xx_ref[0:SUBLANES, :] = jnp.zeros((SUBLANES, SSD_CONV_DIM), F32)
        stt_ref[...] = jnp.zeros((D_STATE, D_INNER), F32)

    xbc = xbc_ref[...]
    xx_ref[SUBLANES:SUBLANES + q, :] = xbc
    cw = cw_ref[...]
    conv = (cw[3:4] * xbc + cw[2:3] * xx_ref[7:7 + q, :] + cw[1:2] * xx_ref[6:6 + q, :]
            + cw[0:1] * xx_ref[5:5 + q, :] + cb_ref[...])
    xx_ref[0:SUBLANES, :] = xx_ref[q:q + SUBLANES, :]
    conv_ref[0] = xbc_ref[q - 3:q, :]
    act = conv * _sigmoid(conv)
    xs = act[:, :D_INNER]
    bm = act[:, D_INNER:D_INNER + SSD_GROUPS * D_STATE]
    cm = act[:, D_INNER + SSD_GROUPS * D_STATE:]

    dt = _softplus(dt_ref[...] + dtb_ref[...])
    dta = dt * a_ref[...]
    iq = lax.broadcasted_iota(jnp.int32, (q, q), 0)
    ik = lax.broadcasted_iota(jnp.int32, (q, q), 1)
    causal = iq >= ik
    a_cum = _dot_exact(causal.astype(F32), dta)
    a_cum_t = a_cum.T
    dt_t = dt.T
    e_cum = jnp.exp(a_cum)
    w_t = jnp.exp(a_cum_t[:, q - 1:q] - a_cum_t) * dt_t
    a_last = jnp.broadcast_to(a_cum[q - 1:q, :], (SUBLANES, LANES))
    da_e = jnp.exp(_dot_exact(a_last, e_ref[...])[0:1, :])
    lo = ik < SSD_HEAD_DIM

    for g in range(SSD_GROUPS):
        bg = bm[:, D_STATE * g:D_STATE * (g + 1)]
        cg = cm[:, D_STATE * g:D_STATE * (g + 1)]
        cb = lax.dot_general(cg.astype(BF16), bg.astype(BF16), (((1,), (1,)), ((), ())),
                             preferred_element_type=F32)
        bg_t = bg.T
        for pp in range(4):
            j = 4 * g + pp
            ms, cs, bw = [], [], []
            for s in range(2):
                h = 2 * j + s
                col = a_cum[:, h:h + 1]
                row = a_cum_t[h:h + 1, :]
                dec = jnp.exp(jnp.where(causal, col - row, -jnp.inf))
                ms.append(cb * dec * dt_t[h:h + 1, :])
                cs.append(cg * e_cum[:, h:h + 1])
                bw.append(bg_t * w_t[h:h + 1, :])
            sl = slice(LANES * j, LANES * (j + 1))
            xp = xs[:, sl]
            rhs_x = jnp.concatenate([jnp.where(lo, xp, 0.0), jnp.where(lo, 0.0, xp)], axis=0).astype(BF16)
            stp = stt_ref[:, sl]
            rhs_s = jnp.concatenate([jnp.where(lo, stp, 0.0), jnp.where(lo, 0.0, stp)], axis=0).astype(BF16)
            l_m = jnp.concatenate(ms, axis=1).astype(BF16)
            l_c = jnp.concatenate(cs, axis=1).astype(BF16)
            l_b = jnp.concatenate(bw, axis=1).astype(BF16)
            yacc_ref[:, sl] = _dot(l_m, rhs_x) + _dot(l_c, rhs_s)
            stt_ref[:, sl] = stp * da_e[:, sl] + _dot(l_b, rhs_x)

    y = yacc_ref[...] + dsk_ref[...] * xs
    y_ref[...] = _group_norm_gate(y, z_ref[...], nrm_ref[...]).astype(BF16)

    @pl.when(pl.program_id(1) == pl.num_programs(1) - 1)
    def _():
        st_ref[0] = stt_ref[...].T


def _ssd_prompt(proj, dt, cw, cb, dtb, a_row, dsk, nrm, e_mat, nb, seq):
    nc = seq // CHUNK
    return pl.pallas_call(
        _ssd_kernel,
        grid=(nb, nc),
        in_specs=[
            pl.BlockSpec((CHUNK, SSD_CONV_DIM), lambda b, c: (b * nc + c, COL_XBC)),
            pl.BlockSpec((CHUNK, D_INNER), lambda b, c: (b * nc + c, COL_Z)),
            pl.BlockSpec((CHUNK, LANES), lambda b, c: (b * nc + c, 0)),
            pl.BlockSpec((CONV_W, SSD_CONV_DIM), lambda b, c: (0, 0)),
            pl.BlockSpec((1, SSD_CONV_DIM), lambda b, c: (0, 0)),
            pl.BlockSpec((1, LANES), lambda b, c: (0, 0)),
            pl.BlockSpec((1, LANES), lambda b, c: (0, 0)),
            pl.BlockSpec((1, D_INNER), lambda b, c: (0, 0)),
            pl.BlockSpec((1, D_INNER), lambda b, c: (0, 0)),
            pl.BlockSpec((LANES, D_INNER), lambda b, c: (0, 0)),
        ],
        out_specs=[
            pl.BlockSpec((CHUNK, D_INNER), lambda b, c: (b * nc + c, 0)),
            pl.BlockSpec((1, CONV_W - 1, SSD_CONV_DIM), lambda b, c: (b, 0, 0)),
            pl.BlockSpec((1, D_INNER, D_STATE), lambda b, c: (b, 0, 0)),
        ],
        out_shape=[
            jax.ShapeDtypeStruct((nb * seq, D_INNER), BF16),
            jax.ShapeDtypeStruct((nb, CONV_W - 1, SSD_CONV_DIM), F32),
            jax.ShapeDtypeStruct((nb, D_INNER, D_STATE), F32),
        ],
        scratch_shapes=[
            pltpu.VMEM((CHUNK + SUBLANES, SSD_CONV_DIM), F32),
            pltpu.VMEM((D_STATE, D_INNER), F32),
            pltpu.VMEM((CHUNK, D_INNER), F32),
        ],
        compiler_params=_cparams(("parallel", "arbitrary")),
        name="ssd_prompt",
    )(proj, proj, dt, cw, cb, dtb, a_row, dsk, nrm, e_mat)


def _sample_pre_kernel(proj_ref, dt_ref, lconv_ref, h0_ref, sconv_ref,
                       lcw_ref, lcb_ref, wg_ref, bg_ref, lam_ref,
                       scw_ref, scb_ref, dtb_ref, a_ref, e_ref,
                       ga_ref, lconv_o, h_o, sconv_o, xdt_t_o, da_o, b_o, c_o, xs_o):
    x = proj_ref[:, 2048:3072]
    gate = proj_ref[:, 6144:7168]
    cw = lcw_ref[...]
    c0 = lconv_ref[:, 0:1024]
    c1 = lconv_ref[:, 1024:2048]
    c2 = lconv_ref[:, 2048:3072]
    xc = cw[3:4] * x + cw[2:3] * c2 + cw[1:2] * c1 + cw[0:1] * c0 + lcb_ref[...]
    lconv_o[:, 0:1024] = c1
    lconv_o[:, 1024:2048] = c2
    lconv_o[:, 2048:3072] = x
    a, bt = _lru_gates(xc, wg_ref, bg_ref, lam_ref[...])
    h = a * h0_ref[...] + bt
    h_o[...] = h
    ga_ref[...] = (_gelu_tanh(gate) * h).astype(BF16)

    xbc = proj_ref[:, 3072:6144]
    sw = scw_ref[...]
    w = SSD_CONV_DIM
    s0 = sconv_ref[:, 0:w]
    s1 = sconv_ref[:, w:2 * w]
    s2 = sconv_ref[:, 2 * w:3 * w]
    conv = sw[3:4] * xbc + sw[2:3] * s2 + sw[1:2] * s1 + sw[0:1] * s0 + scb_ref[...]
    sconv_o[:, 0:w] = s1
    sconv_o[:, w:2 * w] = s2
    sconv_o[:, 2 * w:3 * w] = xbc
    act = conv * _sigmoid(conv)
    xs = act[:, :D_INNER]
    xs_o[...] = xs
    b_o[...] = act[:, D_INNER:D_INNER + SSD_GROUPS * D_STATE]
    c_o[...] = act[:, D_INNER + SSD_GROUPS * D_STATE:]
    dt = _softplus(dt_ref[...] + dtb_ref[...])
    da_o[...] = jnp.exp(dt * a_ref[...])
    dt_e = _dot_exact(dt, e_ref[...])
    xdt_t_o[...] = (xs * dt_e).T


def _sample_pre(proj, dt, lconv, h0, sconv, lcw, lcb, wg, bg, lam, scw, scb, dtb, a_row, e_mat):
    nb = proj.shape[0]
    out_shape = [
        jax.ShapeDtypeStruct((nb, D_RNN), BF16),
        jax.ShapeDtypeStruct((nb, 3 * D_RNN), F32),
        jax.ShapeDtypeStruct((nb, D_RNN), F32),
        jax.ShapeDtypeStruct((nb, 3 * SSD_CONV_DIM), F32),
        jax.ShapeDtypeStruct((D_INNER, nb), F32),
        jax.ShapeDtypeStruct((nb, LANES), F32),
        jax.ShapeDtypeStruct((nb, SSD_GROUPS * D_STATE), F32),
        jax.ShapeDtypeStruct((nb, SSD_GROUPS * D_STATE), F32),
        jax.ShapeDtypeStruct((nb, D_INNER), F32),
    ]
    return pl.pallas_call(
        _sample_pre_kernel,
        out_shape=out_shape,
        compiler_params=pltpu.CompilerParams(vmem_limit_bytes=VMEM_LIMIT),
        name="sample_pre",
    )(proj, dt, lconv, h0, sconv, lcw, lcb, wg, bg, lam, scw, scb, dtb, a_row, e_mat)


def _sample_state_kernel(da_ref, s0_ref, xdt_t_ref, b_ref, c_ref, s1_ref, y_t_ref):
    nb = xdt_t_ref.shape[1]

    @pl.when(pl.program_id(0) == 0)
    def _():
        y_t_ref[...] = jnp.zeros(y_t_ref.shape, F32)

    lane = lax.broadcasted_iota(jnp.int32, (D_INNER, nb), 1)
    for j in range(SAMPLE_STATE_ROWS):
        b = pl.program_id(0) * SAMPLE_STATE_ROWS + j
        sel = lane == b
        xcol = jnp.sum(jnp.where(sel, xdt_t_ref[...], 0.0), axis=1, keepdims=True)
        ycols = []
        for g in range(SSD_GROUPS):
            brow = b_ref[j, :, D_STATE * g:D_STATE * (g + 1)]
            crow = c_ref[j, :, D_STATE * g:D_STATE * (g + 1)]
            for hh in range(SSD_HEADS // SSD_GROUPS):
                h = (SSD_HEADS // SSD_GROUPS) * g + hh
                sl = slice(SSD_HEAD_DIM * h, SSD_HEAD_DIM * (h + 1))
                s1 = s0_ref[j, sl, :] * da_ref[b * SSD_HEADS + h] + xcol[sl, :] * brow
                s1_ref[j, sl, :] = s1
                ycols.append(jnp.sum(s1 * crow, axis=1, keepdims=True))
        ycol = jnp.concatenate(ycols, axis=0)
        y_t_ref[...] = jnp.where(sel, ycol, y_t_ref[...])


def _sample_state(da_flat, s0, xdt_t, bmat, cmat):
    nb = s0.shape[0]
    return pl.pallas_call(
        _sample_state_kernel,
        grid_spec=pltpu.PrefetchScalarGridSpec(
            num_scalar_prefetch=1,
            grid=(nb // SAMPLE_STATE_ROWS,),
            in_specs=[
                pl.BlockSpec((SAMPLE_STATE_ROWS, D_INNER, D_STATE), lambda b, da: (b, 0, 0)),
                pl.BlockSpec((D_INNER, nb), lambda b, da: (0, 0)),
                pl.BlockSpec((SAMPLE_STATE_ROWS, 1, SSD_GROUPS * D_STATE), lambda b, da: (b, 0, 0)),
                pl.BlockSpec((SAMPLE_STATE_ROWS, 1, SSD_GROUPS * D_STATE), lambda b, da: (b, 0, 0)),
            ],
            out_specs=[
                pl.BlockSpec((SAMPLE_STATE_ROWS, D_INNER, D_STATE), lambda b, da: (b, 0, 0)),
                pl.BlockSpec((D_INNER, nb), lambda b, da: (0, 0)),
            ],
        ),
        out_shape=[
            jax.ShapeDtypeStruct((nb, D_INNER, D_STATE), F32),
            jax.ShapeDtypeStruct((D_INNER, nb), F32),
        ],
        compiler_params=_cparams(("arbitrary",)),
        name="sample_state",
    )(da_flat, s0, xdt_t, bmat, cmat)


def _sample_post_kernel(y_t_ref, xs_ref, z_ref, dsk_ref, nrm_ref, y_ref):
    y = y_t_ref[...].T + dsk_ref[...] * xs_ref[...]
    y_ref[...] = _group_norm_gate(y, z_ref[:, 0:D_INNER], nrm_ref[...]).astype(BF16)


def _sample_post(y_t, xs, proj, dsk, nrm):
    nb = xs.shape[0]
    return pl.pallas_call(
        _sample_post_kernel,
        grid=(1,),
        in_specs=[
            pl.BlockSpec((D_INNER, nb), lambda i: (0, 0)),
            pl.BlockSpec((nb, D_INNER), lambda i: (0, 0)),
            pl.BlockSpec((nb, D_INNER), lambda i: (0, COL_Z)),
            pl.BlockSpec((1, D_INNER), lambda i: (0, 0)),
            pl.BlockSpec((1, D_INNER), lambda i: (0, 0)),
        ],
        out_specs=pl.BlockSpec((nb, D_INNER), lambda i: (0, 0)),
        out_shape=jax.ShapeDtypeStruct((nb, D_INNER), BF16),
        compiler_params=_cparams(("arbitrary",)),
        name="sample_post",
    )(y_t, xs, proj, dsk, nrm)


def _merge_kernel(ga_ref, yb_ref, gta_ref, gtb_ref, x_ref, wa_ref, wb_ref, wo_ref, nf_ref, wr_ref, br_ref,
                  cnt0_ref, *refs, n_real):
    x1_ref, u2_ref, rf_ref, ri_ref, cnt_ref, carry_ref = refs[-6:]

    @pl.when(pl.program_id(0) == 0)
    def _():
        carry_ref[...] = cnt0_ref[...]

    @pl.when(pl.program_id(0) >= n_real)
    def _():
        x1_ref[...] = jnp.zeros(x1_ref.shape, x1_ref.dtype)
        u2_ref[...] = jnp.zeros(u2_ref.shape, u2_ref.dtype)
        rf_ref[...] = jnp.zeros(rf_ref.shape, rf_ref.dtype)
        ri_ref[...] = jnp.zeros(ri_ref.shape, ri_ref.dtype)

    @pl.when(pl.program_id(0) < n_real)
    def _():
        _merge_body(ga_ref, yb_ref, gta_ref, gtb_ref, x_ref, wa_ref, wb_ref, wo_ref, nf_ref, wr_ref, br_ref,
                    x1_ref, u2_ref, rf_ref, ri_ref, carry_ref)

    cnt_ref[...] = carry_ref[...]


def _merge_body(ga_ref, yb_ref, gta_ref, gtb_ref, x_ref, wa_ref, wb_ref, wo_ref, nf_ref, wr_ref, br_ref,
                x1_ref, u2_ref, rf_ref, ri_ref, carry_ref):
    br_a = _dot(ga_ref[...], wa_ref[...])
    br_b = _dot(yb_ref[...], wb_ref[...])
    merged = _sigmoid(gta_ref[...]) * br_a + _sigmoid(gtb_ref[...]) * br_b
    x1 = x_ref[...] + _dot(merged.astype(BF16), wo_ref[...])
    x1_ref[...] = x1
    ms = jnp.mean(x1 * x1, axis=-1, keepdims=True)
    u2f = x1 * lax.rsqrt(ms + EPS) * nf_ref[...]
    u2_ref[...] = u2f.reshape(u2_ref.shape)
    u2 = u2f.astype(BF16)
    logits = _dot(u2, wr_ref[...]) + br_ref[...]

    tm = logits.shape[0]
    lane = lax.broadcasted_iota(jnp.int32, (tm, LANES), 1)
    lane_f = lane.astype(F32)
    v = jnp.where(lane < N_EXPERTS, logits, -jnp.inf)
    sel = jnp.zeros((tm, LANES), F32)
    hots, vals, idxs = [], [], []
    for _ in range(TOP_K):
        m = jnp.max(v, axis=1, keepdims=True)
        idx = jnp.min(jnp.where(v == m, lane_f, float(LANES)), axis=1, keepdims=True)
        hot = lane_f == idx
        hots.append(hot)
        vals.append(m)
        idxs.append(idx)
        v = jnp.where(hot, -jnp.inf, v)
        sel = sel + hot.astype(F32)
    exps = [jnp.exp(m - vals[0]) for m in vals]
    den = exps[0] + exps[1] + exps[2] + exps[3]
    ir = lax.broadcasted_iota(jnp.int32, (tm, tm), 0)
    ic = lax.broadcasted_iota(jnp.int32, (tm, tm), 1)
    before = _dot((ir > ic).astype(BF16), sel.astype(BF16)) + carry_ref[...]
    rf = jnp.zeros((tm, LANES), F32)
    ri = jnp.zeros((tm, LANES), F32)
    for k in range(TOP_K):
        rank = jnp.sum(jnp.where(hots[k], before, 0.0), axis=1, keepdims=True)
        rf = jnp.where(lane == k, exps[k] / den, rf)
        ri = jnp.where(lane == k, rank, jnp.where(lane == TOP_K + k, idxs[k], ri))
    rf_ref[...] = rf
    ri_ref[...] = ri.astype(jnp.int32)
    carry_ref[...] = carry_ref[...] + jnp.sum(sel, axis=0, keepdims=True)


def _merge(ga, yb, proj, x, wa, wb, wo, nf, wr, br, cnt0, tm, t_all, row0, prev):
    t = x.shape[0]
    blk0 = row0 // tm
    n_real = t // tm
    n_fill = 0 if prev is not None else pl.cdiv(t_all - row0 - t, tm)
    const = lambda i: (0, 0)
    rows = lambda i: jnp.minimum(i, n_real - 1)
    in_specs = [
        pl.BlockSpec((tm, D_RNN), lambda i: (rows(i), 0)),
        pl.BlockSpec((tm, D_INNER), lambda i: (rows(i), 0)),
        pl.BlockSpec((tm, 1024), lambda i: (rows(i), COL_GA)),
        pl.BlockSpec((tm, 1024), lambda i: (rows(i), COL_GB)),
        pl.BlockSpec((tm, D_MODEL), lambda i: (rows(i), 0)),
        pl.BlockSpec((D_RNN, D_MODEL), const),
        pl.BlockSpec((D_INNER, D_MODEL), const),
        pl.BlockSpec((D_MODEL, D_MODEL), const),
        pl.BlockSpec((1, D_MODEL), const),
        pl.BlockSpec((D_MODEL, LANES), const),
        pl.BlockSpec((1, LANES), const),
        pl.BlockSpec((1, LANES), const),
    ]
    args = [ga, yb, proj, proj, x, wa, wb, wo, nf, wr, br, cnt0]
    aliases = {}
    if prev is not None:
        in_specs += [pl.BlockSpec(memory_space=pl.ANY)] * 4
        aliases = {len(args) + k: k for k in range(4)}
        args += list(prev)
    return pl.pallas_call(
        functools.partial(_merge_kernel, n_real=n_real),
        grid=(n_real + n_fill,),
        in_specs=in_specs,
        out_specs=[
            pl.BlockSpec((tm, D_MODEL), lambda i: (blk0 + i, 0)),
            pl.BlockSpec((tm, SUBLANES, LANES), lambda i: (blk0 + i, 0, 0)),
            pl.BlockSpec((tm, LANES), lambda i: (blk0 + i, 0)),
            pl.BlockSpec((tm, LANES), lambda i: (blk0 + i, 0)),
            pl.BlockSpec((1, LANES), const),
        ],
        out_shape=[
            jax.ShapeDtypeStruct((t_all, D_MODEL), F32),
            jax.ShapeDtypeStruct((t_all, SUBLANES, LANES), F32),
            jax.ShapeDtypeStruct((t_all, LANES), F32),
            jax.ShapeDtypeStruct((t_all, LANES), jnp.int32),
            jax.ShapeDtypeStruct((1, LANES), F32),
        ],
        scratch_shapes=[pltpu.VMEM((1, LANES), F32)],
        input_output_aliases=aliases,
        compiler_params=_cparams(("arbitrary",)),
        name="merge_sample" if prev is not None else "merge_prompt",
    )(*args)


def _dispatch_kernel(dest_ref, tail_ref, has_ref, nu_ref, u2_ref, xr_hbm, stage, zero_ref, sem, zsem,
                     *, tm, n_blocks):
    i = pl.program_id(0)
    n = pl.num_programs(0)
    slot = i % 2

    def zero_copy(row0):
        return pltpu.make_async_copy(zero_ref, xr_hbm.at[pl.ds(row0, MOE_BLOCK)], zsem)

    @pl.when(i == 0)
    def _():
        zero_ref[...] = jnp.zeros(zero_ref.shape, F32)
        for e in range(N_EXPERTS):
            @pl.when(has_ref[e] == 1)
            def _():
                zero_copy(tail_ref[e]).start()

        def fill(j, c):
            zero_copy(j * MOE_BLOCK).start()
            return c
        lax.fori_loop(nu_ref[0], n_blocks, fill, 0)
        for e in range(N_EXPERTS):
            @pl.when(has_ref[e] == 1)
            def _():
                zero_copy(0).wait()

        def fill_wait(j, c):
            zero_copy(0).wait()
            return c
        lax.fori_loop(nu_ref[0], n_blocks, fill_wait, 0)

    def row_copy(tok, row, s):
        return pltpu.make_async_copy(stage.at[s, tok], xr_hbm.at[row], sem.at[s])

    stage[slot] = u2_ref[...]

    def issue(g, c):
        rows = [dest_ref[g * ISSUE_GROUP + u] for u in range(ISSUE_GROUP)]
        for u in range(ISSUE_GROUP):
            row_copy(g * (ISSUE_GROUP // TOP_K) + u // TOP_K, rows[u], slot).start(priority=u % 2)
        return c
    lax.fori_loop(0, tm * TOP_K // ISSUE_GROUP, issue, 0)

    def drain(s):
        def body(t, c):
            for k in range(TOP_K):
                row_copy(0, 0, s).wait()
            return c
        lax.fori_loop(0, tm, body, 0, unroll=8)

    @pl.when(i > 0)
    def _():
        drain(1 - slot)

    @pl.when(i == n - 1)
    def _():
        drain(slot)


def _dispatch(dest_flat, tail, has, n_used, u2t, n_blocks, tm):
    t_all = u2t.shape[0]
    smem = functools.partial(pl.BlockSpec, memory_space=pltpu.SMEM)
    return pl.pallas_call(
        functools.partial(_dispatch_kernel, tm=tm, n_blocks=n_blocks),
        grid=(t_all // tm,),
        in_specs=[
            smem((tm * TOP_K,), lambda i: (i,)),
            smem((N_EXPERTS,), lambda i: (0,)),
            smem((N_EXPERTS,), lambda i: (0,)),
            smem((1,), lambda i: (0,)),
            pl.BlockSpec((tm, SUBLANES, LANES), lambda i: (i, 0, 0)),
        ],
        out_specs=pl.BlockSpec(memory_space=pl.ANY),
        out_shape=jax.ShapeDtypeStruct((n_blocks * MOE_BLOCK, SUBLANES, LANES), F32),
        scratch_shapes=[pltpu.VMEM((2, tm, SUBLANES, LANES), F32), pltpu.VMEM((MOE_BLOCK, SUBLANES, LANES), F32),
                        pltpu.SemaphoreType.DMA((2,)), pltpu.SemaphoreType.DMA],
        compiler_params=_cparams(("arbitrary",)),
        name="dispatch",
    )(dest_flat, tail, has, n_used, u2t)


def _expert_kernel(be_ref, first_ref, nu_ref, x_ref, wgu_ref, bgu_ref, wd_ref, bd_ref, o_ref, wgu_b, wd_b):
    i = pl.program_id(0)

    @pl.when(first_ref[i] == 1)
    def _():
        wgu_b[...] = wgu_ref[0].astype(BF16)
        wd_b[...] = wd_ref[0].astype(BF16)

    @pl.when(i < nu_ref[0])
    def _():
        x = x_ref[...].reshape(MOE_BLOCK, D_MODEL).astype(BF16)
        gu = _dot(x, wgu_b[...]) + bgu_ref[0]
        gate = jnp.minimum(gu[:, :D_EXPERT], SWIGLU_LIMIT)
        up = jnp.clip(gu[:, D_EXPERT:], -SWIGLU_LIMIT, SWIGLU_LIMIT)
        act = (up + 1.0) * (gate * _sigmoid(SWIGLU_ALPHA * gate))
        y = _dot(act.astype(BF16), wd_b[...]) + bd_ref[0]
        o_ref[...] = y.reshape(o_ref.shape)

    @pl.when(i >= nu_ref[0])
    def _():
        o_ref[...] = jnp.zeros(o_ref.shape, F32)


def _experts(block_e, first, n_used, xr, wgu, bgu, wd, bd):
    n_rows = xr.shape[0]
    n_blocks = n_rows // MOE_BLOCK
    row_block = (MOE_BLOCK, SUBLANES, LANES)
    return pl.pallas_call(
        _expert_kernel,
        grid_spec=pltpu.PrefetchScalarGridSpec(
            num_scalar_prefetch=3,
            grid=(n_blocks,),
            in_specs=[
                pl.BlockSpec(row_block, lambda i, be, fi, nu: (jnp.minimum(i, nu[0] - 1), 0, 0)),
                pl.BlockSpec((1, D_MODEL, 2 * D_EXPERT), lambda i, be, fi, nu: (be[i], 0, 0)),
                pl.BlockSpec((1, 1, 2 * D_EXPERT), lambda i, be, fi, nu: (be[i], 0, 0)),
                pl.BlockSpec((1, D_EXPERT, D_MODEL), lambda i, be, fi, nu: (be[i], 0, 0)),
                pl.BlockSpec((1, 1, D_MODEL), lambda i, be, fi, nu: (be[i], 0, 0)),
            ],
            out_specs=pl.BlockSpec(row_block, lambda i, be, fi, nu: (i, 0, 0)),
            scratch_shapes=[pltpu.VMEM((D_MODEL, 2 * D_EXPERT), BF16), pltpu.VMEM((D_EXPERT, D_MODEL), BF16)],
        ),
        out_shape=jax.ShapeDtypeStruct((n_rows, SUBLANES, LANES), F32),
        compiler_params=pltpu.CompilerParams(dimension_semantics=("arbitrary",),
                                             vmem_limit_bytes=56 * 1024 * 1024),
        name="experts",
    )(block_e, first, n_used, xr, wgu, bgu, wd, bd)


def _final_kernel(dcur_ref, dnxt_ref, x1_ref, g_ref, nf_ref, yr_hbm, op_ref, os_ref, buf, sem,
                  *, n_prompt_blocks, tm):
    i = pl.program_id(0)
    n = pl.num_programs(0)
    slot = i % 2

    def row_copy(row, s, k, t):
        dst = buf.at[s, k, pl.ds(pl.multiple_of(t * SUBLANES, SUBLANES), SUBLANES)]
        return pltpu.make_async_copy(yr_hbm.at[row], dst, sem.at[s])

    def issue(dest_ref, s):
        def body(g, c):
            rows = [dest_ref[g * ISSUE_GROUP + u] for u in range(ISSUE_GROUP)]
            for u in range(ISSUE_GROUP):
                t = g * (ISSUE_GROUP // TOP_K) + u // TOP_K
                row_copy(rows[u], s, u % TOP_K, t).start(priority=u % 2)
            return c
        lax.fori_loop(0, tm * TOP_K // ISSUE_GROUP, body, 0)

    @pl.when(i == 0)
    def _():
        issue(dcur_ref, 0)

    @pl.when(i + 1 < n)
    def _():
        issue(dnxt_ref, 1 - slot)

    def drain(t, c):
        for k in range(TOP_K):
            row_copy(0, slot, k, t).wait()
        return c
    lax.fori_loop(0, tm, drain, 0, unroll=8)

    g = g_ref[...]
    parts = []
    for c in range(D_MODEL // LANES):
        moe = buf[slot, 0, pl.ds(c, tm, stride=SUBLANES), :] * g[:, 0:1]
        for k in range(1, TOP_K):
            moe = moe + buf[slot, k, pl.ds(c, tm, stride=SUBLANES), :] * g[:, k:k + 1]
        parts.append(x1_ref[:, LANES * c:LANES * (c + 1)] + moe)
    x2 = jnp.concatenate(parts, axis=1)
    ms = jnp.mean(x2 * x2, axis=-1, keepdims=True)
    y = x2 * lax.rsqrt(ms + EPS) * nf_ref[...]

    @pl.when(i < n_prompt_blocks)
    def _():
        op_ref[...] = y

    @pl.when(i >= n_prompt_blocks)
    def _():
        os_ref[...] = y


def _final(dest_flat, x1, yr, gates, nf, t_prompt, tm):
    t_all = x1.shape[0]
    npb = t_prompt // tm
    nsteps = t_all // tm
    return pl.pallas_call(
        functools.partial(_final_kernel, n_prompt_blocks=npb, tm=tm),
        grid=(nsteps,),
        in_specs=[
            pl.BlockSpec((tm * TOP_K,), lambda i: (i,), memory_space=pltpu.SMEM),
            pl.BlockSpec((tm * TOP_K,), lambda i: (jnp.minimum(i + 1, nsteps - 1),), memory_space=pltpu.SMEM),
            pl.BlockSpec((tm, D_MODEL), lambda i: (i, 0)),
            pl.BlockSpec((tm, LANES), lambda i: (i, 0)),
            pl.BlockSpec((1, D_MODEL), lambda i: (0, 0)),
            pl.BlockSpec(memory_space=pl.ANY),
        ],
        out_specs=[
            pl.BlockSpec((tm, D_MODEL), lambda i: (jnp.minimum(i, npb - 1), 0)),
            pl.BlockSpec((tm, D_MODEL), lambda i: (jnp.maximum(i - npb, 0), 0)),
        ],
        out_shape=[
            jax.ShapeDtypeStruct((t_prompt, D_MODEL), F32),
            jax.ShapeDtypeStruct((t_all - t_prompt, D_MODEL), F32),
        ],
        scratch_shapes=[pltpu.VMEM((2, TOP_K, tm * SUBLANES, LANES), F32), pltpu.SemaphoreType.DMA((2,))],
        compiler_params=_cparams(("arbitrary",)),
        name="final",
    )(dest_flat, dest_flat, x1, gates, nf, yr)


def _block_diag4(w):
    w4 = w.reshape(4, 4, RNN_BLOCK, RNN_BLOCK)
    eye = jnp.eye(4, dtype=w.dtype)
    return jnp.einsum('ghij,hk->ghikj', w4, eye).reshape(4, 256, 256)


def _route_tables(route_i, cnt, t_all):
    rank = route_i[:, 0:TOP_K]
    expert = route_i[:, TOP_K:2 * TOP_K]
    counts = cnt[0, :N_EXPERTS].astype(jnp.int32)
    padded = (counts + MOE_BLOCK - 1) // MOE_BLOCK * MOE_BLOCK
    pend = jnp.cumsum(padded)
    pstart = pend - padded
    dest = jnp.take(pstart, expert) + rank
    n_blocks = _moe_blocks(t_all)
    n_used = (pend[-1] // MOE_BLOCK).astype(jnp.int32)
    blk = jnp.minimum(jnp.arange(n_blocks, dtype=jnp.int32), n_used - 1) * MOE_BLOCK
    block_e = jnp.minimum(jnp.sum((pend[None, :] <= blk[:, None]).astype(jnp.int32), axis=1), N_EXPERTS - 1)
    first = jnp.concatenate([jnp.ones((1,), jnp.int32), (block_e[1:] != block_e[:-1]).astype(jnp.int32)])
    tail = jnp.maximum(pend - MOE_BLOCK, 0)
    has = (counts > 0).astype(jnp.int32)
    return dest.reshape(-1), tail, has, block_e, first, n_used.reshape(1)


def _moe_blocks(t_all):
    return pl.cdiv(t_all * TOP_K, MOE_BLOCK) + N_EXPERTS


def kernel(x_prompt, x_sample, state_lru_conv, state_lru_h, state_ssd_conv, state_ssd, norm_mix, w_in, conv_lru_w, conv_lru_b, w_rg, b_rg, w_ig, b_ig, lam, w_proj_a, conv_ssd_w, conv_ssd_b, dt_bias, a_log, d_skip, ssd_norm, w_proj_b, w_out, norm_ffn, w_router, b_router, w_gate_up, b_gate_up, w_down, b_down, norm_final):
    nbp, seq, _ = x_prompt.shape
    nbs = x_sample.shape[0]
    t_p = nbp * seq
    t_all = t_p + nbs
    l = 0

    wi = w_in[l]
    s = (0, 1024, 2048, 4096, 7168, 7200, 8224, 9248)
    w_main = jnp.concatenate([wi[:, s[2]:s[3]], wi[:, s[0]:s[1]], wi[:, s[3]:s[4]], wi[:, s[1]:s[2]],
                              wi[:, s[5]:s[6]], wi[:, s[6]:s[7]]], axis=1).astype(BF16)
    w_dt = jnp.pad(wi[:, s[4]:s[5]], ((0, 0), (0, LANES - SSD_HEADS))).astype(BF16)
    g_mix = norm_mix[l].reshape(1, D_MODEL)
    wg = jnp.concatenate([_block_diag4(w_rg[l]), _block_diag4(w_ig[l])], axis=2).astype(BF16)
    bg = jnp.concatenate([b_rg[l].reshape(4, 1, 256), b_ig[l].reshape(4, 1, 256)], axis=2)
    lam_r = lam[l].reshape(1, D_RNN)
    lcw, lcb = conv_lru_w[l], conv_lru_b[l].reshape(1, D_RNN)
    scw, scb = conv_ssd_w[l], conv_ssd_b[l].reshape(1, SSD_CONV_DIM)
    dtb = jnp.pad(dt_bias[l], (0, LANES - SSD_HEADS)).reshape(1, LANES)
    a_row = jnp.pad(-jnp.exp(a_log[l]), (0, LANES - SSD_HEADS)).reshape(1, LANES)
    dsk = jnp.repeat(d_skip[l], SSD_HEAD_DIM).reshape(1, D_INNER)
    nrm = ssd_norm[l].reshape(1, D_INNER)
    e_mat = (jnp.arange(LANES)[:, None] == (jnp.arange(D_INNER) // SSD_HEAD_DIM)[None, :]).astype(F32)
    wa, wb, wo = w_proj_a[l].astype(BF16), w_proj_b[l].astype(BF16), w_out[l].astype(BF16)
    nf = norm_ffn[l].reshape(1, D_MODEL)
    wr = jnp.pad(w_router[l], ((0, 0), (0, LANES - N_EXPERTS))).astype(BF16)
    br = jnp.pad(b_router[l], (0, LANES - N_EXPERTS)).reshape(1, LANES)
    bgu =b_gate_up[l].reshape(N_EXPERTS, 1, 2 * D_EXPERT)
    bd = b_down[l].reshape(N_EXPERTS, 1, D_MODEL)

    xp = x_prompt.reshape(t_p, D_MODEL)
    proj_p, dt_p = _inproj(xp, g_mix, w_main, w_dt, tm=2048)
    ga_p, p_lc, p_lh = _lru_prompt(proj_p, lcw, lcb, wg, bg, lam_r, nbp, seq, tl=512)
    yb_p, p_sc, p_ss = _ssd_prompt(proj_p, dt_p, scw, scb, dtb, a_row, dsk, nrm, e_mat, nbp, seq)

    xs_in = x_sample.reshape(nbs, D_MODEL)
    proj_s, dt_s = _inproj(xs_in, g_mix, w_main, w_dt, tm=nbs)
    (ga_s, s_lc, s_lh, s_sc, xdt_t, da, b_s, c_s, xs_s) = _sample_pre(
        proj_s, dt_s, state_lru_conv[l].reshape(nbs, 3 * D_RNN), state_lru_h[l],
        state_ssd_conv[l].reshape(nbs, 3 * SSD_CONV_DIM), lcw, lcb, wg, bg, lam_r, scw, scb, dtb, a_row, e_mat)
    s_ss, y_t = _sample_state(da[:, :SSD_HEADS].reshape(-1), state_ssd[l].reshape(nbs, D_INNER, D_STATE),
                              xdt_t, b_s.reshape(nbs, 1, -1), c_s.reshape(nbs, 1, -1))
    yb_s = _sample_post(y_t, xs_s, proj_s, dsk, nrm)

    cnt0 = jnp.zeros((1, LANES), F32)
    *prev, cnt_p = _merge(ga_p, yb_p, proj_p, xp, wa, wb, wo, nf, wr, br, cnt0, tm=512, t_all=t_all, row0=0,
                          prev=None)
    x1, u2, route_f, route_i, cnt = _merge(ga_s, yb_s, proj_s, xs_in, wa, wb, wo, nf, wr, br, cnt_p, tm=nbs,
                                           t_all=t_all, row0=t_p, prev=prev)

    dest, tail, has, block_e, first, n_used = _route_tables(route_i, cnt, t_all)
    xr = _dispatch(dest, tail, has, n_used, u2, _moe_blocks(t_all), tm=nbs)
    yr = _experts(block_e, first, n_used, xr, w_gate_up[l], bgu, w_down[l], bd)
    y_p, y_s = _final(dest, x1, yr, route_f, norm_final.reshape(1, D_MODEL), t_p, tm=nbs)

    return (y_p.reshape(nbp, seq, D_MODEL), y_s.reshape(nbs, 1, D_MODEL),
            p_lc[None], p_lh.reshape(1, nbp, D_RNN), p_sc[None],
            p_ss.reshape(1, nbp, SSD_HEADS, SSD_HEAD_DIM, D_STATE),
            s_lc.reshape(1, nbs, CONV_W - 1, D_RNN), s_lh[None],
            s_sc.reshape(1, nbs, CONV_W - 1, SSD_CONV_DIM),
            s_ss.reshape(1, nbs, SSD_HEADS, SSD_HEAD_DIM, D_STATE))
```

```python
import functools

import jax
import jax.numpy as jnp
from jax import lax
from jax.experimental import pallas as pl
from jax.experimental.pallas import tpu as pltpu

F32 = jnp.float32
BF16 = jnp.bfloat16

D_MODEL = 1024
D_RNN = 1024
RNN_HEADS = 16
RNN_BLOCK = 64
CONV_W = 4
RG_C = 8.0
D_INNER = 2048
SSD_HEAD_DIM = 64
SSD_HEADS = 32
SSD_GROUPS = 4
D_STATE = 128
SSD_CONV_DIM = D_INNER + 2 * SSD_GROUPS * D_STATE
CHUNK = 128
N_EXPERTS = 32
TOP_K = 4
D_EXPERT = 1024
SWIGLU_LIMIT = 7.0
SWIGLU_ALPHA = 1.702
EPS = 1e-6

LANES = 128
SUBLANES = 8
N_MAIN = 9 * 1024
COL_Z, COL_LRU, COL_XBC, COL_GATE, COL_GA, COL_GB = 0, 2, 1, 6, 7, 8
MOE_BLOCK = 384
SAMPLE_STATE_ROWS = 2
ISSUE_GROUP = 32
VMEM_LIMIT = 48 * 1024 * 1024


def _cparams(sem):
    return pltpu.CompilerParams(dimension_semantics=sem, vmem_limit_bytes=VMEM_LIMIT)


def _sigmoid(x):
    return jax.nn.sigmoid(x)


def _softplus(x):
    return jnp.maximum(x, 0.0) + jnp.log1p(jnp.exp(-jnp.abs(x)))


def _gelu_tanh(x):
    return 0.5 * x * (1.0 + jnp.tanh(0.7978845608028654 * (x + 0.044715 * (x * x * x))))


def _dot(a, b):
    return jnp.dot(a, b, preferred_element_type=F32)


def _dot_exact(a, b):
    return jnp.dot(a, b, preferred_element_type=F32, precision=lax.Precision.HIGHEST)


def _inproj_kernel(x_ref, g_ref, w_ref, wdt_ref, o_ref, dt_ref, u_ref):
    @pl.when(pl.program_id(1) == 0)
    def _():
        x = x_ref[...]
        ms = jnp.mean(x * x, axis=-1, keepdims=True)
        u = (x * lax.rsqrt(ms + EPS) * g_ref[...]).astype(BF16)
        u_ref[...] = u
        dt_ref[...] = _dot(u, wdt_ref[...])

    o_ref[...] = _dot(u_ref[...], w_ref[...])


def _inproj(x, g, w_main, w_dt, tm):
    t = x.shape[0]
    return pl.pallas_call(
        _inproj_kernel,
        grid=(t // tm, N_MAIN // 1024),
        in_specs=[
            pl.BlockSpec((tm, D_MODEL), lambda i, j: (i, 0)),
            pl.BlockSpec((1, D_MODEL), lambda i, j: (0, 0)),
            pl.BlockSpec((D_MODEL, 1024), lambda i, j: (0, j)),
            pl.BlockSpec((D_MODEL, LANES), lambda i, j: (0, 0)),
        ],
        out_specs=[
            pl.BlockSpec((tm, 1024), lambda i, j: (i, j)),
            pl.BlockSpec((tm, LANES), lambda i, j: (i, 0)),
        ],
        out_shape=[jax.ShapeDtypeStruct((t, N_MAIN), F32), jax.ShapeDtypeStruct((t, LANES), F32)],
        scratch_shapes=[pltpu.VMEM((tm, D_MODEL), BF16)],
        compiler_params=pltpu.CompilerParams(dimension_semantics=("parallel", "arbitrary"),
                                             vmem_limit_bytes=56 * 1024 * 1024),
        name="inproj",
    )(x, g, w_main, w_dt)


def _lru_gates(xc, wg_ref, bg_ref, lam):
    xcb = xc.astype(BF16)
    sp = _softplus(-lam)
    a_parts, b_parts = [], []
    for g in range(4):
        sl = slice(256 * g, 256 * (g + 1))
        pre = _dot(xcb[:, sl], wg_ref[g]) + bg_ref[g]
        r = _sigmoid(pre[:, :256])
        i = _sigmoid(pre[:, 256:])
        log_a = (-RG_C * r) * sp[:, sl]
        a_parts.append(jnp.exp(log_a))
        th = jnp.tanh(log_a)
        mult = jnp.sqrt(-2.0 * th / (1.0 - th))
        b_parts.append(mult * (i * xc[:, sl]))
    return jnp.concatenate(a_parts, axis=1), jnp.concatenate(b_parts, axis=1)


def _lru_kernel(x_ref, gate_ref, cw_ref, cb_ref, wg_ref, bg_ref, lam_ref,
                out_ref, conv_ref, h_ref, xx_ref, a_ref, b_ref, hc_ref, *, tl):
    @pl.when(pl.program_id(1) == 0)
    def _():
        xx_ref[0:SUBLANES, :] = jnp.zeros((SUBLANES, D_RNN), F32)
        hc_ref[...] = jnp.zeros((SUBLANES, D_RNN), F32)

    x = x_ref[...]
    xx_ref[SUBLANES:SUBLANES + tl, :] = x
    cw = cw_ref[...]
    xc = (cw[3:4] * x + cw[2:3] * xx_ref[7:7 + tl, :] + cw[1:2] * xx_ref[6:6 + tl, :]
          + cw[0:1] * xx_ref[5:5 + tl, :] + cb_ref[...])
    xx_ref[0:SUBLANES, :] = xx_ref[tl:tl + SUBLANES, :]
    conv_ref[0] = x_ref[tl - 3:tl, :]

    a, b = _lru_gates(xc, wg_ref, bg_ref, lam_ref[...])
    nt = tl // SUBLANES
    a = a.reshape(nt, SUBLANES, D_RNN)
    b = b.reshape(nt, SUBLANES, D_RNN)
    rows = lax.broadcasted_iota(jnp.int32, (nt, SUBLANES, D_RNN), 1)
    for d in (1, 2, 4):
        m = rows >= d
        b = jnp.where(m, a * pltpu.roll(b, d, 1) + b, b)
        a = jnp.where(m, a * pltpu.roll(a, d, 1), a)
    a_ref[...] = a
    b_ref[...] = b

    def carry(k, hprev):
        h = a_ref[k] * hprev + b_ref[k]
        b_ref[k] = h
        return jnp.broadcast_to(h[SUBLANES - 1:SUBLANES, :], (SUBLANES, D_RNN))

    hlast = lax.fori_loop(0, nt, carry, hc_ref[...])
    hc_ref[...] = hlast
    h_ref[0] = hlast[0:1, :]
    h_all = b_ref[...].reshape(tl, D_RNN)
    out_ref[...] = (_gelu_tanh(gate_ref[...]) * h_all).astype(BF16)


def _lru_prompt(proj, cw, cb, wg, bg, lam, nb, seq, tl):
    nl = seq // tl
    return pl.pallas_call(
        functools.partial(_lru_kernel, tl=tl),
        grid=(nb, nl),
        in_specs=[
            pl.BlockSpec((tl, 1024), lambda b, l: (b * nl + l, COL_LRU)),
            pl.BlockSpec((tl, 1024), lambda b, l: (b * nl + l, COL_GATE)),
            pl.BlockSpec((CONV_W, D_RNN), lambda b, l: (0, 0)),
            pl.BlockSpec((1, D_RNN), lambda b, l: (0, 0)),
            pl.BlockSpec((4, 256, 512), lambda b, l: (0, 0, 0)),
            pl.BlockSpec((4, 1, 512), lambda b, l: (0, 0, 0)),
            pl.BlockSpec((1, D_RNN), lambda b, l: (0, 0)),
        ],
        out_specs=[
            pl.BlockSpec((tl, D_RNN), lambda b, l: (b * nl + l, 0)),
            pl.BlockSpec((1, CONV_W - 1, D_RNN), lambda b, l: (b, 0, 0)),
            pl.BlockSpec((1, 1, D_RNN), lambda b, l: (b, 0, 0)),
        ],
        out_shape=[
            jax.ShapeDtypeStruct((nb * seq, D_RNN), BF16),
            jax.ShapeDtypeStruct((nb, CONV_W - 1, D_RNN), F32),
            jax.ShapeDtypeStruct((nb, 1, D_RNN), F32),
        ],
        scratch_shapes=[
            pltpu.VMEM((tl + SUBLANES, D_RNN), F32),
            pltpu.VMEM((tl // SUBLANES, SUBLANES, D_RNN), F32),
            pltpu.VMEM((tl // SUBLANES, SUBLANES, D_RNN), F32),
            pltpu.VMEM((SUBLANES, D_RNN), F32),
        ],
        compiler_params=_cparams(("parallel", "arbitrary")),
        name="lru_prompt",
    )(proj, proj, cw, cb, wg, bg, lam)


def _group_norm_gate(y, z, nrm):
    y = y * (z * _sigmoid(z))
    gw = D_INNER // SSD_GROUPS
    parts = []
    for g in range(SSD_GROUPS):
        yg = y[:, gw * g:gw * (g + 1)]
        ms = jnp.mean(yg * yg, axis=-1, keepdims=True)
        parts.append(yg * lax.rsqrt(ms + EPS))
    return jnp.concatenate(parts, axis=1) * nrm


def _ssd_kernel(xbc_ref, z_ref, dt_ref, cw_ref, cb_ref, dtb_ref, a_ref, dsk_ref, nrm_ref, e_ref,
                y_ref, conv_ref, st_ref, xx_ref, stt_ref, yacc_ref):
    q = CHUNK

    @pl.when(pl.program_id(1) == 0)
    def _():
        xx_ref[0:SUBLANES, :] = jnp.zeros((SUBLANES, SSD_CONV_DIM), F32)
        stt_ref[...] = jnp.zeros((D_STATE, D_INNER), F32)

    xbc = xbc_ref[...]
    xx_ref[SUBLANES:SUBLANES + q, :] = xbc
    cw = cw_ref[...]
    conv = (cw[3:4] * xbc + cw[2:3] * xx_ref[7:7 + q, :] + cw[1:2] * xx_ref[6:6 + q, :]
            + cw[0:1] * xx_ref[5:5 + q, :] + cb_ref[...])
    xx_ref[0:SUBLANES, :] = xx_ref[q:q + SUBLANES, :]
    conv_ref[0] = xbc_ref[q - 3:q, :]
    act = conv * _sigmoid(conv)
    xs = act[:, :D_INNER]
    bm = act[:, D_INNER:D_INNER + SSD_GROUPS * D_STATE]
    cm = act[:, D_INNER + SSD_GROUPS * D_STATE:]

    dt = _softplus(dt_ref[...] + dtb_ref[...])
    dta = dt * a_ref[...]
    iq = lax.broadcasted_iota(jnp.int32, (q, q), 0)
    ik = lax.broadcasted_iota(jnp.int32, (q, q), 1)
    causal = iq >= ik
    a_cum = _dot_exact(causal.astype(F32), dta)
    a_cum_t = a_cum.T
    dt_t = dt.T
    e_cum = jnp.exp(a_cum)
    w_t = jnp.exp(a_cum_t[:, q - 1:q] - a_cum_t) * dt_t
    a_last = jnp.broadcast_to(a_cum[q - 1:q, :], (SUBLANES, LANES))
    da_e = jnp.exp(_dot_exact(a_last, e_ref[...])[0:1, :])
    lo = ik < SSD_HEAD_DIM

    for g in range(SSD_GROUPS):
        bg = bm[:, D_STATE * g:D_STATE * (g + 1)]
        cg = cm[:, D_STATE * g:D_STATE * (g + 1)]
        cb = lax.dot_general(cg.astype(BF16), bg.astype(BF16), (((1,), (1,)), ((), ())),
                             preferred_element_type=F32)
        bg_t = bg.T
        for pp in range(4):
            j = 4 * g + pp
            ms, cs, bw = [], [], []
            for s in range(2):
                h = 2 * j + s
                col = a_cum[:, h:h + 1]
                row = a_cum_t[h:h + 1, :]
                dec = jnp.exp(jnp.where(causal, col - row, -jnp.inf))
                ms.append(cb * dec * dt_t[h:h + 1, :])
                cs.append(cg * e_cum[:, h:h + 1])
                bw.append(bg_t * w_t[h:h + 1, :])
            sl = slice(LANES * j, LANES * (j + 1))
            xp = xs[:, sl]
            rhs_x = jnp.concatenate([jnp.where(lo, xp, 0.0), jnp.where(lo, 0.0, xp)], axis=0).astype(BF16)
            stp = stt_ref[:, sl]
            rhs_s = jnp.concatenate([jnp.where(lo, stp, 0.0), jnp.where(lo, 0.0, stp)], axis=0).astype(BF16)
            l_m = jnp.concatenate(ms, axis=1).astype(BF16)
            l_c = jnp.concatenate(cs, axis=1).astype(BF16)
            l_b = jnp.concatenate(bw, axis=1).astype(BF16)
            yacc_ref[:, sl] = _dot(l_m, rhs_x) + _dot(l_c, rhs_s)
            stt_ref[:, sl] = stp * da_e[:, sl] + _dot(l_b, rhs_x)

    y = yacc_ref[...] + dsk_ref[...] * xs
    y_ref[...] = _group_norm_gate(y, z_ref[...], nrm_ref[...]).astype(BF16)

    @pl.when(pl.program_id(1) == pl.num_programs(1) - 1)
    def _():
        st_ref[0] = stt_ref[...].T


def _ssd_prompt(proj, dt, cw, cb, dtb, a_row, dsk, nrm, e_mat, nb, seq):
    nc = seq // CHUNK
    return pl.pallas_call(
        _ssd_kernel,
        grid=(nb, nc),
        in_specs=[
            pl.BlockSpec((CHUNK, SSD_CONV_DIM), lambda b, c: (b * nc + c, COL_XBC)),
            pl.BlockSpec((CHUNK, D_INNER), lambda b, c: (b * nc + c, COL_Z)),
            pl.BlockSpec((CHUNK, LANES), lambda b, c: (b * nc + c, 0)),
            pl.BlockSpec((CONV_W, SSD_CONV_DIM), lambda b, c: (0, 0)),
            pl.BlockSpec((1, SSD_CONV_DIM), lambda b, c: (0, 0)),
            pl.BlockSpec((1, LANES), lambda b, c: (0, 0)),
            pl.BlockSpec((1, LANES), lambda b, c: (0, 0)),
            pl.BlockSpec((1, D_INNER), lambda b, c: (0, 0)),
            pl.BlockSpec((1, D_INNER), lambda b, c: (0, 0)),
            pl.BlockSpec((LANES, D_INNER), lambda b, c: (0, 0)),
        ],
        out_specs=[
            pl.BlockSpec((CHUNK, D_INNER), lambda b, c: (b * nc + c, 0)),
            pl.BlockSpec((1, CONV_W - 1, SSD_CONV_DIM), lambda b, c: (b, 0, 0)),
            pl.BlockSpec((1, D_INNER, D_STATE), lambda b, c: (b, 0, 0)),
        ],
        out_shape=[
            jax.ShapeDtypeStruct((nb * seq, D_INNER), BF16),
            jax.ShapeDtypeStruct((nb, CONV_W - 1, SSD_CONV_DIM), F32),
            jax.ShapeDtypeStruct((nb, D_INNER, D_STATE), F32),
        ],
        scratch_shapes=[
            pltpu.VMEM((CHUNK + SUBLANES, SSD_CONV_DIM), F32),
            pltpu.VMEM((D_STATE, D_INNER), F32),
            pltpu.VMEM((CHUNK, D_INNER), F32),
        ],
        compiler_params=_cparams(("parallel", "arbitrary")),
        name="ssd_prompt",
    )(proj, proj, dt, cw, cb, dtb, a_row, dsk, nrm, e_mat)


def _sample_pre_kernel(proj_ref, dt_ref, lconv_ref, h0_ref, sconv_ref,
                       lcw_ref, lcb_ref, wg_ref, bg_ref, lam_ref,
                       scw_ref, scb_ref, dtb_ref, a_ref, e_ref,
                       ga_ref, lconv_o, h_o, sconv_o, xdt_t_o, da_o, b_o, c_o, xs_o):
    x = proj_ref[:, 2048:3072]
    gate = proj_ref[:, 6144:7168]
    cw = lcw_ref[...]
    c0 = lconv_ref[:, 0:1024]
    c1 = lconv_ref[:, 1024:2048]
    c2 = lconv_ref[:, 2048:3072]
    xc = cw[3:4] * x + cw[2:3] * c2 + cw[1:2] * c1 + cw[0:1] * c0 + lcb_ref[...]
    lconv_o[:, 0:1024] = c1
    lconv_o[:, 1024:2048] = c2
    lconv_o[:, 2048:3072] = x
    a, bt = _lru_gates(xc, wg_ref, bg_ref, lam_ref[...])
    h = a * h0_ref[...] + bt
    h_o[...] = h
    ga_ref[...] = (_gelu_tanh(gate) * h).astype(BF16)

    xbc = proj_ref[:, 3072:6144]
    sw = scw_ref[...]
    w = SSD_CONV_DIM
    s0 = sconv_ref[:, 0:w]
    s1 = sconv_ref[:, w:2 * w]
    s2 = sconv_ref[:, 2 * w:3 * w]
    conv = sw[3:4] * xbc + sw[2:3] * s2 + sw[1:2] * s1 + sw[0:1] * s0 + scb_ref[...]
    sconv_o[:, 0:w] = s1
    sconv_o[:, w:2 * w] = s2
    sconv_o[:, 2 * w:3 * w] = xbc
    act = conv * _sigmoid(conv)
    xs = act[:, :D_INNER]
    xs_o[...] = xs
    b_o[...] = act[:, D_INNER:D_INNER + SSD_GROUPS * D_STATE]
    c_o[...] = act[:, D_INNER + SSD_GROUPS * D_STATE:]
    dt = _softplus(dt_ref[...] + dtb_ref[...])
    da_o[...] = jnp.exp(dt * a_ref[...])
    dt_e = _dot_exact(dt, e_ref[...])
    xdt_t_o[...] = (xs * dt_e).T


def _sample_pre(proj, dt, lconv, h0, sconv, lcw, lcb, wg, bg, lam, scw, scb, dtb, a_row, e_mat):
    nb = proj.shape[0]
    out_shape = [
        jax.ShapeDtypeStruct((nb, D_RNN), BF16),
        jax.ShapeDtypeStruct((nb, 3 * D_RNN), F32),
        jax.ShapeDtypeStruct((nb, D_RNN), F32),
        jax.ShapeDtypeStruct((nb, 3 * SSD_CONV_DIM), F32),
        jax.ShapeDtypeStruct((D_INNER, nb), F32),
        jax.ShapeDtypeStruct((nb, LANES), F32),
        jax.ShapeDtypeStruct((nb, SSD_GROUPS * D_STATE), F32),
        jax.ShapeDtypeStruct((nb, SSD_GROUPS * D_STATE), F32),
        jax.ShapeDtypeStruct((nb, D_INNER), F32),
    ]
    return pl.pallas_call(
        _sample_pre_kernel,
        out_shape=out_shape,
        compiler_params=pltpu.CompilerParams(vmem_limit_bytes=VMEM_LIMIT),
        name="sample_pre",
    )(proj, dt, lconv, h0, sconv, lcw, lcb, wg, bg, lam, scw, scb, dtb, a_row, e_mat)


def _sample_state_kernel(da_ref, s0_ref, xdt_t_ref, b_ref, c_ref, s1_ref, y_t_ref):
    nb = xdt_t_ref.shape[1]

    @pl.when(pl.program_id(0) == 0)
    def _():
        y_t_ref[...] = jnp.zeros(y_t_ref.shape, F32)

    lane = lax.broadcasted_iota(jnp.int32, (D_INNER, nb), 1)
    for j in range(SAMPLE_STATE_ROWS):
        b = pl.program_id(0) * SAMPLE_STATE_ROWS + j
        sel = lane == b
        xcol = jnp.sum(jnp.where(sel, xdt_t_ref[...], 0.0), axis=1, keepdims=True)
        ycols = []
        for g in range(SSD_GROUPS):
            brow = b_ref[j, :, D_STATE * g:D_STATE * (g + 1)]
            crow = c_ref[j, :, D_STATE * g:D_STATE * (g + 1)]
            for hh in range(SSD_HEADS // SSD_GROUPS):
                h = (SSD_HEADS // SSD_GROUPS) * g + hh
                sl = slice(SSD_HEAD_DIM * h, SSD_HEAD_DIM * (h + 1))
                s1 = s0_ref[j, sl, :] * da_ref[b * SSD_HEADS + h] + xcol[sl, :] * brow
                s1_ref[j, sl, :] = s1
                ycols.append(jnp.sum(s1 * crow, axis=1, keepdims=True))
        ycol = jnp.concatenate(ycols, axis=0)
        y_t_ref[...] = jnp.where(sel, ycol, y_t_ref[...])


def _sample_state(da_flat, s0, xdt_t, bmat, cmat):
    nb = s0.shape[0]
    return pl.pallas_call(
        _sample_state_kernel,
        grid_spec=pltpu.PrefetchScalarGridSpec(
            num_scalar_prefetch=1,
            grid=(nb // SAMPLE_STATE_ROWS,),
            in_specs=[
                pl.BlockSpec((SAMPLE_STATE_ROWS, D_INNER, D_STATE), lambda b, da: (b, 0, 0)),
                pl.BlockSpec((D_INNER, nb), lambda b, da: (0, 0)),
                pl.BlockSpec((SAMPLE_STATE_ROWS, 1, SSD_GROUPS * D_STATE), lambda b, da: (b, 0, 0)),
                pl.BlockSpec((SAMPLE_STATE_ROWS, 1, SSD_GROUPS * D_STATE), lambda b, da: (b, 0, 0)),
            ],
            out_specs=[
                pl.BlockSpec((SAMPLE_STATE_ROWS, D_INNER, D_STATE), lambda b, da: (b, 0, 0)),
                pl.BlockSpec((D_INNER, nb), lambda b, da: (0, 0)),
            ],
        ),
        out_shape=[
            jax.ShapeDtypeStruct((nb, D_INNER, D_STATE), F32),
            jax.ShapeDtypeStruct((D_INNER, nb), F32),
        ],
        compiler_params=_cparams(("arbitrary",)),
        name="sample_state",
    )(da_flat, s0, xdt_t, bmat, cmat)


def _sample_post_kernel(y_t_ref, xs_ref, z_ref, dsk_ref, nrm_ref, y_ref):
    y = y_t_ref[...].T + dsk_ref[...] * xs_ref[...]
    y_ref[...] = _group_norm_gate(y, z_ref[:, 0:D_INNER], nrm_ref[...]).astype(BF16)


def _sample_post(y_t, xs, proj, dsk, nrm):
    nb = xs.shape[0]
    return pl.pallas_call(
        _sample_post_kernel,
        grid=(1,),
        in_specs=[
            pl.BlockSpec((D_INNER, nb), lambda i: (0, 0)),
            pl.BlockSpec((nb, D_INNER), lambda i: (0, 0)),
            pl.BlockSpec((nb, D_INNER), lambda i: (0, COL_Z)),
            pl.BlockSpec((1, D_INNER), lambda i: (0, 0)),
            pl.BlockSpec((1, D_INNER), lambda i: (0, 0)),
        ],
        out_specs=pl.BlockSpec((nb, D_INNER), lambda i: (0, 0)),
        out_shape=jax.ShapeDtypeStruct((nb, D_INNER), BF16),
        compiler_params=_cparams(("arbitrary",)),
        name="sample_post",
    )(y_t, xs, proj, dsk, nrm)


def _merge_kernel(ga_ref, yb_ref, gta_ref, gtb_ref, x_ref, wa_ref, wb_ref, wo_ref, nf_ref, wr_ref, br_ref,
                  cnt0_ref, *refs, n_real):
    x1_ref, u2_ref, rf_ref, ri_ref, cnt_ref, carry_ref = refs[-6:]

    @pl.when(pl.program_id(0) == 0)
    def _():
        carry_ref[...] = cnt0_ref[...]

    @pl.when(pl.program_id(0) >= n_real)
    def _():
        x1_ref[...] = jnp.zeros(x1_ref.shape, x1_ref.dtype)
        u2_ref[...] = jnp.zeros(u2_ref.shape, u2_ref.dtype)
        rf_ref[...] = jnp.zeros(rf_ref.shape, rf_ref.dtype)
        ri_ref[...] = jnp.zeros(ri_ref.shape, ri_ref.dtype)

    @pl.when(pl.program_id(0) < n_real)
    def _():
        _merge_body(ga_ref, yb_ref, gta_ref, gtb_ref, x_ref, wa_ref, wb_ref, wo_ref, nf_ref, wr_ref, br_ref,
                    x1_ref, u2_ref, rf_ref, ri_ref, carry_ref)

    cnt_ref[...] = carry_ref[...]


def _merge_body(ga_ref, yb_ref, gta_ref, gtb_ref, x_ref, wa_ref, wb_ref, wo_ref, nf_ref, wr_ref, br_ref,
                x1_ref, u2_ref, rf_ref, ri_ref, carry_ref):
    br_a = _dot(ga_ref[...], wa_ref[...])
    br_b = _dot(yb_ref[...], wb_ref[...])
    merged = _sigmoid(gta_ref[...]) * br_a + _sigmoid(gtb_ref[...]) * br_b
    x1 = x_ref[...] + _dot(merged.astype(BF16), wo_ref[...])
    x1_ref[...] = x1
    ms = jnp.mean(x1 * x1, axis=-1, keepdims=True)
    u2f = x1 * lax.rsqrt(ms + EPS) * nf_ref[...]
    u2_ref[...] = u2f.reshape(u2_ref.shape)
    u2 = u2f.astype(BF16)
    logits = _dot(u2, wr_ref[...]) + br_ref[...]

    tm = logits.shape[0]
    lane = lax.broadcasted_iota(jnp.int32, (tm, LANES), 1)
    lane_f = lane.astype(F32)
    v = jnp.where(lane < N_EXPERTS, logits, -jnp.inf)
    sel = jnp.zeros((tm, LANES), F32)
    hots, vals, idxs = [], [], []
    for _ in range(TOP_K):
        m = jnp.max(v, axis=1, keepdims=True)
        idx = jnp.min(jnp.where(v == m, lane_f, float(LANES)), axis=1, keepdims=True)
        hot = lane_f == idx
        hots.append(hot)
        vals.append(m)
        idxs.append(idx)
        v = jnp.where(hot, -jnp.inf, v)
        sel = sel + hot.astype(F32)
    exps = [jnp.exp(m - vals[0]) for m in vals]
    den = exps[0] + exps[1] + exps[2] + exps[3]
    ir = lax.broadcasted_iota(jnp.int32, (tm, tm), 0)
    ic = lax.broadcasted_iota(jnp.int32, (tm, tm), 1)
    before = _dot((ir > ic).astype(BF16), sel.astype(BF16)) + carry_ref[...]
    rf = jnp.zeros((tm, LANES), F32)
    ri = jnp.zeros((tm, LANES), F32)
    for k in range(TOP_K):
        rank = jnp.sum(jnp.where(hots[k], before, 0.0), axis=1, keepdims=True)
        rf = jnp.where(lane == k, exps[k] / den, rf)
        ri = jnp.where(lane == k, rank, jnp.where(lane == TOP_K + k, idxs[k], ri))
    rf_ref[...] = rf
    ri_ref[...] = ri.astype(jnp.int32)
    carry_ref[...] = carry_ref[...] + jnp.sum(sel, axis=0, keepdims=True)


def _merge(ga, yb, proj, x, wa, wb, wo, nf, wr, br, cnt0, tm, t_all, row0, prev):
    t = x.shape[0]
    blk0 = row0 // tm
    n_real = t // tm
    n_fill = 0 if prev is not None else pl.cdiv(t_all - row0 - t, tm)
    const = lambda i: (0, 0)
    rows = lambda i: jnp.minimum(i, n_real - 1)
    in_specs = [
        pl.BlockSpec((tm, D_RNN), lambda i: (rows(i), 0)),
        pl.BlockSpec((tm, D_INNER), lambda i: (rows(i), 0)),
        pl.BlockSpec((tm, 1024), lambda i: (rows(i), COL_GA)),
        pl.BlockSpec((tm, 1024), lambda i: (rows(i), COL_GB)),
        pl.BlockSpec((tm, D_MODEL), lambda i: (rows(i), 0)),
        pl.BlockSpec((D_RNN, D_MODEL), const),
        pl.BlockSpec((D_INNER, D_MODEL), const),
        pl.BlockSpec((D_MODEL, D_MODEL), const),
        pl.BlockSpec((1, D_MODEL), const),
        pl.BlockSpec((D_MODEL, LANES), const),
        pl.BlockSpec((1, LANES), const),
        pl.BlockSpec((1, LANES), const),
    ]
    args = [ga, yb, proj, proj, x, wa, wb, wo, nf, wr, br, cnt0]
    aliases = {}
    if prev is not None:
        in_specs += [pl.BlockSpec(memory_space=pl.ANY)] * 4
        aliases = {len(args) + k: k for k in range(4)}
        args += list(prev)
    return pl.pallas_call(
        functools.partial(_merge_kernel, n_real=n_real),
        grid=(n_real + n_fill,),
        in_specs=in_specs,
        out_specs=[
            pl.BlockSpec((tm, D_MODEL), lambda i: (blk0 + i, 0)),
            pl.BlockSpec((tm, SUBLANES, LANES), lambda i: (blk0 + i, 0, 0)),
            pl.BlockSpec((tm, LANES), lambda i: (blk0 + i, 0)),
            pl.BlockSpec((tm, LANES), lambda i: (blk0 + i, 0)),
            pl.BlockSpec((1, LANES), const),
        ],
        out_shape=[
            jax.ShapeDtypeStruct((t_all, D_MODEL), F32),
            jax.ShapeDtypeStruct((t_all, SUBLANES, LANES), F32),
            jax.ShapeDtypeStruct((t_all, LANES), F32),
            jax.ShapeDtypeStruct((t_all, LANES), jnp.int32),
            jax.ShapeDtypeStruct((1, LANES), F32),
        ],
        scratch_shapes=[pltpu.VMEM((1, LANES), F32)],
        input_output_aliases=aliases,
        compiler_params=_cparams(("arbitrary",)),
        name="merge_sample" if prev is not None else "merge_prompt",
    )(*args)


def _dispatch_kernel(dest_ref, tail_ref, has_ref, nu_ref, u2_ref, xr_hbm, stage, zero_ref, sem, zsem,
                     *, tm, n_blocks):
    i = pl.program_id(0)
    n = pl.num_programs(0)
    slot = i % 2

    def zero_copy(row0):
        return pltpu.make_async_copy(zero_ref, xr_hbm.at[pl.ds(row0, MOE_BLOCK)], zsem)

    @pl.when(i == 0)
    def _():
        zero_ref[...] = jnp.zeros(zero_ref.shape, F32)
        for e in range(N_EXPERTS):
            @pl.when(has_ref[e] == 1)
            def _():
                zero_copy(tail_ref[e]).start()

        def fill(j, c):
            zero_copy(j * MOE_BLOCK).start()
            return c
        lax.fori_loop(nu_ref[0], n_blocks, fill, 0)
        for e in range(N_EXPERTS):
            @pl.when(has_ref[e] == 1)
            def _():
                zero_copy(0).wait()

        def fill_wait(j, c):
            zero_copy(0).wait()
            return c
        lax.fori_loop(nu_ref[0], n_blocks, fill_wait, 0)

    def row_copy(tok, row, s):
        return pltpu.make_async_copy(stage.at[s, tok], xr_hbm.at[row], sem.at[s])

    stage[slot] = u2_ref[...]

    def issue(g, c):
        rows = [dest_ref[g * ISSUE_GROUP + u] for u in range(ISSUE_GROUP)]
        for u in range(ISSUE_GROUP):
            row_copy(g * (ISSUE_GROUP // TOP_K) + u // TOP_K, rows[u], slot).start(priority=u % 2)
        return c
    lax.fori_loop(0, tm * TOP_K // ISSUE_GROUP, issue, 0)

    def drain(s):
        def body(t, c):
            for k in range(TOP_K):
                row_copy(0, 0, s).wait()
            return c
        lax.fori_loop(0, tm, body, 0, unroll=8)

    @pl.when(i > 0)
    def _():
        drain(1 - slot)

    @pl.when(i == n - 1)
    def _():
        drain(slot)


def _dispatch(dest_flat, tail, has, n_used, u2t, n_blocks, tm):
    t_all = u2t.shape[0]
    smem = functools.partial(pl.BlockSpec, memory_space=pltpu.SMEM)
    return pl.pallas_call(
        functools.partial(_dispatch_kernel, tm=tm, n_blocks=n_blocks),
        grid=(t_all // tm,),
        in_specs=[
            smem((tm * TOP_K,), lambda i: (i,)),
            smem((N_EXPERTS,), lambda i: (0,)),
            smem((N_EXPERTS,), lambda i: (0,)),
            smem((1,), lambda i: (0,)),
            pl.BlockSpec((tm, SUBLANES, LANES), lambda i: (i, 0, 0)),
        ],
        out_specs=pl.BlockSpec(memory_space=pl.ANY),
        out_shape=jax.ShapeDtypeStruct((n_blocks * MOE_BLOCK, SUBLANES, LANES), F32),
        scratch_shapes=[pltpu.VMEM((2, tm, SUBLANES, LANES), F32), pltpu.VMEM((MOE_BLOCK, SUBLANES, LANES), F32),
                        pltpu.SemaphoreType.DMA((2,)), pltpu.SemaphoreType.DMA],
        compiler_params=_cparams(("arbitrary",)),
        name="dispatch",
    )(dest_flat, tail, has, n_used, u2t)


def _expert_kernel(be_ref, first_ref, nu_ref, x_ref, wgu_ref, bgu_ref, wd_ref, bd_ref, o_ref, wgu_b, wd_b):
    i = pl.program_id(0)

    @pl.when(first_ref[i] == 1)
    def _():
        wgu_b[...] = wgu_ref[0].astype(BF16)
        wd_b[...] = wd_ref[0].astype(BF16)

    @pl.when(i < nu_ref[0])
    def _():
        x = x_ref[...].reshape(MOE_BLOCK, D_MODEL).astype(BF16)
        gu = _dot(x, wgu_b[...]) + bgu_ref[0]
        gate = jnp.minimum(gu[:, :D_EXPERT], SWIGLU_LIMIT)
        up = jnp.clip(gu[:, D_EXPERT:], -SWIGLU_LIMIT, SWIGLU_LIMIT)
        act = (up + 1.0) * (gate * _sigmoid(SWIGLU_ALPHA * gate))
        y = _dot(act.astype(BF16), wd_b[...]) + bd_ref[0]
        o_ref[...] = y.reshape(o_ref.shape)

    @pl.when(i >= nu_ref[0])
    def _():
        o_ref[...] = jnp.zeros(o_ref.shape, F32)


def _experts(block_e, first, n_used, xr, wgu, bgu, wd, bd):
    n_rows = xr.shape[0]
    n_blocks = n_rows // MOE_BLOCK
    row_block = (MOE_BLOCK, SUBLANES, LANES)
    return pl.pallas_call(
        _expert_kernel,
        grid_spec=pltpu.PrefetchScalarGridSpec(
            num_scalar_prefetch=3,
            grid=(n_blocks,),
            in_specs=[
                pl.BlockSpec(row_block, lambda i, be, fi, nu: (jnp.minimum(i, nu[0] - 1), 0, 0)),
                pl.BlockSpec((1, D_MODEL, 2 * D_EXPERT), lambda i, be, fi, nu: (be[i], 0, 0)),
                pl.BlockSpec((1, 1, 2 * D_EXPERT), lambda i, be, fi, nu: (be[i], 0, 0)),
                pl.BlockSpec((1, D_EXPERT, D_MODEL), lambda i, be, fi, nu: (be[i], 0, 0)),
                pl.BlockSpec((1, 1, D_MODEL), lambda i, be, fi, nu: (be[i], 0, 0)),
            ],
            out_specs=pl.BlockSpec(row_block, lambda i, be, fi, nu: (i, 0, 0)),
            scratch_shapes=[pltpu.VMEM((D_MODEL, 2 * D_EXPERT), BF16), pltpu.VMEM((D_EXPERT, D_MODEL), BF16)],
        ),
        out_shape=jax.ShapeDtypeStruct((n_rows, SUBLANES, LANES), F32),
        compiler_params=pltpu.CompilerParams(dimension_semantics=("arbitrary",),
                                             vmem_limit_bytes=56 * 1024 * 1024),
        name="experts",
    )(block_e, first, n_used, xr, wgu, bgu, wd, bd)


def _final_kernel(dcur_ref, dnxt_ref, x1_ref, g_ref, nf_ref, yr_hbm, op_ref, os_ref, buf, sem,
                  *, n_prompt_blocks, tm):
    i = pl.program_id(0)
    n = pl.num_programs(0)
    slot = i % 2

    def row_copy(row, s, k, t):
        dst = buf.at[s, k, pl.ds(pl.multiple_of(t * SUBLANES, SUBLANES), SUBLANES)]
        return pltpu.make_async_copy(yr_hbm.at[row], dst, sem.at[s])

    def issue(dest_ref, s):
        def body(g, c):
            rows = [dest_ref[g * ISSUE_GROUP + u] for u in range(ISSUE_GROUP)]
            for u in range(ISSUE_GROUP):
                t = g * (ISSUE_GROUP // TOP_K) + u // TOP_K
                row_copy(rows[u], s, u % TOP_K, t).start(priority=u % 2)
            return c
        lax.fori_loop(0, tm * TOP_K // ISSUE_GROUP, body, 0)

    @pl.when(i == 0)
    def _():
        issue(dcur_ref, 0)

    @pl.when(i + 1 < n)
    def _():
        issue(dnxt_ref, 1 - slot)

    def drain(t, c):
        for k in range(TOP_K):
            row_copy(0, slot, k, t).wait()
        return c
    lax.fori_loop(0, tm, drain, 0, unroll=8)

    g = g_ref[...]
    parts = []
    for c in range(D_MODEL // LANES):
        moe = buf[slot, 0, pl.ds(c, tm, stride=SUBLANES), :] * g[:, 0:1]
        for k in range(1, TOP_K):
            moe = moe + buf[slot, k, pl.ds(c, tm, stride=SUBLANES), :] * g[:, k:k + 1]
        parts.append(x1_ref[:, LANES * c:LANES * (c + 1)] + moe)
    x2 = jnp.concatenate(parts, axis=1)
    ms = jnp.mean(x2 * x2, axis=-1, keepdims=True)
    y = x2 * lax.rsqrt(ms + EPS) * nf_ref[...]

    @pl.when(i < n_prompt_blocks)
    def _():
        op_ref[...] = y

    @pl.when(i >= n_prompt_blocks)
    def _():
        os_ref[...] = y


def _final(dest_flat, x1, yr, gates, nf, t_prompt, tm):
    t_all = x1.shape[0]
    npb = t_prompt // tm
    nsteps = t_all // tm
    return pl.pallas_call(
        functools.partial(_final_kernel, n_prompt_blocks=npb, tm=tm),
        grid=(nsteps,),
        in_specs=[
            pl.BlockSpec((tm * TOP_K,), lambda i: (i,), memory_space=pltpu.SMEM),
            pl.BlockSpec((tm * TOP_K,), lambda i: (jnp.minimum(i + 1, nsteps - 1),), memory_space=pltpu.SMEM),
            pl.BlockSpec((tm, D_MODEL), lambda i: (i, 0)),
            pl.BlockSpec((tm, LANES), lambda i: (i, 0)),
            pl.BlockSpec((1, D_MODEL), lambda i: (0, 0)),
            pl.BlockSpec(memory_space=pl.ANY),
        ],
        out_specs=[
            pl.BlockSpec((tm, D_MODEL), lambda i: (jnp.minimum(i, npb - 1), 0)),
            pl.BlockSpec((tm, D_MODEL), lambda i: (jnp.maximum(i - npb, 0), 0)),
        ],
        out_shape=[
            jax.ShapeDtypeStruct((t_prompt, D_MODEL), F32),
            jax.ShapeDtypeStruct((t_all - t_prompt, D_MODEL), F32),
        ],
        scratch_shapes=[pltpu.VMEM((2, TOP_K, tm * SUBLANES, LANES), F32), pltpu.SemaphoreType.DMA((2,))],
        compiler_params=_cparams(("arbitrary",)),
        name="final",
    )(dest_flat, dest_flat, x1, gates, nf, yr)


def _block_diag4(w):
    w4 = w.reshape(4, 4, RNN_BLOCK, RNN_BLOCK)
    eye = jnp.eye(4, dtype=w.dtype)
    return jnp.einsum('ghij,hk->ghikj', w4, eye).reshape(4, 256, 256)


def _route_tables(route_i, cnt, t_all):
    rank = route_i[:, 0:TOP_K]
    expert = route_i[:, TOP_K:2 * TOP_K]
    counts = cnt[0, :N_EXPERTS].astype(jnp.int32)
    padded = (counts + MOE_BLOCK - 1) // MOE_BLOCK * MOE_BLOCK
    pend = jnp.cumsum(padded)
    pstart = pend - padded
    dest = jnp.take(pstart, expert) + rank
    n_blocks = _moe_blocks(t_all)
    n_used = (pend[-1] // MOE_BLOCK).astype(jnp.int32)
    blk = jnp.minimum(jnp.arange(n_blocks, dtype=jnp.int32), n_used - 1) * MOE_BLOCK
    block_e = jnp.minimum(jnp.sum((pend[None, :] <= blk[:, None]).astype(jnp.int32), axis=1), N_EXPERTS - 1)
    first = jnp.concatenate([jnp.ones((1,), jnp.int32), (block_e[1:] != block_e[:-1]).astype(jnp.int32)])
    tail = jnp.maximum(pend - MOE_BLOCK, 0)
    has = (counts > 0).astype(jnp.int32)
    return dest.reshape(-1), tail, has, block_e, first, n_used.reshape(1)


def _moe_blocks(t_all):
    return pl.cdiv(t_all * TOP_K, MOE_BLOCK) + N_EXPERTS


def kernel(x_prompt, x_sample, state_lru_conv, state_lru_h, state_ssd_conv, state_ssd, norm_mix, w_in, conv_lru_w, conv_lru_b, w_rg, b_rg, w_ig, b_ig, lam, w_proj_a, conv_ssd_w, conv_ssd_b, dt_bias, a_log, d_skip, ssd_norm, w_proj_b, w_out, norm_ffn, w_router, b_router, w_gate_up, b_gate_up, w_down, b_down, norm_final):
    nbp, seq, _ = x_prompt.shape
    nbs = x_sample.shape[0]
    t_p = nbp * seq
    t_all = t_p + nbs
    l = 0

    wi = w_in[l]
    s = (0, 1024, 2048, 4096, 7168, 7200, 8224, 9248)
    w_main = jnp.concatenate([wi[:, s[2]:s[3]], wi[:, s[0]:s[1]], wi[:, s[3]:s[4]], wi[:, s[1]:s[2]],
                              wi[:, s[5]:s[6]], wi[:, s[6]:s[7]]], axis=1).astype(BF16)
    w_dt = jnp.pad(wi[:, s[4]:s[5]], ((0, 0), (0, LANES - SSD_HEADS))).astype(BF16)
    g_mix = norm_mix[l].reshape(1, D_MODEL)
    wg = jnp.concatenate([_block_diag4(w_rg[l]), _block_diag4(w_ig[l])], axis=2).astype(BF16)
    bg = jnp.concatenate([b_rg[l].reshape(4, 1, 256), b_ig[l].reshape(4, 1, 256)], axis=2)
    lam_r = lam[l].reshape(1, D_RNN)
    lcw, lcb = conv_lru_w[l], conv_lru_b[l].reshape(1, D_RNN)
    scw, scb = conv_ssd_w[l], conv_ssd_b[l].reshape(1, SSD_CONV_DIM)
    dtb = jnp.pad(dt_bias[l], (0, LANES - SSD_HEADS)).reshape(1, LANES)
    a_row = jnp.pad(-jnp.exp(a_log[l]), (0, LANES - SSD_HEADS)).reshape(1, LANES)
    dsk = jnp.repeat(d_skip[l], SSD_HEAD_DIM).reshape(1, D_INNER)
    nrm = ssd_norm[l].reshape(1, D_INNER)
    e_mat = (jnp.arange(LANES)[:, None] == (jnp.arange(D_INNER) // SSD_HEAD_DIM)[None, :]).astype(F32)
    wa, wb, wo = w_proj_a[l].astype(BF16), w_proj_b[l].astype(BF16), w_out[l].astype(BF16)
    nf = norm_ffn[l].reshape(1, D_MODEL)
    wr = jnp.pad(w_router[l], ((0, 0), (0, LANES - N_EXPERTS))).astype(BF16)
    br = jnp.pad(b_router[l], (0, LANES - N_EXPERTS)).reshape(1, LANES)
    bgu =b_gate_up[l].reshape(N_EXPERTS, 1, 2 * D_EXPERT)
    bd = b_down[l].reshape(N_EXPERTS, 1, D_MODEL)

    xp = x_prompt.reshape(t_p, D_MODEL)
    proj_p, dt_p = _inproj(xp, g_mix, w_main, w_dt, tm=2048)
    ga_p, p_lc, p_lh = _lru_prompt(proj_p, lcw, lcb, wg, bg, lam_r, nbp, seq, tl=512)
    yb_p, p_sc, p_ss = _ssd_prompt(proj_p, dt_p, scw, scb, dtb, a_row, dsk, nrm, e_mat, nbp, seq)

    xs_in = x_sample.reshape(nbs, D_MODEL)
    proj_s, dt_s = _inproj(xs_in, g_mix, w_main, w_dt, tm=nbs)
    (ga_s, s_lc, s_lh, s_sc, xdt_t, da, b_s, c_s, xs_s) = _sample_pre(
        proj_s, dt_s, state_lru_conv[l].reshape(nbs, 3 * D_RNN), state_lru_h[l],
        state_ssd_conv[l].reshape(nbs, 3 * SSD_CONV_DIM), lcw, lcb, wg, bg, lam_r, scw, scb, dtb, a_row, e_mat)
    s_ss, y_t = _sample_state(da[:, :SSD_HEADS].reshape(-1), state_ssd[l].reshape(nbs, D_INNER, D_STATE),
                              xdt_t, b_s.reshape(nbs, 1, -1), c_s.reshape(nbs, 1, -1))
    yb_s = _sample_post(y_t, xs_s, proj_s, dsk, nrm)

    cnt0 = jnp.zeros((1, LANES), F32)
    *prev, cnt_p = _merge(ga_p, yb_p, proj_p, xp, wa, wb, wo, nf, wr, br, cnt0, tm=512, t_all=t_all, row0=0,
                          prev=None)
    x1, u2, route_f, route_i, cnt = _merge(ga_s, yb_s, proj_s, xs_in, wa, wb, wo, nf, wr, br, cnt_p, tm=nbs,
                                           t_all=t_all, row0=t_p, prev=prev)

    dest, tail, has, block_e, first, n_used = _route_tables(route_i, cnt, t_all)
    xr = _dispatch(dest, tail, has, n_used, u2, _moe_blocks(t_all), tm=nbs)
    yr = _experts(block_e, first, n_used, xr, w_gate_up[l], bgu, w_down[l], bd)
    y_p, y_s = _final(dest, x1, yr, route_f, norm_final.reshape(1, D_MODEL), t_p, tm=nbs)

    return (y_p.reshape(nbp, seq, D_MODEL), y_s.reshape(nbs, 1, D_MODEL),
            p_lc[None], p_lh.reshape(1, nbp, D_RNN), p_sc[None],
            p_ss.reshape(1, nbp, SSD_HEADS, SSD_HEAD_DIM, D_STATE),
            s_lc.reshape(1, nbs, CONV_W - 1, D_RNN), s_lh[None],
            s_sc.reshape(1, nbs, CONV_W - 1, SSD_CONV_DIM),
            s_ss.reshape(1, nbs, SSD_HEADS, SSD_HEAD_DIM, D_STATE))
```
